```python
import jax, jax.numpy as jnp
from jax import lax
import numpy as np

D_MODEL = 1024
BATCH = 4
SEQ = 8192
DEPTH = 1
DEC_BATCH = 32
DEC_SEQ = 8
PAST_LEN = 16384
PAGE_SIZE = 128

H_ATT = 8
DH_ATT = 64
H_IDX = 8
D_IDX = 64
TOPK_MAX = 256
Q_BLOCK = 128
LRU_W = 512
LRU_BLOCKS = 8
LRU_BS = LRU_W // LRU_BLOCKS
CONV_W = 4
LRU_C = 8.0
H_MEM = 4
DH_MEM = 128
N_MEM = 256
D_FF = 2816
N_BRANCH = 3
EPS = 1e-6
IN_SIZES = (H_ATT * DH_ATT, H_ATT * DH_ATT, H_ATT * DH_ATT, H_IDX * D_IDX, D_IDX, H_IDX,
            LRU_W, LRU_W, H_MEM * DH_MEM, N_BRANCH * D_MODEL)
D_IN = sum(IN_SIZES)

kernel_name = "hybrid_dsa_rglru_memxattn_macaron_step"


def rmsnorm(x, g):
    xf = x.astype(jnp.float32)
    y = xf * lax.rsqrt(jnp.mean(xf * xf, axis=-1, keepdims=True) + EPS)
    return (y * g.astype(jnp.float32)).astype(x.dtype)


def swiglu(x, g, w_in, w_out):
    gate, up = jnp.split(rmsnorm(x, g) @ w_in, 2, axis=-1)
    return (jax.nn.silu(gate) * up) @ w_out


def split_in(h, w_in):
    offs, acc = [], 0
    for s in IN_SIZES[:-1]:
        acc += s
        offs.append(acc)
    return jnp.split(h @ w_in, offs, axis=-1)


def mix_inputs(x, lw):
    B, T = x.shape[:2]
    h = rmsnorm(x, lw["norm_mix_g"])
    q, k, v, qi, ki, wi, lx, lg, qm, gates = split_in(h, lw["w_in"])
    q = rmsnorm(q.reshape(B, T, H_ATT, DH_ATT), lw["q_norm_g"])
    k = rmsnorm(k.reshape(B, T, H_ATT, DH_ATT), lw["k_norm_g"])
    v = v.reshape(B, T, H_ATT, DH_ATT)
    qi = qi.reshape(B, T, H_IDX, D_IDX)
    qm = rmsnorm(qm.reshape(B, T, H_MEM, DH_MEM), lw["mem_q_norm_g"])
    return q, k, v, qi, ki, wi, lx, lg, qm, gates


def indexer_scores(qi, wi, kidx):
    dots = jnp.einsum("bthd,bsd->bths", qi, kidx, preferred_element_type=jnp.float32)
    w = wi.astype(jnp.float32) * (H_IDX ** -0.5 * D_IDX ** -0.5)
    return jnp.einsum("bths,bth->bts", jax.nn.relu(dots), w)


def sparse_core(q, k_sel, v_sel, valid):
    s = jnp.einsum("bthd,btkhd->bthk", q, k_sel, preferred_element_type=jnp.float32) * (DH_ATT ** -0.5)
    s = jnp.where(valid[:, :, None, :], s, -jnp.inf)
    p = jax.nn.softmax(s, axis=-1).astype(v_sel.dtype)
    o = jnp.einsum("bthk,btkhd->bthd", p, v_sel)
    return o.reshape(o.shape[0], o.shape[1], H_ATT * DH_ATT)


def dsa_prompt(q, k, v, qi, wi, kidx):
    B, S = q.shape[:2]
    topk = min(TOPK_MAX, S // 4)
    n_blk = S // Q_BLOCK
    bidx = jnp.arange(B)[:, None, None]
    key_pos = jnp.arange(S)

    def block(i):
        t0 = i * Q_BLOCK
        qb = lax.dynamic_slice_in_dim(q, t0, Q_BLOCK, axis=1)
        qib = lax.dynamic_slice_in_dim(qi, t0, Q_BLOCK, axis=1)
        wib = lax.dynamic_slice_in_dim(wi, t0, Q_BLOCK, axis=1)
        qpos = t0 + jnp.arange(Q_BLOCK)
        causal = key_pos[None, :] <= qpos[:, None]
        sc = jnp.where(causal[None], indexer_scores(qib, wib, kidx), -jnp.inf)
        _, idx = lax.top_k(sc, topk)
        valid = idx <= qpos[None, :, None]
        return sparse_core(qb, k[bidx, idx], v[bidx, idx], valid)

    o = lax.map(block, jnp.arange(n_blk))
    return jnp.moveaxis(o, 0, 1).reshape(B, S, H_ATT * DH_ATT)


def dsa_sample(q, k, v, qi, wi, kidx, cache_k, cache_v, cache_kidx, page_table):
    B, T = q.shape[:2]
    past = page_table.shape[1] * PAGE_SIZE
    n_keys = past + T
    topk = min(TOPK_MAX, n_keys // 4)
    kidx_past = cache_kidx[page_table].reshape(B, past, D_IDX)
    kidx_all = jnp.concatenate([kidx_past, kidx.astype(kidx_past.dtype)], axis=1)
    qpos = past + jnp.arange(T)
    causal = jnp.arange(n_keys)[None, :] <= qpos[:, None]
    sc = jnp.where(causal[None], indexer_scores(qi, wi, kidx_all), -jnp.inf)
    _, idx = lax.top_k(sc, topk)
    valid = idx <= qpos[None, :, None]
    bidx = jnp.arange(B)[:, None, None]
    pidx = jnp.minimum(idx, past - 1)
    phys = page_table[bidx, pidx // PAGE_SIZE]
    off = pidx % PAGE_SIZE
    nidx = jnp.clip(idx - past, 0, T - 1)
    is_new = (idx >= past)[..., None, None]
    k_sel = jnp.where(is_new, k[bidx, nidx].astype(cache_k.dtype), cache_k[phys, off])
    v_sel = jnp.where(is_new, v[bidx, nidx].astype(cache_v.dtype), cache_v[phys, off])
    return sparse_core(q, k_sel, v_sel, valid)


def rglru_branch(lx, lg, conv_buf, h_prev, lw):
    f32 = jnp.float32
    B, T = lx.shape[:2]
    xpad = jnp.concatenate([conv_buf.astype(lx.dtype), lx], axis=1)
    conv = lw["conv_b"].astype(f32)
    for j in range(CONV_W):
        conv = conv + xpad[:, j:j + T].astype(f32) * lw["conv_w"][j].astype(f32)
    new_buf = xpad[:, T:]
    xb = conv.reshape(B, T, LRU_BLOCKS, LRU_BS)
    r = jax.nn.sigmoid(jnp.einsum("btni,nij->btnj", xb, lw["lru_wa"].astype(f32)).reshape(B, T, LRU_W)
                       + lw["lru_ba"].astype(f32))
    i = jax.nn.sigmoid(jnp.einsum("btni,nij->btnj", xb, lw["lru_wi"].astype(f32)).reshape(B, T, LRU_W)
                       + lw["lru_bi"].astype(f32))
    log_a = -LRU_C * r * jax.nn.softplus(-lw["lru_lambda"].astype(f32))
    a = jnp.exp(log_a)
    b = jnp.sqrt(-jnp.expm1(2.0 * log_a)) * (i * conv)
    b = b.at[:, 0].add(a[:, 0] * h_prev.astype(f32))

    def comb(c1, c2):
        a1, b1 = c1
        a2, b2 = c2
        return a1 * a2, a2 * b1 + b2

    _, h = lax.associative_scan(comb, (a, b), axis=1)
    y = h * jax.nn.gelu(lg.astype(f32))
    return y.astype(lx.dtype), new_buf, h[:, -1].astype(lx.dtype)


def mem_kv(mem, lw):
    B, M = mem.shape[:2]
    mk, mv = jnp.split(rmsnorm(mem, lw["norm_mem_g"]) @ lw["w_mem_kv"], 2, axis=-1)
    mk = rmsnorm(mk.reshape(B, M, H_MEM, DH_MEM), lw["mem_k_norm_g"])
    return mk, mv.reshape(B, M, H_MEM, DH_MEM)


def mem_attend(qm, mk, mv):
    s = jnp.einsum("bthd,bmhd->bthm", qm, mk, preferred_element_type=jnp.float32) * (DH_MEM ** -0.5)
    p = jax.nn.softmax(s, axis=-1).astype(mv.dtype)
    o = jnp.einsum("bthm,bmhd->bthd", p, mv)
    return o.reshape(o.shape[0], o.shape[1], H_MEM * DH_MEM)


def merge(o_att, o_lru, o_mem, gates, lw):
    g = jax.nn.sigmoid(gates.astype(jnp.float32)).astype(o_att.dtype)
    g = g.reshape(gates.shape[0], gates.shape[1], N_BRANCH, D_MODEL)
    m = (g[:, :, 0] * (o_att @ lw["w_attn_o"]) + g[:, :, 1] * (o_lru @ lw["w_lru_o"])
         + g[:, :, 2] * (o_mem @ lw["w_mem_o"]))
    return m @ lw["w_out"]


def layer_prompt(x, mem, lw):
    x = x + 0.5 * swiglu(x, lw["norm_ffn1_g"], lw["w_ffn1_in"], lw["w_ffn1_out"])
    q, k, v, qi, ki, wi, lx, lg, qm, gates = mix_inputs(x, lw)
    o_att = dsa_prompt(q, k, v, qi, wi, ki)
    B = x.shape[0]
    buf0 = jnp.zeros((B, CONV_W - 1, LRU_W), x.dtype)
    h0 = jnp.zeros((B, LRU_W), jnp.float32)
    o_lru, conv_buf, h_last = rglru_branch(lx, lg, buf0, h0, lw)
    mk, mv = mem_kv(mem, lw)
    o_mem = mem_attend(qm, mk, mv)
    x = x + merge(o_att, o_lru, o_mem, gates, lw)
    x = x + 0.5 * swiglu(x, lw["norm_ffn2_g"], lw["w_ffn2_in"], lw["w_ffn2_out"])
    return x, (k, v, ki, mk, mv, conv_buf, h_last)


def layer_sample(x, ck, cv, cki, page_table, cmk, cmv, sconv, sh, lw):
    x = x + 0.5 * swiglu(x, lw["norm_ffn1_g"], lw["w_ffn1_in"], lw["w_ffn1_out"])
    q, k, v, qi, ki, wi, lx, lg, qm, gates = mix_inputs(x, lw)
    o_att = dsa_sample(q, k, v, qi, wi, ki, ck, cv, cki, page_table)
    o_lru, conv_buf, h_last = rglru_branch(lx, lg, sconv, sh, lw)
    o_mem = mem_attend(qm, cmk, cmv)
    x = x + merge(o_att, o_lru, o_mem, gates, lw)
    x = x + 0.5 * swiglu(x, lw["norm_ffn2_g"], lw["w_ffn2_in"], lw["w_ffn2_out"])
    return x, (k, v, ki, conv_buf, h_last)


def setup_inputs(seed: int = 0) -> dict:
    key = jax.random.key(seed)
    keys = iter(jax.random.split(key, 64))
    f32 = jnp.float32

    def normal(shape, scale):
        return scale * jax.random.normal(next(keys), shape, f32)

    def gain(shape):
        return 1.0 + 0.05 * jax.random.normal(next(keys), shape, f32)

    L = DEPTH
    n_pages = PAST_LEN // PAGE_SIZE
    n_used = DEC_BATCH * n_pages
    n_phys = n_used + max(1, n_used // 4)
    page_table = jax.random.permutation(next(keys), n_phys)[:n_used].reshape(DEC_BATCH, n_pages).astype(jnp.int32)
    hd = H_ATT * DH_ATT
    md = H_MEM * DH_MEM
    a0 = jax.random.uniform(next(keys), (L, LRU_W), f32, 0.9, 0.999)
    p = jnp.exp(jnp.log(a0) / LRU_C)
    lru_lambda = jnp.log(p) - jnp.log1p(-p)
    return {
        "x_prompt": normal((BATCH, SEQ, D_MODEL), 1.0),
        "x_sample": normal((DEC_BATCH, DEC_SEQ, D_MODEL), 1.0),
        "cache_k": normal((L, n_phys, PAGE_SIZE, H_ATT, DH_ATT), 1.0),
        "cache_v": normal((L, n_phys, PAGE_SIZE, H_ATT, DH_ATT), 1.0),
        "cache_kidx": normal((L, n_phys, PAGE_SIZE, D_IDX), 1.0),
        "page_table": page_table,
        "cache_mem_k": normal((L, DEC_BATCH, N_MEM, H_MEM, DH_MEM), 1.0),
        "cache_mem_v": normal((L, DEC_BATCH, N_MEM, H_MEM, DH_MEM), 1.0),
        "state_conv": normal((L, DEC_BATCH, CONV_W - 1, LRU_W), 1.0),
        "state_h": normal((L, DEC_BATCH, LRU_W), 0.5),
        "mem_prompt": normal((BATCH, N_MEM, D_MODEL), 1.0),
        "norm_ffn1_g": gain((L, D_MODEL)),
        "w_ffn1_in": normal((L, D_MODEL, 2 * D_FF), D_MODEL ** -0.5),
        "w_ffn1_out": normal((L, D_FF, D_MODEL), D_FF ** -0.5),
        "norm_mix_g": gain((L, D_MODEL)),
        "w_in": normal((L, D_MODEL, D_IN), D_MODEL ** -0.5),
        "q_norm_g": gain((L, DH_ATT)),
        "k_norm_g": gain((L, DH_ATT)),
        "w_attn_o": normal((L, hd, D_MODEL), hd ** -0.5),
        "conv_w": normal((L, CONV_W, LRU_W), CONV_W ** -0.5),
        "conv_b": normal((L, LRU_W), 0.01),
        "lru_wa": normal((L, LRU_BLOCKS, LRU_BS, LRU_BS), LRU_BS ** -0.5),
        "lru_ba": normal((L, LRU_W), 0.01),
        "lru_wi": normal((L, LRU_BLOCKS, LRU_BS, LRU_BS), LRU_BS ** -0.5),
        "lru_bi": normal((L, LRU_W), 0.01),
        "lru_lambda": lru_lambda,
        "w_lru_o": normal((L, LRU_W, D_MODEL), LRU_W ** -0.5),
        "norm_mem_g": gain((L, D_MODEL)),
        "w_mem_kv": normal((L, D_MODEL, 2 * md), D_MODEL ** -0.5),
        "mem_q_norm_g": gain((L, DH_MEM)),
        "mem_k_norm_g": gain((L, DH_MEM)),
        "w_mem_o": normal((L, md, D_MODEL), md ** -0.5),
        "w_out": normal((L, D_MODEL, D_MODEL), D_MODEL ** -0.5),
        "norm_ffn2_g": gain((L, D_MODEL)),
        "w_ffn2_in": normal((L, D_MODEL, 2 * D_FF), D_MODEL ** -0.5),
        "w_ffn2_out": normal((L, D_FF, D_MODEL), D_FF ** -0.5),
    }


def reference(x_prompt, x_sample, cache_k, cache_v, cache_kidx, page_table, cache_mem_k, cache_mem_v,
              state_conv, state_h, mem_prompt, norm_ffn1_g, w_ffn1_in, w_ffn1_out, norm_mix_g, w_in,
              q_norm_g, k_norm_g, w_attn_o, conv_w, conv_b, lru_wa, lru_ba, lru_wi, lru_bi, lru_lambda,
              w_lru_o, norm_mem_g, w_mem_kv, mem_q_norm_g, mem_k_norm_g, w_mem_o, w_out, norm_ffn2_g,
              w_ffn2_in, w_ffn2_out):
    xp, xs = x_prompt, x_sample
    p_states, s_states = [], []
    for l in range(DEPTH):
        lw = {
            "norm_ffn1_g": norm_ffn1_g[l], "w_ffn1_in": w_ffn1_in[l], "w_ffn1_out": w_ffn1_out[l],
            "norm_mix_g": norm_mix_g[l], "w_in": w_in[l], "q_norm_g": q_norm_g[l], "k_norm_g": k_norm_g[l],
            "w_attn_o": w_attn_o[l], "conv_w": conv_w[l], "conv_b": conv_b[l], "lru_wa": lru_wa[l],
            "lru_ba": lru_ba[l], "lru_wi": lru_wi[l], "lru_bi": lru_bi[l], "lru_lambda": lru_lambda[l],
            "w_lru_o": w_lru_o[l], "norm_mem_g": norm_mem_g[l], "w_mem_kv": w_mem_kv[l],
            "mem_q_norm_g": mem_q_norm_g[l], "mem_k_norm_g": mem_k_norm_g[l], "w_mem_o": w_mem_o[l],
            "w_out": w_out[l], "norm_ffn2_g": norm_ffn2_g[l], "w_ffn2_in": w_ffn2_in[l],
            "w_ffn2_out": w_ffn2_out[l],
        }
        xp, st_p = layer_prompt(xp, mem_prompt, lw)
        xs, st_s = layer_sample(xs, cache_k[l], cache_v[l], cache_kidx[l], page_table, cache_mem_k[l],
                                cache_mem_v[l], state_conv[l], state_h[l], lw)
        p_states.append(st_p)
        s_states.append(st_s)
    new_k_prompt = jnp.stack([s[0] for s in p_states])
    new_v_prompt = jnp.stack([s[1] for s in p_states])
    new_kidx_prompt = jnp.stack([s[2] for s in p_states])
    new_mem_k_prompt = jnp.stack([s[3] for s in p_states])
    new_mem_v_prompt = jnp.stack([s[4] for s in p_states])
    new_conv_prompt = jnp.stack([s[5] for s in p_states])
    new_h_prompt = jnp.stack([s[6] for s in p_states])
    new_k_sample = jnp.stack([s[0] for s in s_states])
    new_v_sample = jnp.stack([s[1] for s in s_states])
    new_kidx_sample = jnp.stack([s[2] for s in s_states])
    new_conv_sample = jnp.stack([s[3] for s in s_states])
    new_h_sample = jnp.stack([s[4] for s in s_states])
    return (xp, xs, new_k_prompt, new_v_prompt, new_kidx_prompt, new_mem_k_prompt, new_mem_v_prompt,
            new_conv_prompt, new_h_prompt, new_k_sample, new_v_sample, new_kidx_sample,
            new_conv_sample, new_h_sample)
```

```python
import functools

import jax
import jax.numpy as jnp
from jax import lax
from jax.experimental import pallas as pl
from jax.experimental.pallas import tpu as pltpu

F32, BF16, I32 = jnp.float32, jnp.bfloat16, jnp.int32

D_MODEL = 1024
H_ATT, DH_ATT = 8, 64
H_IDX, D_IDX = 8, 64
TOPK_MAX = 256
LRU_W, LRU_BLOCKS, CONV_W, LRU_C = 512, 8, 4, 8.0
H_MEM, DH_MEM = 4, 128
D_FF = 2816
EPS = 1e-6
HD = H_ATT * DH_ATT
MD = H_MEM * DH_MEM

LANES = 128
SUBLANES = 8
VMEM_BYTES_V7X = 64 * 1024 * 1024
VMEM_LIMIT = VMEM_BYTES_V7X - 8 * 1024 * 1024

FF_CHUNK = 256
TQ = 256
KB = 256
ROW_CHUNK = 64
INT_MIN = -2 ** 31
KEY_NEG_INF = -2139095041
NEG_INF = float("-inf")

NT_DIMS = (((1,), (1,)), ((), ()))


def _params(n_grid, parallel=True):
    sem = ("parallel" if parallel else "arbitrary",) * n_grid
    return pltpu.CompilerParams(dimension_semantics=sem, vmem_limit_bytes=VMEM_LIMIT)


def _const_spec(shape):
    nd = len(shape)
    return pl.BlockSpec(shape, lambda *_: (0,) * nd)


def _rms(x, g):
    ms = jnp.mean(x * x, axis=-1, keepdims=True)
    return x * lax.rsqrt(ms + EPS) * g


def _group_rms(x, gmat, g, group):
    x2 = x * x
    hi = x2.astype(BF16)
    lo = (x2 - hi.astype(F32)).astype(BF16)
    ss = jnp.dot(hi, gmat, preferred_element_type=F32) + jnp.dot(lo, gmat, preferred_element_type=F32)
    return x * lax.rsqrt(ss * (1.0 / group) + EPS) * g


def _group_matrix(width, group):
    idx = jnp.arange(width) // group
    return (idx[:, None] == idx[None, :]).astype(BF16)


def _ffn_kernel(x_ref, g_ref, wg_ref, wu_ref, wo_ref, o_ref):
    x = x_ref[...]
    hn = _rms(x, g_ref[...]).astype(BF16)
    acc = jnp.zeros_like(x)
    for c in range(wg_ref.shape[0]):
        gate = jnp.dot(hn, wg_ref[c], preferred_element_type=F32)
        up = jnp.dot(hn, wu_ref[c], preferred_element_type=F32)
        act = (gate * jax.nn.sigmoid(gate) * up).astype(BF16)
        acc = acc + jnp.dot(act, wo_ref[c], preferred_element_type=F32)
    o_ref[...] = x + 0.5 * acc


def _prep_ffn(g, w_in, w_out):
    nc = D_FF // FF_CHUNK
    wg = w_in[:, :D_FF].reshape(D_MODEL, nc, FF_CHUNK).transpose(1, 0, 2).astype(BF16)
    wu = w_in[:, D_FF:].reshape(D_MODEL, nc, FF_CHUNK).transpose(1, 0, 2).astype(BF16)
    wo = w_out.reshape(nc, FF_CHUNK, D_MODEL).astype(BF16)
    return g.reshape(1, D_MODEL), wg, wu, wo


def _ffn(x, prep, tm):
    g, wg, wu, wo = prep
    n = x.shape[0]
    row = pl.BlockSpec((tm, D_MODEL), lambda i: (i, 0))
    return pl.pallas_call(
        _ffn_kernel,
        grid=(n // tm,),
        in_specs=[row, _const_spec(g.shape), _const_spec(wg.shape), _const_spec(wu.shape), _const_spec(wo.shape)],
        out_specs=row,
        out_shape=jax.ShapeDtypeStruct((n, D_MODEL), F32),
        compiler_params=_params(1),
        name="ffn",
    )(x, g, wg, wu, wo)


def _proj_kernel(x_ref, g_ref, wqkv_ref, wqi_ref, wkw_ref, wl_ref, wqm_ref, qg_ref, kg_ref, mg_ref,
                 kwscale_ref, g64_ref, g128_ref,
                 k32_ref, v32_ref, kw32_ref, qb_ref, kb_ref, vb_ref, qib_ref, kwb_ref, lx_ref, lg_ref, qmb_ref):
    hn = _rms(x_ref[...], g_ref[...]).astype(BF16)
    qkv = jnp.dot(hn, wqkv_ref[...], preferred_element_type=F32)
    q = _group_rms(qkv[:, :HD], g64_ref[...], qg_ref[...], DH_ATT)
    k = _group_rms(qkv[:, HD:2 * HD], g64_ref[...], kg_ref[...], DH_ATT)
    v = qkv[:, 2 * HD:]
    k32_ref[...] = k
    v32_ref[...] = v
    qb_ref[...] = (q * (DH_ATT ** -0.5)).astype(BF16)
    kb_ref[...] = k.astype(BF16)
    vb_ref[...] = v.astype(BF16)
    qib_ref[...] = jnp.dot(hn, wqi_ref[...], preferred_element_type=F32).astype(BF16)
    kw = jnp.dot(hn, wkw_ref[...], preferred_element_type=F32) * kwscale_ref[...]
    kw32_ref[...] = kw
    kwb_ref[...] = kw.astype(BF16)
    lxg = jnp.dot(hn, wl_ref[...], preferred_element_type=F32)
    lx_ref[...] = lxg[:, :LRU_W]
    lg_ref[...] = lxg[:, LRU_W:]
    qm = jnp.dot(hn, wqm_ref[...], preferred_element_type=F32)
    qmb_ref[...] = _group_rms(qm, g128_ref[...], mg_ref[...], DH_MEM).astype(BF16)


def _prep_proj(norm_g, w_in, q_norm_g, k_norm_g, mem_q_norm_g):
    o = 0
    cols = {}
    for name, size in (("q", HD), ("k", HD), ("v", HD), ("qi", H_IDX * D_IDX), ("ki", D_IDX), ("wi", H_IDX),
                       ("lx", LRU_W), ("lg", LRU_W), ("qm", MD), ("gates", 3 * D_MODEL)):
        cols[name] = w_in[:, o:o + size]
        o += size
    pad = jnp.zeros((D_MODEL, LANES - D_IDX - H_IDX), w_in.dtype)
    wqkv = jnp.concatenate([cols["q"], cols["k"], cols["v"]], axis=1).astype(BF16)
    wkw = jnp.concatenate([cols["ki"], cols["wi"], pad], axis=1).astype(BF16)
    wl = jnp.concatenate([cols["lx"], cols["lg"]], axis=1).astype(BF16)
    kwscale = jnp.concatenate([jnp.ones((D_IDX,), F32),
                               jnp.full((H_IDX,), H_IDX ** -0.5 * D_IDX ** -0.5, F32),
                               jnp.zeros((LANES - D_IDX - H_IDX,), F32)]).reshape(1, LANES)
    return dict(
        g=norm_g.reshape(1, D_MODEL), wqkv=wqkv, wqi=cols["qi"].astype(BF16), wkw=wkw, wl=wl,
        wqm=cols["qm"].astype(BF16),
        qg=jnp.tile(q_norm_g, H_ATT).reshape(1, HD), kg=jnp.tile(k_norm_g, H_ATT).reshape(1, HD),
        mg=jnp.tile(mem_q_norm_g, H_MEM).reshape(1, MD), kwscale=kwscale,
        g64=_group_matrix(HD, DH_ATT), g128=_group_matrix(MD, DH_MEM),
        wgates=cols["gates"].astype(BF16),
    )


def _proj(x, p, tm):
    n = x.shape[0]
    consts = [p[k] for k in ("g", "wqkv", "wqi", "wkw", "wl", "wqm", "qg", "kg", "mg", "kwscale", "g64", "g128")]

    def row(w):
        return pl.BlockSpec((tm, w), lambda i: (i, 0))

    outs = [(HD, F32), (HD, F32), (LANES, F32), (HD, BF16), (HD, BF16), (HD, BF16), (HD, BF16), (LANES, BF16),
            (LRU_W, F32), (LRU_W, F32), (MD, BF16)]
    return pl.pallas_call(
        _proj_kernel,
        grid=(n // tm,),
        in_specs=[row(D_MODEL)] + [_const_spec(c.shape) for c in consts],
        out_specs=[row(w) for w, _ in outs],
        out_shape=[jax.ShapeDtypeStruct((n, w), dt) for w, dt in outs],
        compiler_params=_params(1),
        name="proj",
    )(x, *consts)


def _memkv_kernel(m_ref, g_ref, w_ref, kg_ref, g128_ref, mk_ref, mv_ref):
    hn = _rms(m_ref[...], g_ref[...]).astype(BF16)
    kv = jnp.dot(hn, w_ref[...], preferred_element_type=F32)
    mk_ref[...] = _group_rms(kv[:, :MD], g128_ref[...], kg_ref[...], DH_MEM)
    mv_ref[...] = kv[:, MD:]


def _memkv(mem, norm_g, w_mem_kv, mem_k_norm_g, g128):
    n = mem.shape[0]
    tm = min(n, 512)
    consts = [norm_g.reshape(1, D_MODEL), w_mem_kv.astype(BF16), jnp.tile(mem_k_norm_g, H_MEM).reshape(1, MD), g128]
    row = lambda w: pl.BlockSpec((tm, w), lambda i: (i, 0))
    return pl.pallas_call(
        _memkv_kernel,
        grid=(n // tm,),
        in_specs=[row(D_MODEL)] + [_const_spec(c.shape) for c in consts],
        out_specs=[row(MD), row(MD)],
        out_shape=[jax.ShapeDtypeStruct((n, MD), F32)] * 2,
        compiler_params=_params(1),
        name="memkv",
    )(mem, *consts)


def _memattn_kernel(q_ref, mk_ref, mv_ref, o_ref):
    q = q_ref[0]
    mk = mk_ref[0].astype(BF16)
    mv = mv_ref[0].astype(BF16)
    for h in range(H_MEM):
        sl = slice(h * DH_MEM, (h + 1) * DH_MEM)
        s = lax.dot_general(q[:, sl], mk[:, sl], NT_DIMS, preferred_element_type=F32) * (DH_MEM ** -0.5)
        m = jnp.max(s, axis=-1, keepdims=True)
        e = jnp.exp(s - m)
        p = (e / jnp.sum(e, axis=-1, keepdims=True)).astype(BF16)
        o_ref[0, :, sl] = jnp.dot(p, mv[:, sl], preferred_element_type=F32).astype(BF16)


def _memattn(qm, mk, mv, tm):
    b, t, _ = qm.shape
    n_mem = mk.shape[1]
    return pl.pallas_call(
        _memattn_kernel,
        grid=(b, t // tm),
        in_specs=[pl.BlockSpec((1, tm, MD), lambda i, j: (i, j, 0)),
                  pl.BlockSpec((1, n_mem, MD), lambda i, j: (i, 0, 0)),
                  pl.BlockSpec((1, n_mem, MD), lambda i, j: (i, 0, 0))],
        out_specs=pl.BlockSpec((1, tm, MD), lambda i, j: (i, j, 0)),
        out_shape=jax.ShapeDtypeStruct((b, t, MD), BF16),
        compiler_params=_params(2),
        name="memattn",
    )(qm, mk, mv)


def _shift_rows(x, k, fill):
    rows = lax.broadcasted_iota(I32, x.shape, 0)
    return jnp.where(rows >= k, pltpu.roll(x, k, 0), fill)


def _rglru_kernel(lx_ref, lg_ref, cs_ref, h0_ref, cw_ref, cb_ref, wa_ref, ba_ref, wi_ref, bi_ref, lam_ref,
                  y_ref, nb_ref, hl_ref, tail_ref, h_ref):
    t = pl.program_id(1)

    @pl.when(t == 0)
    def _():
        tail_ref[...] = cs_ref[0]
        h_ref[...] = h0_ref[0]

    x = lx_ref[0]
    tt = x.shape[0]
    tail = tail_ref[...]
    rows8 = lax.broadcasted_iota(I32, tail.shape, 0)
    conv = cb_ref[...] + x * cw_ref[CONV_W - 1:CONV_W, :]
    for d in range(1, CONV_W):
        xs = pltpu.roll(x, d, 0)
        head = jnp.where(rows8 < d, pltpu.roll(tail, d, 0), xs[:SUBLANES])
        xs = head if tt == SUBLANES else jnp.concatenate([head, xs[SUBLANES:]], axis=0)
        conv = conv + xs * cw_ref[CONV_W - 1 - d:CONV_W - d, :]
    tail_ref[...] = x[tt - SUBLANES:]
    nb_ref[0] = x[tt - SUBLANES:]

    cb16 = conv.astype(BF16)
    r = jax.nn.sigmoid(jnp.dot(cb16, wa_ref[...], preferred_element_type=F32) + ba_ref[...])
    ig = jax.nn.sigmoid(jnp.dot(cb16, wi_ref[...], preferred_element_type=F32) + bi_ref[...])
    nl = -lam_ref[...]
    softplus = jnp.maximum(nl, 0.0) + jnp.log1p(jnp.exp(-jnp.abs(nl)))
    log_a = -LRU_C * r * softplus
    a = jnp.exp(log_a)
    b = jnp.sqrt(-jnp.tanh(log_a) * (jnp.exp(2.0 * log_a) + 1.0)) * (ig * conv)
    rows = lax.broadcasted_iota(I32, x.shape, 0)
    b = b + jnp.where(rows == 0, a * h_ref[SUBLANES - 1:SUBLANES, :], 0.0)
    k = 1
    while k < tt:
        b = a * _shift_rows(b, k, 0.0) + b
        a = a * _shift_rows(a, k, 1.0)
        k *= 2
    h = b
    h_ref[...] = h[tt - SUBLANES:]
    hl_ref[0] = h[tt - SUBLANES:]
    y_ref[0] = (h * jax.nn.gelu(lg_ref[0])).astype(BF16)


def _block_diag(w):
    nb, bs, _ = w.shape
    eye = jnp.eye(nb, dtype=w.dtype)
    return (eye[:, None, :, None] * w[:, :, None, :]).reshape(nb * bs, nb * bs)


def _prep_rglru(conv_w, conv_b, lru_wa, lru_ba, lru_wi, lru_bi, lru_lambda):
    r = lambda v: v.reshape(1, LRU_W)
    return [conv_w, r(conv_b), _block_diag(lru_wa).astype(BF16), r(lru_ba), _block_diag(lru_wi).astype(BF16),
            r(lru_bi), r(lru_lambda)]


def _rglru(lx, lg, conv_state, h0, consts, tt):
    b, t, w = lx.shape
    cs = jnp.concatenate([jnp.zeros((b, SUBLANES - (CONV_W - 1), w), F32), conv_state], axis=1)
    h0p = jnp.concatenate([jnp.zeros((b, SUBLANES - 1, w), F32), h0[:, None, :]], axis=1)
    seq = pl.BlockSpec((1, tt, w), lambda i, j: (i, j, 0))
    st = pl.BlockSpec((1, SUBLANES, w), lambda i, j: (i, 0, 0))
    y, nb, hl = pl.pallas_call(
        _rglru_kernel,
        grid=(b, t // tt),
        in_specs=[seq, seq, st, st] + [_const_spec(c.shape) for c in consts],
        out_specs=[seq, st, st],
        out_shape=[jax.ShapeDtypeStruct((b, t, w), BF16), jax.ShapeDtypeStruct((b, SUBLANES, w), F32),
                   jax.ShapeDtypeStruct((b, SUBLANES, w), F32)],
        scratch_shapes=[pltpu.VMEM((SUBLANES, w), F32), pltpu.VMEM((SUBLANES, w), F32)],
        compiler_params=pltpu.CompilerParams(dimension_semantics=("parallel", "arbitrary"),
                                             vmem_limit_bytes=VMEM_LIMIT),
        name="rglru",
    )(lx, lg, cs, h0p, *consts)
    return y, nb[:, SUBLANES - (CONV_W - 1):], hl[:, SUBLANES - 1]


def _score_key(score):
    bits = lax.bitcast_convert_type(score, I32)
    return bits ^ ((bits >> 31) & 0x7FFFFFFF)


def _radix_threshold(count_ge, rows, topk):
    def bit_step(i, thr_u):
        cand_u = thr_u | (jnp.int32(1) << (31 - i))
        cnt = count_ge(cand_u ^ INT_MIN)
        return jnp.where(cnt >= topk, cand_u, thr_u)

    thr_u = lax.fori_loop(0, 32, bit_step, jnp.zeros((rows, 1), I32))
    return jnp.maximum(thr_u ^ INT_MIN, KEY_NEG_INF + 1)


def _dsa_prompt_kernel(q_ref, qi_ref, kwq_ref, kik_ref, k_ref, v_ref, tri_ref, o_ref, sc_ref, *, topk):
    i = pl.program_id(1)
    nk = i + 1
    row = lax.broadcasted_iota(I32, (TQ, KB), 0)
    col = lax.broadcasted_iota(I32, (TQ, KB), 1)
    w = kwq_ref[0][:, D_IDX:D_IDX + H_IDX]
    qi = qi_ref[0]

    def score_block(j, carry):
        kib = kik_ref[0, pl.ds(pl.multiple_of(j * KB, KB), KB), :][:, :D_IDX]
        acc = jnp.zeros((TQ, KB), F32)
        for h in range(H_IDX):
            d = lax.dot_general(qi[:, h * D_IDX:(h + 1) * D_IDX], kib, NT_DIMS, preferred_element_type=F32)
            acc = acc + jnp.maximum(d, 0.0) * w[:, h:h + 1]
        acc = jnp.where(col + j * KB <= row + i * TQ, acc, NEG_INF)
        sc_ref[j] = _score_key(acc)
        return carry

    lax.fori_loop(0, nk, score_block, 0)

    def count(rc, cmp_fn):
        rsl = slice(rc * ROW_CHUNK, (rc + 1) * ROW_CHUNK)

        def body(j, cnt):
            m = cmp_fn(sc_ref[j, rsl, :]).astype(I32)
            for c in range(KB // LANES):
                cnt = cnt + m[:, c * LANES:(c + 1) * LANES]
            return cnt

        cnt = lax.fori_loop(0, nk, body, jnp.zeros((ROW_CHUNK, LANES), I32))
        return jnp.sum(cnt, axis=1, keepdims=True)

    thr_parts, quota_parts = [], []
    for rc in range(TQ // ROW_CHUNK):
        thr_c = _radix_threshold(lambda c: count(rc, lambda key: key >= c), ROW_CHUNK, topk)
        thr_parts.append(thr_c)
        quota_parts.append(topk - count(rc, lambda key: key > thr_c))
    thr = jnp.concatenate(thr_parts, axis=0)
    quota = jnp.concatenate(quota_parts, axis=0).astype(F32)

    def bias_block(j, ties_before):
        key = sc_ref[j]
        eq = key == thr
        eqf = eq.astype(F32)
        rank = jnp.dot(eqf.astype(BF16), tri_ref[...], preferred_element_type=F32) + ties_before
        sel = (key > thr) | (eq & (rank < quota))
        sc_ref[j] = lax.bitcast_convert_type(jnp.where(sel, 0.0, NEG_INF), I32)
        return ties_before + jnp.sum(eqf, axis=1, keepdims=True)

    lax.fori_loop(0, nk, bias_block, jnp.zeros((TQ, 1), F32))

    q = q_ref[0]
    for h in range(H_ATT):
        hs = slice(h * DH_ATT, (h + 1) * DH_ATT)
        qh = q[:, hs]

        def attend(j, carry):
            m, l, acc = carry
            ks = k_ref[0, pl.ds(pl.multiple_of(j * KB, KB), KB), hs]
            vs = v_ref[0, pl.ds(pl.multiple_of(j * KB, KB), KB), hs]
            s = lax.dot_general(qh, ks, NT_DIMS, preferred_element_type=F32)
            s = s + lax.bitcast_convert_type(sc_ref[j], F32)
            m_new = jnp.maximum(m, jnp.max(s, axis=1, keepdims=True))
            m_safe = jnp.where(m_new == NEG_INF, 0.0, m_new)
            alpha = jnp.exp(m - m_safe)
            p = jnp.exp(s - m_safe)
            l = alpha * l + jnp.sum(p, axis=1, keepdims=True)
            acc = alpha * acc + jnp.dot(p.astype(BF16), vs, preferred_element_type=F32)
            return m_new, l, acc

        init = (jnp.full((TQ, 1), NEG_INF, F32), jnp.zeros((TQ, 1), F32), jnp.zeros((TQ, DH_ATT), F32))
        _, l, acc = lax.fori_loop(0, nk, attend, init)
        o_ref[0, :, hs] = (acc / l).astype(BF16)


def _dsa_prompt(qb, qib, kw32, kwb, kb, vb):
    b, s, _ = qb.shape
    topk = min(TOPK_MAX, s // 4)
    tri = (jnp.arange(KB)[:, None] < jnp.arange(KB)[None, :]).astype(BF16)
    qspec = lambda w: pl.BlockSpec((1, TQ, w), lambda i, j: (i, j, 0))
    full = lambda w: pl.BlockSpec((1, s, w), lambda i, j: (i, 0, 0))
    return pl.pallas_call(
        functools.partial(_dsa_prompt_kernel, topk=topk),
        grid=(b, s // TQ),
        in_specs=[qspec(HD), qspec(HD), qspec(LANES), full(LANES), full(HD), full(HD), _const_spec(tri.shape)],
        out_specs=qspec(HD),
        out_shape=jax.ShapeDtypeStruct((b, s, HD), BF16),
        scratch_shapes=[pltpu.VMEM((s // KB, TQ, KB), I32)],
        compiler_params=_params(2),
        name="dsa_prompt",
    )(qb, qib, kw32, kwb, kb, vb, tri)


PAGE_GROUP_IDX = 16
PAGE_GROUP_KV = 4


def _dsa_sample_select_kernel(pt_ref, qi_ref, kwq_ref, tri_ref, *rest, n_pages, topk, group):
    page_refs, (bias_ref, biasn_ref) = rest[:group], rest[group:]
    pg = pl.program_id(1)
    t = qi_ref.shape[1]
    qi = qi_ref[0]
    kwq = kwq_ref[0]
    w = kwq[:, D_IDX:D_IDX + H_IDX]

    def scores(keys_bf16):
        acc = jnp.zeros((t, keys_bf16.shape[0]), F32)
        for h in range(H_IDX):
            d = lax.dot_general(qi[:, h * D_IDX:(h + 1) * D_IDX], keys_bf16, NT_DIMS, preferred_element_type=F32)
            acc = acc + jnp.maximum(d, 0.0) * w[:, h:h + 1]
        return acc

    for g in range(group):
        bias_ref[0, pg * group + g] = _score_key(scores(page_refs[g][0].astype(BF16)))

    @pl.when(pg == pl.num_programs(1) - 1)
    def _():
        new_keys = jnp.concatenate([kwq[:, :D_IDX], jnp.zeros((LANES - t, D_IDX), F32)], axis=0).astype(BF16)
        row = lax.broadcasted_iota(I32, (t, LANES), 0)
        col = lax.broadcasted_iota(I32, (t, LANES), 1)
        key_new = _score_key(jnp.where(col <= row, scores(new_keys), NEG_INF))

        def count(cmp_fn):
            def body(p, cnt):
                return cnt + cmp_fn(bias_ref[0, p]).astype(I32)

            cnt = lax.fori_loop(0, n_pages, body, cmp_fn(key_new).astype(I32))
            return jnp.sum(cnt, axis=1, keepdims=True)

        thr = _radix_threshold(lambda c: count(lambda key: key >= c), t, topk)
        quota = (topk - count(lambda key: key > thr)).astype(F32)

        def select(key, ties_before):
            eq = key == thr
            eqf = eq.astype(F32)
            rank = jnp.dot(eqf.astype(BF16), tri_ref[...], preferred_element_type=F32) + ties_before
            sel = (key > thr) | (eq & (rank < quota))
            bias = lax.bitcast_convert_type(jnp.where(sel, 0.0, NEG_INF), I32)
            return bias, ties_before + jnp.sum(eqf, axis=1, keepdims=True)

        def bias_page(p, ties_before):
            bias, ties = select(bias_ref[0, p], ties_before)
            bias_ref[0, p] = bias
            return ties

        ties = lax.fori_loop(0, n_pages, bias_page, jnp.zeros((t, 1), F32))
        biasn_ref[0], _ = select(key_new, ties)


def _dsa_sample_select(page_table, qib, kw32, cache_kidx):
    b, t, _ = qib.shape
    n_pages = page_table.shape[1]
    page = cache_kidx.shape[1]
    topk = min(TOPK_MAX, (n_pages * page + t) // 4)
    group = PAGE_GROUP_IDX
    tri = (jnp.arange(page)[:, None] < jnp.arange(page)[None, :]).astype(BF16)
    tok = lambda w: pl.BlockSpec((1, t, w), lambda i, j, pt: (i, 0, 0))
    page_specs = [pl.BlockSpec((1, page, D_IDX), lambda i, j, pt, g=g: (pt[i, j * group + g], 0, 0))
                  for g in range(group)]
    grid_spec = pltpu.PrefetchScalarGridSpec(
        num_scalar_prefetch=1,
        grid=(b, n_pages // group),
        in_specs=[tok(H_IDX * D_IDX), tok(LANES), pl.BlockSpec(tri.shape, lambda i, j, pt: (0, 0))] + page_specs,
        out_specs=[pl.BlockSpec((1, n_pages, t, page), lambda i, j, pt: (i, 0, 0, 0)),
                   pl.BlockSpec((1, t, LANES), lambda i, j, pt: (i, 0, 0))],
    )
    return pl.pallas_call(
        functools.partial(_dsa_sample_select_kernel, n_pages=n_pages, topk=topk, group=group),
        grid_spec=grid_spec,
        out_shape=[jax.ShapeDtypeStruct((b, n_pages, t, page), I32), jax.ShapeDtypeStruct((b, t, LANES), I32)],
        compiler_params=pltpu.CompilerParams(dimension_semantics=("parallel", "arbitrary"),
                                             vmem_limit_bytes=VMEM_LIMIT),
        name="dsa_sample_select",
    )(page_table, qib, kw32, tri, *([cache_kidx] * group))


def _dsa_sample_attend_kernel(pt_ref, q_ref, kn_ref, vn_ref, bias_ref, biasn_ref, *rest, group):
    k_refs, v_refs = rest[:group], rest[group:2 * group]
    o_ref, m_ref, l_ref, acc_ref = rest[2 * group:]
    pg = pl.program_id(1)
    t = q_ref.shape[1]
    lane_head = lax.broadcasted_iota(I32, (t, HD), 1) // DH_ATT
    q = q_ref[0].astype(F32)
    qbd = jnp.concatenate([jnp.where(lane_head == h, q, 0.0) for h in range(H_ATT)], axis=0).astype(BF16)

    @pl.when(pg == 0)
    def _():
        m_ref[...] = jnp.full(m_ref.shape, NEG_INF, F32)
        l_ref[...] = jnp.zeros(l_ref.shape, F32)
        acc_ref[...] = jnp.zeros(acc_ref.shape, F32)

    def update(keys, vals, bias):
        s = lax.dot_general(qbd, keys, NT_DIMS, preferred_element_type=F32)
        s = s + jnp.concatenate([lax.bitcast_convert_type(bias, F32)] * H_ATT, axis=0)
        m = m_ref[...]
        m_new = jnp.maximum(m, jnp.max(s, axis=1, keepdims=True))
        m_safe = jnp.where(m_new == NEG_INF, 0.0, m_new)
        alpha = jnp.exp(m - m_safe)
        p = jnp.exp(s - m_safe)
        l_ref[...] = alpha * l_ref[...] + jnp.sum(p, axis=1, keepdims=True)
        acc_ref[...] = alpha * acc_ref[...] + jnp.dot(p.astype(BF16), vals, preferred_element_type=F32)
        m_ref[...] = m_new

    for g in range(group):
        update(k_refs[g][0].astype(BF16), v_refs[g][0].astype(BF16), bias_ref[0, g])

    @pl.when(pg == pl.num_programs(1) - 1)
    def _():
        pad = jnp.zeros((LANES - t, HD), F32)
        update(jnp.concatenate([kn_ref[0], pad], axis=0).astype(BF16),
               jnp.concatenate([vn_ref[0], pad], axis=0).astype(BF16), biasn_ref[0])
        o = acc_ref[...] / l_ref[...]
        out = jnp.zeros((t, HD), F32)
        for h in range(H_ATT):
            out = out + jnp.where(lane_head == h, o[h * t:(h + 1) * t], 0.0)
        o_ref[0] = out.astype(BF16)


def _dsa_sample_attend(page_table, qb, k32, v32, bias, bias_new, cache_k, cache_v):
    b, t, _ = qb.shape
    n_pages = page_table.shape[1]
    page = cache_k.shape[1]
    group = PAGE_GROUP_KV
    tok = lambda w: pl.BlockSpec((1, t, w), lambda i, j, pt: (i, 0, 0))
    kv_specs = [pl.BlockSpec((1, page, HD), lambda i, j, pt, g=g: (pt[i, j * group + g], 0, 0))
                for g in range(group)]
    grid_spec = pltpu.PrefetchScalarGridSpec(
        num_scalar_prefetch=1,
        grid=(b, n_pages // group),
        in_specs=[tok(HD), tok(HD), tok(HD),
                  pl.BlockSpec((1, group, t, page), lambda i, j, pt: (i, j, 0, 0)),
                  tok(LANES)] + kv_specs + kv_specs,
        out_specs=tok(HD),
        scratch_shapes=[pltpu.VMEM((H_ATT * t, 1), F32), pltpu.VMEM((H_ATT * t, 1), F32),
                        pltpu.VMEM((H_ATT * t, HD), F32)],
    )
    return pl.pallas_call(
        functools.partial(_dsa_sample_attend_kernel, group=group),
        grid_spec=grid_spec,
        out_shape=jax.ShapeDtypeStruct((b, t, HD), BF16),
        compiler_params=pltpu.CompilerParams(dimension_semantics=("parallel", "arbitrary"),
                                             vmem_limit_bytes=VMEM_LIMIT),
        name="dsa_sample_attend",
    )(page_table, qb, k32, v32, bias, bias_new, *([cache_k] * group), *([cache_v] * group))


def _merge_kernel(x_ref, oa_ref, ol_ref, om_ref, g_ref, wg_ref, wa_ref, wl_ref, wm_ref, wo_ref, o_ref):
    x = x_ref[...]
    hn = _rms(x, g_ref[...]).astype(BF16)
    m = jnp.zeros_like(x)
    for idx, (o_r, w_r) in enumerate(((oa_ref, wa_ref), (ol_ref, wl_ref), (om_ref, wm_ref))):
        gate = jax.nn.sigmoid(jnp.dot(hn, wg_ref[:, idx * D_MODEL:(idx + 1) * D_MODEL], preferred_element_type=F32))
        m = m + gate * jnp.dot(o_r[...], w_r[...], preferred_element_type=F32)
    o_ref[...] = x + jnp.dot(m.astype(BF16), wo_ref[...], preferred_element_type=F32)


def _merge(x, o_att, o_lru, o_mem, consts, tm):
    n = x.shape[0]
    row = lambda w: pl.BlockSpec((tm, w), lambda i: (i, 0))
    return pl.pallas_call(
        _merge_kernel,
        grid=(n // tm,),
        in_specs=[row(D_MODEL), row(HD), row(LRU_W), row(MD)] + [_const_spec(c.shape) for c in consts],
        out_specs=row(D_MODEL),
        out_shape=jax.ShapeDtypeStruct((n, D_MODEL), F32),
        compiler_params=_params(1),
        name="merge",
    )(x, o_att, o_lru, o_mem, *consts)


def _token_tile(n):
    return min(n, 512)


def _layer(x, is_prompt, lw, pp, extra):
    b, t, _ = x.shape
    n = b * t
    tm = _token_tile(n)
    x1 = _ffn(x.reshape(n, D_MODEL), pp["ffn1"], tm)
    k32, v32, kw32, qb, kb, vb, qib, kwb, lx, lg, qmb = _proj(x1, pp["proj"], tm)
    r3 = lambda a: a.reshape(b, t, a.shape[-1])
    if is_prompt:
        o_att = _dsa_prompt(r3(qb), r3(qib), r3(kw32), r3(kwb), r3(kb), r3(vb))
        conv_state = jnp.zeros((b, CONV_W - 1, LRU_W), F32)
        h0 = jnp.zeros((b, LRU_W), F32)
        mem = extra["mem"]
        mk, mv = _memkv(mem.reshape(-1, D_MODEL), lw["norm_mem_g"], lw["w_mem_kv"], lw["mem_k_norm_g"],
                        pp["proj"]["g128"])
        mk = mk.reshape(b, -1, MD)
        mv = mv.reshape(b, -1, MD)
    else:
        pt = extra["page_table"]
        bias, bias_new = _dsa_sample_select(pt, r3(qib), r3(kw32), extra["cache_kidx"])
        o_att = _dsa_sample_attend(pt, r3(qb), r3(k32), r3(v32), bias, bias_new, extra["cache_k"], extra["cache_v"])
        conv_state, h0 = extra["state_conv"], extra["state_h"]
        mk, mv = extra["cache_mem_k"], extra["cache_mem_v"]
    o_lru, conv_buf, h_last = _rglru(r3(lx), r3(lg), conv_state, h0, pp["rglru"], min(t, 256))
    o_mem = _memattn(r3(qmb), mk, mv, min(t, 512))
    x2 = _merge(x1, o_att.reshape(n, HD), o_lru.reshape(n, LRU_W), o_mem.reshape(n, MD), pp["merge"], tm)
    y = _ffn(x2, pp["ffn2"], tm).reshape(b, t, D_MODEL)
    k_new = k32.reshape(b, t, H_ATT, DH_ATT)
    v_new = v32.reshape(b, t, H_ATT, DH_ATT)
    ki_new = r3(kw32)[:, :, :D_IDX]
    if is_prompt:
        state = (k_new, v_new, ki_new, mk.reshape(b, -1, H_MEM, DH_MEM), mv.reshape(b, -1, H_MEM, DH_MEM),
                 conv_buf, h_last)
    else:
        state = (k_new, v_new, ki_new, conv_buf, h_last)
    return y, state


def kernel(x_prompt, x_sample, cache_k, cache_v, cache_kidx, page_table, cache_mem_k, cache_mem_v, state_conv, state_h, mem_prompt, norm_ffn1_g, w_ffn1_in, w_ffn1_out, norm_mix_g, w_in, q_norm_g, k_norm_g, w_attn_o, conv_w, conv_b, lru_wa, lru_ba, lru_wi, lru_bi, lru_lambda, w_lru_o, norm_mem_g, w_mem_kv, mem_q_norm_g, mem_k_norm_g, w_mem_o, w_out, norm_ffn2_g, w_ffn2_in, w_ffn2_out):
    depth = w_in.shape[0]
    n_phys, page = cache_k.shape[1], cache_k.shape[2]
    xp, xs = x_prompt, x_sample
    p_states, s_states = [], []
    for l in range(depth):
        proj = _prep_proj(norm_mix_g[l], w_in[l], q_norm_g[l], k_norm_g[l], mem_q_norm_g[l])
        pp = dict(
            ffn1=_prep_ffn(norm_ffn1_g[l], w_ffn1_in[l], w_ffn1_out[l]),
            ffn2=_prep_ffn(norm_ffn2_g[l], w_ffn2_in[l], w_ffn2_out[l]),
            proj=proj,
            rglru=_prep_rglru(conv_w[l], conv_b[l], lru_wa[l], lru_ba[l], lru_wi[l], lru_bi[l], lru_lambda[l]),
            merge=[proj["g"], proj["wgates"], w_attn_o[l].astype(BF16), w_lru_o[l].astype(BF16),
                   w_mem_o[l].astype(BF16), w_out[l].astype(BF16)],
        )
        lw = dict(norm_mem_g=norm_mem_g[l], w_mem_kv=w_mem_kv[l], mem_k_norm_g=mem_k_norm_g[l])
        xp, st_p = _layer(xp, True, lw, pp, dict(mem=mem_prompt))
        xs, st_s = _layer(xs, False, lw, pp, dict(
            page_table=page_table,
            cache_k=cache_k[l].reshape(n_phys, page, HD), cache_v=cache_v[l].reshape(n_phys, page, HD),
            cache_kidx=cache_kidx[l], cache_mem_k=cache_mem_k[l].reshape(-1, cache_mem_k.shape[2], MD),
            cache_mem_v=cache_mem_v[l].reshape(-1, cache_mem_v.shape[2], MD),
            state_conv=state_conv[l], state_h=state_h[l]))
        p_states.append(st_p)
        s_states.append(st_s)
    stack = lambda states, i: jnp.stack([s[i] for s in states])
    return (xp, xs) + tuple(stack(p_states, i) for i in range(7)) + tuple(stack(s_states, i) for i in range(5))
```

```python
import functools

import jax
import jax.numpy as jnp
from jax import lax
from jax.experimental import pallas as pl
from jax.experimental.pallas import tpu as pltpu

F32, BF16, I32 = jnp.float32, jnp.bfloat16, jnp.int32

D_MODEL = 1024
H_ATT, DH_ATT = 8, 64
H_IDX, D_IDX = 8, 64
TOPK_MAX = 256
LRU_W, LRU_BLOCKS, CONV_W, LRU_C = 512, 8, 4, 8.0
H_MEM, DH_MEM = 4, 128
D_FF = 2816
EPS = 1e-6
HD = H_ATT * DH_ATT
MD = H_MEM * DH_MEM

LANES = 128
SUBLANES = 8
VMEM_BYTES_V7X = 64 * 1024 * 1024
VMEM_LIMIT = VMEM_BYTES_V7X - 8 * 1024 * 1024

FF_CHUNK = 256
TQ = 256
KB = 256
CNT_ROWS = 32
BISECT_STEPS = 28
LOG2E = 1.4426950408889634
INT_MIN = -2 ** 31
KEY_NEG_INF = -2139095041
NEG_INF = float("-inf")

NT_DIMS = (((1,), (1,)), ((), ()))


def _params(n_grid, parallel=True):
    sem = ("parallel" if parallel else "arbitrary",) * n_grid
    return pltpu.CompilerParams(dimension_semantics=sem, vmem_limit_bytes=VMEM_LIMIT)


def _const_spec(shape):
    nd = len(shape)
    return pl.BlockSpec(shape, lambda *_: (0,) * nd)


def _rms(x, g):
    ms = jnp.mean(x * x, axis=-1, keepdims=True)
    return x * lax.rsqrt(ms + EPS) * g


def _group_rms(x, gmat, g, group):
    x2 = x * x
    hi = x2.astype(BF16)
    lo = (x2 - hi.astype(F32)).astype(BF16)
    ss = jnp.dot(hi, gmat, preferred_element_type=F32) + jnp.dot(lo, gmat, preferred_element_type=F32)
    return x * lax.rsqrt(ss * (1.0 / group) + EPS) * g


def _group_matrix(width, group):
    idx = jnp.arange(width) // group
    return (idx[:, None] == idx[None, :]).astype(BF16)


def _ffn_kernel(x_ref, g_ref, wg_ref, wu_ref, wo_ref, o_ref):
    x = x_ref[...]
    hn = _rms(x, g_ref[...]).astype(BF16)
    acc = jnp.zeros_like(x)
    for c in range(wg_ref.shape[0]):
        gate = jnp.dot(hn, wg_ref[c], preferred_element_type=F32)
        up = jnp.dot(hn, wu_ref[c], preferred_element_type=F32)
        act = (gate * jax.nn.sigmoid(gate) * up).astype(BF16)
        acc = acc + jnp.dot(act, wo_ref[c], preferred_element_type=F32)
    o_ref[...] = x + 0.5 * acc


def _prep_ffn(g, w_in, w_out):
    nc = D_FF // FF_CHUNK
    wg = w_in[:, :D_FF].reshape(D_MODEL, nc, FF_CHUNK).transpose(1, 0, 2).astype(BF16)
    wu = w_in[:, D_FF:].reshape(D_MODEL, nc, FF_CHUNK).transpose(1, 0, 2).astype(BF16)
    wo = w_out.reshape(nc, FF_CHUNK, D_MODEL).astype(BF16)
    return g.reshape(1, D_MODEL), wg, wu, wo


def _ffn(x, prep, tm):
    g, wg, wu, wo = prep
    n = x.shape[0]
    row = pl.BlockSpec((tm, D_MODEL), lambda i: (i, 0))
    return pl.pallas_call(
        _ffn_kernel,
        grid=(n // tm,),
        in_specs=[row, _const_spec(g.shape), _const_spec(wg.shape), _const_spec(wu.shape), _const_spec(wo.shape)],
        out_specs=row,
        out_shape=jax.ShapeDtypeStruct((n, D_MODEL), F32),
        compiler_params=_params(1),
        name="ffn",
    )(x, g, wg, wu, wo)


def _proj_kernel(x_ref, g_ref, wqkv_ref, wqi_ref, wkw_ref, wl_ref, wqm_ref, qg_ref, kg_ref, mg_ref,
                 kwscale_ref, g64_ref, g128_ref,
                 k32_ref, v32_ref, kw32_ref, qb_ref, kb_ref, vb_ref, qib_ref, kwb_ref, lx_ref, lg_ref, qmb_ref):
    hn = _rms(x_ref[...], g_ref[...]).astype(BF16)
    qkv = jnp.dot(hn, wqkv_ref[...], preferred_element_type=F32)
    q = _group_rms(qkv[:, :HD], g64_ref[...], qg_ref[...], DH_ATT)
    k = _group_rms(qkv[:, HD:2 * HD], g64_ref[...], kg_ref[...], DH_ATT)
    v = qkv[:, 2 * HD:]
    k32_ref[...] = k
    v32_ref[...] = v
    qb_ref[...] = (q * (DH_ATT ** -0.5)).astype(BF16)
    kb_ref[...] = k.astype(BF16)
    vb_ref[...] = v.astype(BF16)
    qib_ref[...] = jnp.dot(hn, wqi_ref[...], preferred_element_type=F32).astype(BF16)
    kw = jnp.dot(hn, wkw_ref[...], preferred_element_type=F32) * kwscale_ref[...]
    kw32_ref[...] = kw
    kwb_ref[...] = kw.astype(BF16)
    lxg = jnp.dot(hn, wl_ref[...], preferred_element_type=F32)
    lx_ref[...] = lxg[:, :LRU_W]
    lg_ref[...] = lxg[:, LRU_W:]
    qm = jnp.dot(hn, wqm_ref[...], preferred_element_type=F32)
    qmb_ref[...] = _group_rms(qm, g128_ref[...], mg_ref[...], DH_MEM).astype(BF16)


def _prep_proj(norm_g, w_in, q_norm_g, k_norm_g, mem_q_norm_g):
    o = 0
    cols = {}
    for name, size in (("q", HD), ("k", HD), ("v", HD), ("qi", H_IDX * D_IDX), ("ki", D_IDX), ("wi", H_IDX),
                       ("lx", LRU_W), ("lg", LRU_W), ("qm", MD), ("gates", 3 * D_MODEL)):
        cols[name] = w_in[:, o:o + size]
        o += size
    pad = jnp.zeros((D_MODEL, LANES - D_IDX - H_IDX), w_in.dtype)
    wqkv = jnp.concatenate([cols["q"], cols["k"], cols["v"]], axis=1).astype(BF16)
    wkw = jnp.concatenate([cols["ki"], cols["wi"], pad], axis=1).astype(BF16)
    wl = jnp.concatenate([cols["lx"], cols["lg"]], axis=1).astype(BF16)
    kwscale = jnp.concatenate([jnp.ones((D_IDX,), F32),
                               jnp.full((H_IDX,), H_IDX ** -0.5 * D_IDX ** -0.5, F32),
                               jnp.zeros((LANES - D_IDX - H_IDX,), F32)]).reshape(1, LANES)
    return dict(
        g=norm_g.reshape(1, D_MODEL), wqkv=wqkv, wqi=cols["qi"].astype(BF16), wkw=wkw, wl=wl,
        wqm=cols["qm"].astype(BF16),
        qg=jnp.tile(q_norm_g, H_ATT).reshape(1, HD), kg=jnp.tile(k_norm_g, H_ATT).reshape(1, HD),
        mg=jnp.tile(mem_q_norm_g, H_MEM).reshape(1, MD), kwscale=kwscale,
        g64=_group_matrix(HD, DH_ATT), g128=_group_matrix(MD, DH_MEM),
        wgates=cols["gates"].astype(BF16),
        wqkT=wqkv[:, :2 * HD].T, wvT=wqkv[:, 2 * HD:].T, wqiT=cols["qi"].astype(BF16).T, wkwT=wkw.T,
        qg_col=jnp.tile(q_norm_g, H_ATT).reshape(HD, 1), kg_col=jnp.tile(k_norm_g, H_ATT).reshape(HD, 1),
        kwscale_col=kwscale.reshape(LANES, 1),
    )


def _group_rms_t(xt, gmat, g, group):
    x2 = xt * xt
    hi = x2.astype(BF16)
    lo = (x2 - hi.astype(F32)).astype(BF16)
    ss = jnp.dot(gmat, hi, preferred_element_type=F32) + jnp.dot(gmat, lo, preferred_element_type=F32)
    return xt * lax.rsqrt(ss * (1.0 / group) + EPS) * g


def _proj_t_kernel(x_ref, g_ref, wqkT_ref, wvT_ref, wqiT_ref, wkwT_ref, wl_ref, wqm_ref, qgT_ref, kgT_ref, mg_ref,
                   kwscaleT_ref, g64_ref, g128_ref,
                   kT_ref, vT_ref, kiT_ref, wT_ref, qT_ref, qiT_ref, kb_ref, kwb_ref, vTb_ref, lx_ref, lg_ref,
                   qmb_ref):
    hn = _rms(x_ref[0], g_ref[...]).astype(BF16)
    nt = lambda w_ref: lax.dot_general(w_ref[...], hn, NT_DIMS, preferred_element_type=F32)
    qkT = nt(wqkT_ref)
    qT = _group_rms_t(qkT[:HD], g64_ref[...], qgT_ref[...], DH_ATT)
    kT = _group_rms_t(qkT[HD:], g64_ref[...], kgT_ref[...], DH_ATT)
    kT_ref[0] = kT
    kb_ref[0] = kT.T.astype(BF16)
    qT_ref[0] = (qT * (DH_ATT ** -0.5 * LOG2E)).astype(BF16)
    vT = nt(wvT_ref)
    vT_ref[0] = vT
    for c in range(vTb_ref.shape[1]):
        vTb_ref[0, c] = vT[:, c * KB:(c + 1) * KB].astype(BF16)
    qiT_ref[0] = nt(wqiT_ref).astype(BF16)
    kwT = nt(wkwT_ref) * kwscaleT_ref[...]
    kiT_ref[0] = kwT[:D_IDX]
    wT_ref[0] = kwT[D_IDX:D_IDX + H_IDX]
    kwb_ref[0] = kwT.T.astype(BF16)
    lxg = jnp.dot(hn, wl_ref[...], preferred_element_type=F32)
    lx_ref[0] = lxg[:, :LRU_W]
    lg_ref[0] = lxg[:, LRU_W:]
    qm = jnp.dot(hn, wqm_ref[...], preferred_element_type=F32)
    qmb_ref[0] = _group_rms(qm, g128_ref[...], mg_ref[...], DH_MEM).astype(BF16)


def _proj_t(x, p, tm):
    b, s, _ = x.shape
    bc = lambda col: jnp.broadcast_to(col, (col.shape[0], tm))
    consts = [p["g"], p["wqkT"], p["wvT"], p["wqiT"], p["wkwT"], p["wl"], p["wqm"], bc(p["qg_col"]),
              bc(p["kg_col"]), p["mg"], bc(p["kwscale_col"]), p["g64"], p["g128"]]
    tok = lambda w: pl.BlockSpec((1, tm, w), lambda i, j: (i, j, 0))
    feat = lambda w: pl.BlockSpec((1, w, tm), lambda i, j: (i, 0, j))
    outs = [
        (feat(HD), (b, HD, s), F32), (feat(HD), (b, HD, s), F32), (feat(D_IDX), (b, D_IDX, s), F32),
        (feat(H_IDX), (b, H_IDX, s), F32), (feat(HD), (b, HD, s), BF16), (feat(HD), (b, HD, s), BF16),
        (tok(HD), (b, s, HD), BF16), (tok(LANES), (b, s, LANES), BF16),
        (pl.BlockSpec((1, tm // KB, HD, KB), lambda i, j: (i, j, 0, 0)), (b, s // KB, HD, KB), BF16),
        (tok(LRU_W), (b, s, LRU_W), F32), (tok(LRU_W), (b, s, LRU_W), F32), (tok(MD), (b, s, MD), BF16),
    ]
    return pl.pallas_call(
        _proj_t_kernel,
        grid=(b, s // tm),
        in_specs=[tok(D_MODEL)] + [_const_spec(c.shape) for c in consts],
        out_specs=[o[0] for o in outs],
        out_shape=[jax.ShapeDtypeStruct(o[1], o[2]) for o in outs],
        compiler_params=_params(2),
        name="proj_t",
    )(x, *consts)


def _proj(x, p, tm):
    n = x.shape[0]
    consts = [p[k] for k in ("g", "wqkv", "wqi", "wkw", "wl", "wqm", "qg", "kg", "mg", "kwscale", "g64", "g128")]

    def row(w):
        return pl.BlockSpec((tm, w), lambda i: (i, 0))

    outs = [(HD, F32), (HD, F32), (LANES, F32), (HD, BF16), (HD, BF16), (HD, BF16), (HD, BF16), (LANES, BF16),
            (LRU_W, F32), (LRU_W, F32), (MD, BF16)]
    return pl.pallas_call(
        _proj_kernel,
        grid=(n // tm,),
        in_specs=[row(D_MODEL)] + [_const_spec(c.shape) for c in consts],
        out_specs=[row(w) for w, _ in outs],
        out_shape=[jax.ShapeDtypeStruct((n, w), dt) for w, dt in outs],
        compiler_params=_params(1),
        name="proj",
    )(x, *consts)


def _memkv_kernel(m_ref, g_ref, w_ref, kg_ref, g128_ref, mk_ref, mv_ref):
    hn = _rms(m_ref[...], g_ref[...]).astype(BF16)
    kv = jnp.dot(hn, w_ref[...], preferred_element_type=F32)
    mk_ref[...] = _group_rms(kv[:, :MD], g128_ref[...], kg_ref[...], DH_MEM)
    mv_ref[...] = kv[:, MD:]


def _memkv(mem, norm_g, w_mem_kv, mem_k_norm_g, g128):
    n = mem.shape[0]
    tm = min(n, 512)
    consts = [norm_g.reshape(1, D_MODEL), w_mem_kv.astype(BF16), jnp.tile(mem_k_norm_g, H_MEM).reshape(1, MD), g128]
    row = lambda w: pl.BlockSpec((tm, w), lambda i: (i, 0))
    return pl.pallas_call(
        _memkv_kernel,
        grid=(n // tm,),
        in_specs=[row(D_MODEL)] + [_const_spec(c.shape) for c in consts],
        out_specs=[row(MD), row(MD)],
        out_shape=[jax.ShapeDtypeStruct((n, MD), F32)] * 2,
        compiler_params=_params(1),
        name="memkv",
    )(mem, *consts)


def _memattn_kernel(q_ref, mk_ref, mv_ref, o_ref):
    q = q_ref[0]
    mk = mk_ref[0].astype(BF16)
    mv = mv_ref[0].astype(BF16)
    for h in range(H_MEM):
        sl = slice(h * DH_MEM, (h + 1) * DH_MEM)
        s = lax.dot_general(q[:, sl], mk[:, sl], NT_DIMS, preferred_element_type=F32) * (DH_MEM ** -0.5)
        m = jnp.max(s, axis=-1, keepdims=True)
        e = jnp.exp(s - m)
        p = (e / jnp.sum(e, axis=-1, keepdims=True)).astype(BF16)
        o_ref[0, :, sl] = jnp.dot(p, mv[:, sl], preferred_element_type=F32).astype(BF16)


def _memattn(qm, mk, mv, tm):
    b, t, _ = qm.shape
    n_mem = mk.shape[1]
    return pl.pallas_call(
        _memattn_kernel,
        grid=(b, t // tm),
        in_specs=[pl.BlockSpec((1, tm, MD), lambda i, j: (i, j, 0)),
                  pl.BlockSpec((1, n_mem, MD), lambda i, j: (i, 0, 0)),
                  pl.BlockSpec((1, n_mem, MD), lambda i, j: (i, 0, 0))],
        out_specs=pl.BlockSpec((1, tm, MD), lambda i, j: (i, j, 0)),
        out_shape=jax.ShapeDtypeStruct((b, t, MD), BF16),
        compiler_params=_params(2),
        name="memattn",
    )(qm, mk, mv)


def _shift_rows(x, k, fill):
    rows = lax.broadcasted_iota(I32, x.shape, 0)
    return jnp.where(rows >= k, pltpu.roll(x, k, 0), fill)


def _rglru_kernel(lx_ref, lg_ref, cs_ref, h0_ref, cw_ref, cb_ref, wa_ref, ba_ref, wi_ref, bi_ref, lam_ref,
                  y_ref, nb_ref, hl_ref, tail_ref, h_ref):
    t = pl.program_id(1)

    @pl.when(t == 0)
    def _():
        tail_ref[...] = cs_ref[0]
        h_ref[...] = h0_ref[0]

    x = lx_ref[0]
    tt = x.shape[0]
    tail = tail_ref[...]
    rows8 = lax.broadcasted_iota(I32, tail.shape, 0)
    conv = cb_ref[...] + x * cw_ref[CONV_W - 1:CONV_W, :]
    for d in range(1, CONV_W):
        xs = pltpu.roll(x, d, 0)
        head = jnp.where(rows8 < d, pltpu.roll(tail, d, 0), xs[:SUBLANES])
        xs = head if tt == SUBLANES else jnp.concatenate([head, xs[SUBLANES:]], axis=0)
        conv = conv + xs * cw_ref[CONV_W - 1 - d:CONV_W - d, :]
    tail_ref[...] = x[tt - SUBLANES:]
    nb_ref[0] = x[tt - SUBLANES:]

    cb16 = conv.astype(BF16)
    r = jax.nn.sigmoid(jnp.dot(cb16, wa_ref[...], preferred_element_type=F32) + ba_ref[...])
    ig = jax.nn.sigmoid(jnp.dot(cb16, wi_ref[...], preferred_element_type=F32) + bi_ref[...])
    nl = -lam_ref[...]
    softplus = jnp.maximum(nl, 0.0) + jnp.log1p(jnp.exp(-jnp.abs(nl)))
    log_a = -LRU_C * r * softplus
    a = jnp.exp(log_a)
    b = jnp.sqrt(-jnp.tanh(log_a) * (jnp.exp(2.0 * log_a) + 1.0)) * (ig * conv)
    rows = lax.broadcasted_iota(I32, x.shape, 0)
    b = b + jnp.where(rows == 0, a * h_ref[SUBLANES - 1:SUBLANES, :], 0.0)
    k = 1
    while k < tt:
        b = a * _shift_rows(b, k, 0.0) + b
        a = a * _shift_rows(a, k, 1.0)
        k *= 2
    h = b
    h_ref[...] = h[tt - SUBLANES:]
    hl_ref[0] = h[tt - SUBLANES:]
    y_ref[0] = (h * jax.nn.gelu(lg_ref[0])).astype(BF16)


def _block_diag(w):
    nb, bs, _ = w.shape
    eye = jnp.eye(nb, dtype=w.dtype)
    return (eye[:, None, :, None] * w[:, :, None, :]).reshape(nb * bs, nb * bs)


def _prep_rglru(conv_w, conv_b, lru_wa, lru_ba, lru_wi, lru_bi, lru_lambda):
    r = lambda v: v.reshape(1, LRU_W)
    return [conv_w, r(conv_b), _block_diag(lru_wa).astype(BF16), r(lru_ba), _block_diag(lru_wi).astype(BF16),
            r(lru_bi), r(lru_lambda)]


def _rglru(lx, lg, conv_state, h0, consts, tt):
    b, t, w = lx.shape
    cs = jnp.concatenate([jnp.zeros((b, SUBLANES - (CONV_W - 1), w), F32), conv_state], axis=1)
    h0p = jnp.concatenate([jnp.zeros((b, SUBLANES - 1, w), F32), h0[:, None, :]], axis=1)
    seq = pl.BlockSpec((1, tt, w), lambda i, j: (i, j, 0))
    st = pl.BlockSpec((1, SUBLANES, w), lambda i, j: (i, 0, 0))
    y, nb, hl = pl.pallas_call(
        _rglru_kernel,
        grid=(b, t // tt),
        in_specs=[seq, seq, st, st] + [_const_spec(c.shape) for c in consts],
        out_specs=[seq, st, st],
        out_shape=[jax.ShapeDtypeStruct((b, t, w), BF16), jax.ShapeDtypeStruct((b, SUBLANES, w), F32),
                   jax.ShapeDtypeStruct((b, SUBLANES, w), F32)],
        scratch_shapes=[pltpu.VMEM((SUBLANES, w), F32), pltpu.VMEM((SUBLANES, w), F32)],
        compiler_params=pltpu.CompilerParams(dimension_semantics=("parallel", "arbitrary"),
                                             vmem_limit_bytes=VMEM_LIMIT),
        name="rglru",
    )(lx, lg, cs, h0p, *consts)
    return y, nb[:, SUBLANES - (CONV_W - 1):], hl[:, SUBLANES - 1]


def _score_key(score):
    bits = lax.bitcast_convert_type(score, I32)
    return bits ^ ((bits >> 31) & 0x7FFFFFFF)


def _radix_threshold(count_ge, rows, topk):
    def bit_step(i, thr_u):
        cand_u = thr_u | (jnp.int32(1) << (31 - i))
        cnt = count_ge(cand_u ^ INT_MIN)
        return jnp.where(cnt >= topk, cand_u, thr_u)

    thr_u = lax.fori_loop(0, 32, bit_step, jnp.zeros((rows, 1), I32))
    return jnp.maximum(thr_u ^ INT_MIN, KEY_NEG_INF + 1)


def _dsa_prompt_kernel(qT_ref, qiT_ref, wT_ref, kw_ref, k_ref, vT_ref, tri_ref, o_ref,
                       sc_ref, qpad_ref, qipad_ref, oT_ref, *, topk):
    i = pl.program_id(1)
    nk = i + 1
    kf = float(topk)
    kblock = lambda j: pl.ds(pl.multiple_of(j * KB, KB), KB)

    zeros64 = jnp.zeros((D_IDX, TQ), BF16)
    for h in range(H_IDX):
        qipad_ref[h] = jnp.concatenate([qiT_ref[0, h * D_IDX:(h + 1) * D_IDX, :], zeros64], axis=0)
    for h in range(H_ATT):
        qh = qT_ref[0, h * DH_ATT:(h + 1) * DH_ATT, :]
        qpad_ref[h] = jnp.concatenate([qh, zeros64] if h % 2 == 0 else [zeros64, qh], axis=0)
    wT = wT_ref[0]

    def block_scores(j):
        kw = kw_ref[0, kblock(j), :]
        acc = jnp.zeros((KB, TQ), F32)
        for h in range(H_IDX):
            d = jnp.dot(kw, qipad_ref[h], preferred_element_type=F32)
            acc = acc + jnp.maximum(d, 0.0) * wT[h:h + 1, :]
        return acc

    def score_block(j, carry):
        lo, hi = carry
        acc = block_scores(j)
        sc_ref[kblock(j), :] = acc
        return (jnp.minimum(lo, jnp.min(acc, axis=0, keepdims=True)),
                jnp.maximum(hi, jnp.max(acc, axis=0, keepdims=True)))

    init = (jnp.full((1, TQ), jnp.inf, F32), jnp.full((1, TQ), NEG_INF, F32))
    rmin, rmax = lax.fori_loop(0, i, score_block, init)
    acc = block_scores(i)
    krow = lax.broadcasted_iota(I32, (KB, TQ), 0)
    qcol = lax.broadcasted_iota(I32, (KB, TQ), 1)
    sc_ref[kblock(i), :] = jnp.where(krow <= qcol, acc, NEG_INF)
    rmin = jnp.minimum(rmin, jnp.min(acc, axis=0, keepdims=True))
    rmax = jnp.maximum(rmax, jnp.max(acc, axis=0, keepdims=True))

    def count(pred):
        def body(j, cnt):
            m = jnp.where(pred(sc_ref[kblock(j), :]), 1.0, 0.0)
            for r in range(KB // CNT_ROWS):
                cnt = cnt + m[r * CNT_ROWS:(r + 1) * CNT_ROWS]
            return cnt

        cnt = lax.fori_loop(0, nk, body, jnp.zeros((CNT_ROWS, TQ), F32))
        return jnp.sum(cnt, axis=0, keepdims=True)

    def n_open(done):
        return jnp.sum(1.0 - done).astype(I32)

    done0 = jnp.where(count(lambda x: x >= rmin) <= kf, 1.0, 0.0)

    def bisect(c):
        it, lo, hi, done, _ = c
        mid = 0.5 * lo + 0.5 * jnp.minimum(hi, rmax)
        c_mid = count(lambda x: x >= mid)
        ge = c_mid >= kf
        live = done < 0.5
        lo = jnp.where(live & ge, mid, lo)
        hi = jnp.where(live & jnp.logical_not(ge), mid, hi)
        done = jnp.maximum(done, jnp.where(c_mid == kf, 1.0, 0.0))
        return it + 1, lo, hi, done, n_open(done)

    _, lo, hi, done, left = lax.while_loop(
        lambda c: (c[0] < BISECT_STEPS) & (c[4] > 0), bisect,
        (jnp.int32(0), rmin, jnp.full((1, TQ), jnp.inf, F32), done0, n_open(done0)))

    def step_down(c):
        lo, hi, done, _ = c

        def body(j, best):
            x = sc_ref[kblock(j), :]
            return jnp.maximum(best, jnp.max(jnp.where(x < hi, x, NEG_INF), axis=0, keepdims=True))

        cand = lax.fori_loop(0, nk, body, jnp.full((1, TQ), NEG_INF, F32))
        ok = count(lambda x: x >= cand) >= kf
        live = done < 0.5
        lo = jnp.where(live & ok, cand, lo)
        hi = jnp.where(live & jnp.logical_not(ok), cand, hi)
        done = jnp.maximum(done, jnp.where(ok, 1.0, 0.0))
        return lo, hi, done, n_open(done)

    thr, _, _, _ = lax.while_loop(lambda c: c[3] > 0, step_down, (lo, hi, done, left))

    @pl.when(left == 0)
    def _():
        def body(j, carry):
            x = sc_ref[kblock(j), :]
            sc_ref[kblock(j), :] = jnp.where(x >= thr, 0.0, NEG_INF)
            return carry

        lax.fori_loop(0, nk, body, 0)

    @pl.when(left > 0)
    def _():
        quota = kf - count(lambda x: x > thr)

        def body(j, ties_before):
            x = sc_ref[kblock(j), :]
            eq = jnp.where(x == thr, 1.0, 0.0)
            rank = jnp.dot(tri_ref[...], eq.astype(BF16), preferred_element_type=F32) + ties_before
            sel = (x > thr) | ((x == thr) & (rank < quota))
            sc_ref[kblock(j), :] = jnp.where(sel, 0.0, NEG_INF)
            return ties_before + jnp.sum(eq, axis=0, keepdims=True)

        lax.fori_loop(0, nk, body, jnp.zeros((1, TQ), F32))

    def attend(j, carry):
        ms, ls, accs = carry
        new_ms, new_ls, new_accs = [], [], []

        def qk(h):
            pair = slice((h // 2) * LANES, (h // 2 + 1) * LANES)
            return jnp.dot(k_ref[0, kblock(j), pair], qpad_ref[h], preferred_element_type=F32)

        s_next = qk(0)
        for h in range(H_ATT):
            hrows = slice(h * DH_ATT, (h + 1) * DH_ATT)
            s, s_next = s_next, (qk(h + 1) if h + 1 < H_ATT else None)
            s = s + sc_ref[kblock(j), :]
            m_new = jnp.maximum(ms[h], jnp.max(s, axis=0, keepdims=True))
            m_safe = jnp.where(m_new == NEG_INF, 0.0, m_new)
            alpha = jnp.exp2(ms[h] - m_safe)
            p = jnp.exp2(s - m_safe)
            new_ms.append(m_new)
            new_ls.append(alpha * ls[h] + jnp.sum(p, axis=0, keepdims=True))
            pv = jnp.dot(vT_ref[0, j, hrows, :], p.astype(BF16), preferred_element_type=F32)
            new_accs.append(alpha * accs[h] + pv)
        return tuple(new_ms), tuple(new_ls), tuple(new_accs)

    init = (tuple(jnp.full((1, TQ), NEG_INF, F32) for _ in range(H_ATT)),
            tuple(jnp.zeros((1, TQ), F32) for _ in range(H_ATT)),
            tuple(jnp.zeros((DH_ATT, TQ), F32) for _ in range(H_ATT)))
    _, ls, accs = lax.fori_loop(0, nk, attend, init)
    for h in range(H_ATT):
        oT_ref[h * DH_ATT:(h + 1) * DH_ATT, :] = accs[h] / ls[h]
    o_ref[0] = oT_ref[...].T.astype(BF16)


def _dsa_prompt(qT, qiT, wT, kwb, kb, vTb):
    b, _, s = qT.shape
    topk = min(TOPK_MAX, s // 4)
    tri = (jnp.arange(KB)[:, None] > jnp.arange(KB)[None, :]).astype(BF16)
    feat = lambda w: pl.BlockSpec((1, w, TQ), lambda i, j: (i, 0, j))
    full = lambda w: pl.BlockSpec((1, s, w), lambda i, j: (i, 0, 0))
    return pl.pallas_call(
        functools.partial(_dsa_prompt_kernel, topk=topk),
        grid=(b, s // TQ),
        in_specs=[feat(HD), feat(HD), feat(H_IDX), full(LANES), full(HD),
                  pl.BlockSpec((1, s // KB, HD, KB), lambda i, j: (i, 0, 0, 0)), _const_spec(tri.shape)],
        out_specs=pl.BlockSpec((1, TQ, HD), lambda i, j: (i, j, 0)),
        out_shape=jax.ShapeDtypeStruct((b, s, HD), BF16),
        scratch_shapes=[pltpu.VMEM((s, TQ), F32), pltpu.VMEM((H_ATT, LANES, TQ), BF16),
                        pltpu.VMEM((H_IDX, LANES, TQ), BF16), pltpu.VMEM((HD, TQ), F32)],
        compiler_params=_params(2),
        name="dsa_prompt",
    )(qT, qiT, wT, kwb, kb, vTb, tri)


PAGE_GROUP_IDX = 16
PAGE_GROUP_KV = 8


def _dsa_sample_select_kernel(pt_ref, qi_ref, kwq_ref, tri_ref, *rest, n_pages, topk, group):
    page_refs, (bias_ref, biasn_ref, qiall_ref, wb_ref) = rest[:group], rest[group:]
    pg = pl.program_id(1)
    t = qi_ref.shape[1]
    kwq = kwq_ref[0]

    @pl.when(pg == 0)
    def _():
        qi = qi_ref[0].astype(F32)
        qiall_ref[...] = jnp.concatenate(
            [qi[:, h * D_IDX:(h + 1) * D_IDX] for h in range(H_IDX)], axis=0).astype(BF16)
        for h in range(H_IDX):
            wb_ref[h] = jnp.broadcast_to(kwq[:, D_IDX + h:D_IDX + h + 1], (t, LANES))

    def scores(dots):
        acc = jnp.zeros((t, dots.shape[1]), F32)
        for h in range(H_IDX):
            acc = acc + jnp.maximum(dots[h * t:(h + 1) * t], 0.0) * wb_ref[h]
        return acc

    for g in range(group):
        dots = jnp.dot(qiall_ref[...], page_refs[g][0].astype(BF16), preferred_element_type=F32)
        bias_ref[0, pg * group + g] = _score_key(scores(dots))

    @pl.when(pg == pl.num_programs(1) - 1)
    def _():
        new_keys = jnp.concatenate([kwq[:, :D_IDX], jnp.zeros((LANES - t, D_IDX), F32)], axis=0).astype(BF16)
        row = lax.broadcasted_iota(I32, (t, LANES), 0)
        col = lax.broadcasted_iota(I32, (t, LANES), 1)
        dots_new = lax.dot_general(qiall_ref[...], new_keys, NT_DIMS, preferred_element_type=F32)
        key_new = _score_key(jnp.where(col <= row, scores(dots_new), NEG_INF))

        def count(cmp_fn):
            def body(p, cnt):
                return cnt + cmp_fn(bias_ref[0, p]).astype(I32)

            cnt = lax.fori_loop(0, n_pages, body, cmp_fn(key_new).astype(I32))
            return jnp.sum(cnt, axis=1, keepdims=True)

        thr = _radix_threshold(lambda c: count(lambda key: key >= c), t, topk)
        quota = (topk - count(lambda key: key > thr)).astype(F32)

        def select(key, ties_before):
            eq = key == thr
            eqf = eq.astype(F32)
            rank = jnp.dot(eqf.astype(BF16), tri_ref[...], preferred_element_type=F32) + ties_before
            sel = (key > thr) | (eq & (rank < quota))
            bias = lax.bitcast_convert_type(jnp.where(sel, 0.0, NEG_INF), I32)
            return bias, ties_before + jnp.sum(eqf, axis=1, keepdims=True)

        def bias_page(p, ties_before):
            bias, ties = select(bias_ref[0, p], ties_before)
            bias_ref[0, p] = bias
            return ties

        ties = lax.fori_loop(0, n_pages, bias_page, jnp.zeros((t, 1), F32))
        biasn_ref[0], _ = select(key_new, ties)


def _dsa_sample_select(page_table, qib, kw32, cache_kidx_t):
    b, t, _ = qib.shape
    n_pages = page_table.shape[1]
    page = cache_kidx_t.shape[2]
    topk = min(TOPK_MAX, (n_pages * page + t) // 4)
    group = PAGE_GROUP_IDX
    tri = (jnp.arange(page)[:, None] < jnp.arange(page)[None, :]).astype(BF16)
    tok = lambda w: pl.BlockSpec((1, t, w), lambda i, j, pt: (i, 0, 0))
    page_specs = [pl.BlockSpec((1, D_IDX, page), lambda i, j, pt, g=g: (pt[i, j * group + g], 0, 0))
                  for g in range(group)]
    grid_spec = pltpu.PrefetchScalarGridSpec(
        num_scalar_prefetch=1,
        grid=(b, n_pages // group),
        in_specs=[tok(H_IDX * D_IDX), tok(LANES), pl.BlockSpec(tri.shape, lambda i, j, pt: (0, 0))] + page_specs,
        out_specs=[pl.BlockSpec((1, n_pages, t, page), lambda i, j, pt: (i, 0, 0, 0)),
                   pl.BlockSpec((1, t, LANES), lambda i, j, pt: (i, 0, 0))],
        scratch_shapes=[pltpu.VMEM((H_IDX * t, D_IDX), BF16), pltpu.VMEM((H_IDX, t, LANES), F32)],
    )
    return pl.pallas_call(
        functools.partial(_dsa_sample_select_kernel, n_pages=n_pages, topk=topk, group=group),
        grid_spec=grid_spec,
        out_shape=[jax.ShapeDtypeStruct((b, n_pages, t, page), I32), jax.ShapeDtypeStruct((b, t, LANES), I32)],
        compiler_params=pltpu.CompilerParams(dimension_semantics=("parallel", "arbitrary"),
                                             vmem_limit_bytes=VMEM_LIMIT),
        name="dsa_sample_select",
    )(page_table, qib, kw32, tri, *([cache_kidx_t] * group))


def _dsa_sample_attend_kernel(pt_ref, q_ref, kn_ref, vn_ref, bias_ref, biasn_ref, *rest, group):
    kT_refs, vT_refs = rest[:group], rest[group:2 * group]
    o_ref, m_ref, l_ref, acc_ref = rest[2 * group:]
    pg = pl.program_id(1)
    t = q_ref.shape[1]
    page = kT_refs[0].shape[2]
    lane_head = lax.broadcasted_iota(I32, (t, HD), 1) // DH_ATT
    q = q_ref[0].astype(F32)
    qbd = jnp.concatenate([jnp.where(lane_head == h, q, 0.0) for h in range(H_ATT)], axis=0).astype(BF16)

    @pl.when(pg == 0)
    def _():
        m_ref[...] = jnp.full(m_ref.shape, NEG_INF, F32)
        l_ref[...] = jnp.zeros(l_ref.shape, F32)
        acc_ref[...] = jnp.zeros(acc_ref.shape, F32)

    def update(s, bias, pv):
        s = s + jnp.concatenate([lax.bitcast_convert_type(bias, F32)] * H_ATT, axis=0)
        m = m_ref[...]
        m_new = jnp.maximum(m, jnp.max(s, axis=1, keepdims=True))
        m_safe = jnp.where(m_new == NEG_INF, 0.0, m_new)
        alpha = jnp.exp(m - m_safe)
        p = jnp.exp(s - m_safe)
        l_ref[...] = alpha * l_ref[...] + jnp.sum(p, axis=1, keepdims=True)
        acc_ref[...] = alpha * acc_ref[...] + pv(p.astype(BF16))
        m_ref[...] = m_new

    s_pages = jnp.concatenate(
        [jnp.dot(qbd, kT_refs[g][0].astype(BF16), preferred_element_type=F32) for g in range(group)], axis=1)
    bias_pages = jnp.concatenate([bias_ref[0, g] for g in range(group)], axis=1)

    def pv_pages(p):
        out = jnp.zeros((H_ATT * t, HD), F32)
        for g in range(group):
            out = out + lax.dot_general(p[:, g * page:(g + 1) * page], vT_refs[g][0].astype(BF16), NT_DIMS,
                                        preferred_element_type=F32)
        return out

    update(s_pages, bias_pages, pv_pages)

    @pl.when(pg == pl.num_programs(1) - 1)
    def _():
        pad = jnp.zeros((LANES - t, HD), F32)
        kn = jnp.concatenate([kn_ref[0], pad], axis=0).astype(BF16)
        vn = jnp.concatenate([vn_ref[0], pad], axis=0).astype(BF16)
        update(lax.dot_general(qbd, kn, NT_DIMS, preferred_element_type=F32), biasn_ref[0],
               lambda p: jnp.dot(p, vn, preferred_element_type=F32))
        o = acc_ref[...] / l_ref[...]
        out = jnp.zeros((t, HD), F32)
        for h in range(H_ATT):
            out = out + jnp.where(lane_head == h, o[h * t:(h + 1) * t], 0.0)
        o_ref[0] = out.astype(BF16)


def _dsa_sample_attend(page_table, qb, k32, v32, bias, bias_new, cache_kt, cache_vt):
    b, t, _ = qb.shape
    n_pages = page_table.shape[1]
    page = cache_kt.shape[2]
    group = PAGE_GROUP_KV
    tok = lambda w: pl.BlockSpec((1, t, w), lambda i, j, pt: (i, 0, 0))
    kv_specs = [pl.BlockSpec((1, HD, page), lambda i, j, pt, g=g: (pt[i, j * group + g], 0, 0))
                for g in range(group)]
    grid_spec = pltpu.PrefetchScalarGridSpec(
        num_scalar_prefetch=1,
        grid=(b, n_pages // group),
        in_specs=[tok(HD), tok(HD), tok(HD),
                  pl.BlockSpec((1, group, t, page), lambda i, j, pt: (i, j, 0, 0)),
                  tok(LANES)] + kv_specs + kv_specs,
        out_specs=tok(HD),
        scratch_shapes=[pltpu.VMEM((H_ATT * t, 1), F32), pltpu.VMEM((H_ATT * t, 1), F32),
                        pltpu.VMEM((H_ATT * t, HD), F32)],
    )
    return pl.pallas_call(
        functools.partial(_dsa_sample_attend_kernel, group=group),
        grid_spec=grid_spec,
        out_shape=jax.ShapeDtypeStruct((b, t, HD), BF16),
        compiler_params=pltpu.CompilerParams(dimension_semantics=("parallel", "arbitrary"),
                                             vmem_limit_bytes=VMEM_LIMIT),
        name="dsa_sample_attend",
    )(page_table, qb, k32, v32, bias, bias_new, *([cache_kt] * group), *([cache_vt] * group))


def _merge_kernel(x_ref, oa_ref, ol_ref, om_ref, g_ref, wg_ref, wa_ref, wl_ref, wm_ref, wo_ref, o_ref):
    x = x_ref[...]
    hn = _rms(x, g_ref[...]).astype(BF16)
    m = jnp.zeros_like(x)
    for idx, (o_r, w_r) in enumerate(((oa_ref, wa_ref), (ol_ref, wl_ref), (om_ref, wm_ref))):
        gate = jax.nn.sigmoid(jnp.dot(hn, wg_ref[:, idx * D_MODEL:(idx + 1) * D_MODEL], preferred_element_type=F32))
        m = m + gate * jnp.dot(o_r[...], w_r[...], preferred_element_type=F32)
    o_ref[...] = x + jnp.dot(m.astype(BF16), wo_ref[...], preferred_element_type=F32)


def _merge(x, o_att, o_lru, o_mem, consts, tm):
    n = x.shape[0]
    row = lambda w: pl.BlockSpec((tm, w), lambda i: (i, 0))
    return pl.pallas_call(
        _merge_kernel,
        grid=(n // tm,),
        in_specs=[row(D_MODEL), row(HD), row(LRU_W), row(MD)] + [_const_spec(c.shape) for c in consts],
        out_specs=row(D_MODEL),
        out_shape=jax.ShapeDtypeStruct((n, D_MODEL), F32),
        compiler_params=_params(1),
        name="merge",
    )(x, o_att, o_lru, o_mem, *consts)


def _token_tile(n):
    return min(n, 512)


def _layer(x, is_prompt, lw, pp, extra):
    b, t, _ = x.shape
    n = b * t
    tm = _token_tile(n)
    x1 = _ffn(x.reshape(n, D_MODEL), pp["ffn1"], tm)
    r3 = lambda a: a.reshape(b, t, a.shape[-1])
    if is_prompt:
        kT32, vT32, kiT32, wT, qT, qiT, kb, kwb, vTb, lx, lg, qmb = _proj_t(r3(x1), pp["proj"], tm)
        o_att = _dsa_prompt(qT, qiT, wT, kwb, kb, vTb)
        k_new = kT32.reshape(b, H_ATT, DH_ATT, t).transpose(0, 3, 1, 2)
        v_new = vT32.reshape(b, H_ATT, DH_ATT, t).transpose(0, 3, 1, 2)
        ki_new = kiT32.transpose(0, 2, 1)
        conv_state = jnp.zeros((b, CONV_W - 1, LRU_W), F32)
        h0 = jnp.zeros((b, LRU_W), F32)
        mem = extra["mem"]
        mk, mv = _memkv(mem.reshape(-1, D_MODEL), lw["norm_mem_g"], lw["w_mem_kv"], lw["mem_k_norm_g"],
                        pp["proj"]["g128"])
        mk = mk.reshape(b, -1, MD)
        mv = mv.reshape(b, -1, MD)
    else:
        k32, v32, kw32, qb, _, _, qib, _, lx, lg, qmb = _proj(x1, pp["proj"], tm)
        pt = extra["page_table"]
        bias, bias_new = _dsa_sample_select(pt, r3(qib), r3(kw32), extra["cache_kidx_t"])
        o_att = _dsa_sample_attend(pt, r3(qb), r3(k32), r3(v32), bias, bias_new, extra["cache_kt"],
                                   extra["cache_vt"])
        k_new = k32.reshape(b, t, H_ATT, DH_ATT)
        v_new = v32.reshape(b, t, H_ATT, DH_ATT)
        ki_new = r3(kw32)[:, :, :D_IDX]
        conv_state, h0 = extra["state_conv"], extra["state_h"]
        mk, mv = extra["cache_mem_k"], extra["cache_mem_v"]
    o_lru, conv_buf, h_last = _rglru(r3(lx), r3(lg), conv_state, h0, pp["rglru"], min(t, 256))
    o_mem = _memattn(r3(qmb), mk, mv, min(t, 512))
    x2 = _merge(x1, o_att.reshape(n, HD), o_lru.reshape(n, LRU_W), o_mem.reshape(n, MD), pp["merge"], tm)
    y = _ffn(x2, pp["ffn2"], tm).reshape(b, t, D_MODEL)
    if is_prompt:
        state = (k_new, v_new, ki_new, mk.reshape(b, -1, H_MEM, DH_MEM), mv.reshape(b, -1, H_MEM, DH_MEM),
                 conv_buf, h_last)
    else:
        state = (k_new, v_new, ki_new, conv_buf, h_last)
    return y, state


def kernel(x_prompt, x_sample, cache_k, cache_v, cache_kidx, page_table, cache_mem_k, cache_mem_v, state_conv, state_h, mem_prompt, norm_ffn1_g, w_ffn1_in, w_ffn1_out, norm_mix_g, w_in, q_norm_g, k_norm_g, w_attn_o, conv_w, conv_b, lru_wa, lru_ba, lru_wi, lru_bi, lru_lambda, w_lru_o, norm_mem_g, w_mem_kv, mem_q_norm_g, mem_k_norm_g, w_mem_o, w_out, norm_ffn2_g, w_ffn2_in, w_ffn2_out):
    depth = w_in.shape[0]
    n_phys, page = cache_k.shape[1], cache_k.shape[2]
    xp, xs = x_prompt, x_sample
    p_states, s_states = [], []
    for l in range(depth):
        proj = _prep_proj(norm_mix_g[l], w_in[l], q_norm_g[l], k_norm_g[l], mem_q_norm_g[l])
        pp = dict(
            ffn1=_prep_ffn(norm_ffn1_g[l], w_ffn1_in[l], w_ffn1_out[l]),
            ffn2=_prep_ffn(norm_ffn2_g[l], w_ffn2_in[l], w_ffn2_out[l]),
            proj=proj,
            rglru=_prep_rglru(conv_w[l], conv_b[l], lru_wa[l], lru_ba[l], lru_wi[l], lru_bi[l], lru_lambda[l]),
            merge=[proj["g"], proj["wgates"], w_attn_o[l].astype(BF16), w_lru_o[l].astype(BF16),
                   w_mem_o[l].astype(BF16), w_out[l].astype(BF16)],
        )
        lw = dict(norm_mem_g=norm_mem_g[l], w_mem_kv=w_mem_kv[l], mem_k_norm_g=mem_k_norm_g[l])
        xp, st_p = _layer(xp, True, lw, pp, dict(mem=mem_prompt))
        xs, st_s = _layer(xs, False, lw, pp, dict(
            page_table=page_table,
            cache_kt=cache_k[l].transpose(0, 2, 3, 1).reshape(n_phys, HD, page),
            cache_vt=cache_v[l].transpose(0, 2, 3, 1).reshape(n_phys, HD, page),
            cache_kidx_t=cache_kidx[l].transpose(0, 2, 1), cache_mem_k=cache_mem_k[l].reshape(-1, cache_mem_k.shape[2], MD),
            cache_mem_v=cache_mem_v[l].reshape(-1, cache_mem_v.shape[2], MD),
            state_conv=state_conv[l], state_h=state_h[l]))
        p_states.append(st_p)
        s_states.append(st_s)
    stack = lambda states, i: jnp.stack([s[i] for s in states])
    return (xp, xs) + tuple(stack(p_states, i) for i in range(7)) + tuple(stack(s_states, i) for i in range(5))
```

```python
import functools

import jax
import jax.numpy as jnp
from jax import lax
from jax.experimental import pallas as pl
from jax.experimental.pallas import tpu as pltpu

F32, BF16, I32 = jnp.float32, jnp.bfloat16, jnp.int32

D_MODEL = 1024
H_ATT, DH_ATT = 8, 64
H_IDX, D_IDX = 8, 64
TOPK_MAX = 256
LRU_W, LRU_BLOCKS, CONV_W, LRU_C = 512, 8, 4, 8.0
H_MEM, DH_MEM = 4, 128
D_FF = 2816
EPS = 1e-6
HD = H_ATT * DH_ATT
MD = H_MEM * DH_MEM

LANES = 128
SUBLANES = 8
VMEM_BYTES_V7X = 64 * 1024 * 1024
VMEM_LIMIT = VMEM_BYTES_V7X - 8 * 1024 * 1024

FF_CHUNK = 256
TQ = 256
KB = 256
CNT_ROWS = 32
BISECT_STEPS = 15
LOG2E = 1.4426950408889634
INT_MIN = -2 ** 31
KEY_NEG_INF = -2139095041
NEG_INF = float("-inf")

NT_DIMS = (((1,), (1,)), ((), ()))


def _params(n_grid, parallel=True):
    sem = ("parallel" if parallel else "arbitrary",) * n_grid
    return pltpu.CompilerParams(dimension_semantics=sem, vmem_limit_bytes=VMEM_LIMIT)


def _const_spec(shape):
    nd = len(shape)
    return pl.BlockSpec(shape, lambda *_: (0,) * nd)


def _rms(x, g):
    ms = jnp.mean(x * x, axis=-1, keepdims=True)
    return x * lax.rsqrt(ms + EPS) * g


def _group_rms(x, gmat, g, group):
    x2 = x * x
    hi = x2.astype(BF16)
    lo = (x2 - hi.astype(F32)).astype(BF16)
    ss = jnp.dot(hi, gmat, preferred_element_type=F32) + jnp.dot(lo, gmat, preferred_element_type=F32)
    return x * lax.rsqrt(ss * (1.0 / group) + EPS) * g


def _group_matrix(width, group):
    idx = jnp.arange(width) // group
    return (idx[:, None] == idx[None, :]).astype(BF16)


def _ffn_kernel(x_ref, g_ref, wg_ref, wu_ref, wo_ref, o_ref):
    x = x_ref[...]
    hn = _rms(x, g_ref[...]).astype(BF16)
    acc = jnp.zeros_like(x)
    for c in range(wg_ref.shape[0]):
        gate = jnp.dot(hn, wg_ref[c], preferred_element_type=F32)
        up = jnp.dot(hn, wu_ref[c], preferred_element_type=F32)
        act = (gate * jax.nn.sigmoid(gate) * up).astype(BF16)
        acc = acc + jnp.dot(act, wo_ref[c], preferred_element_type=F32)
    o_ref[...] = x + 0.5 * acc


def _prep_ffn(g, w_in, w_out):
    nc = D_FF // FF_CHUNK
    wg = w_in[:, :D_FF].reshape(D_MODEL, nc, FF_CHUNK).transpose(1, 0, 2).astype(BF16)
    wu = w_in[:, D_FF:].reshape(D_MODEL, nc, FF_CHUNK).transpose(1, 0, 2).astype(BF16)
    wo = w_out.reshape(nc, FF_CHUNK, D_MODEL).astype(BF16)
    return g.reshape(1, D_MODEL), wg, wu, wo


def _ffn(x, prep, tm):
    g, wg, wu, wo = prep
    n = x.shape[0]
    row = pl.BlockSpec((tm, D_MODEL), lambda i: (i, 0))
    return pl.pallas_call(
        _ffn_kernel,
        grid=(n // tm,),
        in_specs=[row, _const_spec(g.shape), _const_spec(wg.shape), _const_spec(wu.shape), _const_spec(wo.shape)],
        out_specs=row,
        out_shape=jax.ShapeDtypeStruct((n, D_MODEL), F32),
        compiler_params=_params(1),
        name="ffn",
    )(x, g, wg, wu, wo)


def _proj_kernel(x_ref, g_ref, wqkv_ref, wqi_ref, wkw_ref, wl_ref, wqm_ref, qg_ref, kg_ref, mg_ref,
                 kwscale_ref, g64_ref, g128_ref,
                 k32_ref, v32_ref, kw32_ref, qb_ref, kb_ref, vb_ref, qib_ref, kwb_ref, lx_ref, lg_ref, qmb_ref):
    hn = _rms(x_ref[...], g_ref[...]).astype(BF16)
    qkv = jnp.dot(hn, wqkv_ref[...], preferred_element_type=F32)
    q = _group_rms(qkv[:, :HD], g64_ref[...], qg_ref[...], DH_ATT)
    k = _group_rms(qkv[:, HD:2 * HD], g64_ref[...], kg_ref[...], DH_ATT)
    v = qkv[:, 2 * HD:]
    k32_ref[...] = k
    v32_ref[...] = v
    qb_ref[...] = (q * (DH_ATT ** -0.5)).astype(BF16)
    kb_ref[...] = k.astype(BF16)
    vb_ref[...] = v.astype(BF16)
    qib_ref[...] = jnp.dot(hn, wqi_ref[...], preferred_element_type=F32).astype(BF16)
    kw = jnp.dot(hn, wkw_ref[...], preferred_element_type=F32) * kwscale_ref[...]
    kw32_ref[...] = kw
    kwb_ref[...] = kw.astype(BF16)
    lxg = jnp.dot(hn, wl_ref[...], preferred_element_type=F32)
    lx_ref[...] = lxg[:, :LRU_W]
    lg_ref[...] = lxg[:, LRU_W:]
    qm = jnp.dot(hn, wqm_ref[...], preferred_element_type=F32)
    qmb_ref[...] = _group_rms(qm, g128_ref[...], mg_ref[...], DH_MEM).astype(BF16)


def _prep_proj(norm_g, w_in, q_norm_g, k_norm_g, mem_q_norm_g):
    o = 0
    cols = {}
    for name, size in (("q", HD), ("k", HD), ("v", HD), ("qi", H_IDX * D_IDX), ("ki", D_IDX), ("wi", H_IDX),
                       ("lx", LRU_W), ("lg", LRU_W), ("qm", MD), ("gates", 3 * D_MODEL)):
        cols[name] = w_in[:, o:o + size]
        o += size
    pad = jnp.zeros((D_MODEL, LANES - D_IDX - H_IDX), w_in.dtype)
    wqkv = jnp.concatenate([cols["q"], cols["k"], cols["v"]], axis=1).astype(BF16)
    wkw = jnp.concatenate([cols["ki"], cols["wi"], pad], axis=1).astype(BF16)
    wl = jnp.concatenate([cols["lx"], cols["lg"]], axis=1).astype(BF16)
    kwscale = jnp.concatenate([jnp.ones((D_IDX,), F32),
                               jnp.full((H_IDX,), H_IDX ** -0.5 * D_IDX ** -0.5, F32),
                               jnp.zeros((LANES - D_IDX - H_IDX,), F32)]).reshape(1, LANES)
    return dict(
        g=norm_g.reshape(1, D_MODEL), wqkv=wqkv, wqi=cols["qi"].astype(BF16), wkw=wkw, wl=wl,
        wqm=cols["qm"].astype(BF16),
        qg=jnp.tile(q_norm_g, H_ATT).reshape(1, HD), kg=jnp.tile(k_norm_g, H_ATT).reshape(1, HD),
        mg=jnp.tile(mem_q_norm_g, H_MEM).reshape(1, MD), kwscale=kwscale,
        g64=_group_matrix(HD, DH_ATT), g128=_group_matrix(MD, DH_MEM),
        wgates=cols["gates"].astype(BF16),
        wqkT=wqkv[:, :2 * HD].T, wvT=wqkv[:, 2 * HD:].T, wqiT=cols["qi"].astype(BF16).T, wkwT=wkw.T,
        qg_col=jnp.tile(q_norm_g, H_ATT).reshape(HD, 1), kg_col=jnp.tile(k_norm_g, H_ATT).reshape(HD, 1),
        kwscale_col=kwscale.reshape(LANES, 1),
    )


def _group_rms_t(xt, gmat, g, group):
    x2 = xt * xt
    hi = x2.astype(BF16)
    lo = (x2 - hi.astype(F32)).astype(BF16)
    ss = jnp.dot(gmat, hi, preferred_element_type=F32) + jnp.dot(gmat, lo, preferred_element_type=F32)
    return xt * lax.rsqrt(ss * (1.0 / group) + EPS) * g


def _proj_t_kernel(x_ref, g_ref, wqkT_ref, wvT_ref, wqiT_ref, wkwT_ref, wl_ref, wqm_ref, qgT_ref, kgT_ref, mg_ref,
                   kwscaleT_ref, g64_ref, g128_ref,
                   kT_ref, vT_ref, kiT_ref, wT_ref, qT_ref, qiT_ref, kb_ref, kwb_ref, vTb_ref, lx_ref, lg_ref,
                   qmb_ref):
    hn = _rms(x_ref[0], g_ref[...]).astype(BF16)
    nt = lambda w_ref: lax.dot_general(w_ref[...], hn, NT_DIMS, preferred_element_type=F32)
    qkT = nt(wqkT_ref)
    qT = _group_rms_t(qkT[:HD], g64_ref[...], qgT_ref[...], DH_ATT)
    kT = _group_rms_t(qkT[HD:], g64_ref[...], kgT_ref[...], DH_ATT)
    kT_ref[0] = kT
    kb_ref[0] = kT.T.astype(BF16)
    qT_ref[0] = (qT * (DH_ATT ** -0.5 * LOG2E)).astype(BF16)
    vT = nt(wvT_ref)
    vT_ref[0] = vT
    for c in range(vTb_ref.shape[1]):
        vTb_ref[0, c] = vT[:, c * KB:(c + 1) * KB].astype(BF16)
    qiT_ref[0] = nt(wqiT_ref).astype(BF16)
    kwT = nt(wkwT_ref) * kwscaleT_ref[...]
    kiT_ref[0] = kwT[:D_IDX]
    wT_ref[0] = kwT[D_IDX:D_IDX + H_IDX]
    kwb_ref[0] = kwT.T.astype(BF16)
    lxg = jnp.dot(hn, wl_ref[...], preferred_element_type=F32)
    lx_ref[0] = lxg[:, :LRU_W]
    lg_ref[0] = lxg[:, LRU_W:]
    qm = jnp.dot(hn, wqm_ref[...], preferred_element_type=F32)
    qmb_ref[0] = _group_rms(qm, g128_ref[...], mg_ref[...], DH_MEM).astype(BF16)


def _proj_t(x, p, tm):
    b, s, _ = x.shape
    bc = lambda col: jnp.broadcast_to(col, (col.shape[0], tm))
    consts = [p["g"], p["wqkT"], p["wvT"], p["wqiT"], p["wkwT"], p["wl"], p["wqm"], bc(p["qg_col"]),
              bc(p["kg_col"]), p["mg"], bc(p["kwscale_col"]), p["g64"], p["g128"]]
    tok = lambda w: pl.BlockSpec((1, tm, w), lambda i, j: (i, j, 0))
    feat = lambda w: pl.BlockSpec((1, w, tm), lambda i, j: (i, 0, j))
    outs = [
        (feat(HD), (b, HD, s), F32), (feat(HD), (b, HD, s), F32), (feat(D_IDX), (b, D_IDX, s), F32),
        (feat(H_IDX), (b, H_IDX, s), F32), (feat(HD), (b, HD, s), BF16), (feat(HD), (b, HD, s), BF16),
        (tok(HD), (b, s, HD), BF16), (tok(LANES), (b, s, LANES), BF16),
        (pl.BlockSpec((1, tm // KB, HD, KB), lambda i, j: (i, j, 0, 0)), (b, s // KB, HD, KB), BF16),
        (tok(LRU_W), (b, s, LRU_W), F32), (tok(LRU_W), (b, s, LRU_W), F32), (tok(MD), (b, s, MD), BF16),
    ]
    return pl.pallas_call(
        _proj_t_kernel,
        grid=(b, s // tm),
        in_specs=[tok(D_MODEL)] + [_const_spec(c.shape) for c in consts],
        out_specs=[o[0] for o in outs],
        out_shape=[jax.ShapeDtypeStruct(o[1], o[2]) for o in outs],
        compiler_params=_params(2),
        name="proj_t",
    )(x, *consts)


def _proj(x, p, tm):
    n = x.shape[0]
    consts = [p[k] for k in ("g", "wqkv", "wqi", "wkw", "wl", "wqm", "qg", "kg", "mg", "kwscale", "g64", "g128")]

    def row(w):
        return pl.BlockSpec((tm, w), lambda i: (i, 0))

    outs = [(HD, F32), (HD, F32), (LANES, F32), (HD, BF16), (HD, BF16), (HD, BF16), (HD, BF16), (LANES, BF16),
            (LRU_W, F32), (LRU_W, F32), (MD, BF16)]
    return pl.pallas_call(
        _proj_kernel,
        grid=(n // tm,),
        in_specs=[row(D_MODEL)] + [_const_spec(c.shape) for c in consts],
        out_specs=[row(w) for w, _ in outs],
        out_shape=[jax.ShapeDtypeStruct((n, w), dt) for w, dt in outs],
        compiler_params=_params(1),
        name="proj",
    )(x, *consts)


def _memkv_kernel(m_ref, g_ref, w_ref, kg_ref, g128_ref, mk_ref, mv_ref):
    hn = _rms(m_ref[...], g_ref[...]).astype(BF16)
    kv = jnp.dot(hn, w_ref[...], preferred_element_type=F32)
    mk_ref[...] = _group_rms(kv[:, :MD], g128_ref[...], kg_ref[...], DH_MEM)
    mv_ref[...] = kv[:, MD:]


def _memkv(mem, norm_g, w_mem_kv, mem_k_norm_g, g128):
    n = mem.shape[0]
    tm = min(n, 512)
    consts = [norm_g.reshape(1, D_MODEL), w_mem_kv.astype(BF16), jnp.tile(mem_k_norm_g, H_MEM).reshape(1, MD), g128]
    row = lambda w: pl.BlockSpec((tm, w), lambda i: (i, 0))
    return pl.pallas_call(
        _memkv_kernel,
        grid=(n // tm,),
        in_specs=[row(D_MODEL)] + [_const_spec(c.shape) for c in consts],
        out_specs=[row(MD), row(MD)],
        out_shape=[jax.ShapeDtypeStruct((n, MD), F32)] * 2,
        compiler_params=_params(1),
        name="memkv",
    )(mem, *consts)


def _memattn_kernel(q_ref, mk_ref, mv_ref, o_ref):
    q = q_ref[0]
    mk = mk_ref[0].astype(BF16)
    mv = mv_ref[0].astype(BF16)
    for h in range(H_MEM):
        sl = slice(h * DH_MEM, (h + 1) * DH_MEM)
        s = lax.dot_general(q[:, sl], mk[:, sl], NT_DIMS, preferred_element_type=F32) * (DH_MEM ** -0.5)
        m = jnp.max(s, axis=-1, keepdims=True)
        e = jnp.exp(s - m)
        p = (e / jnp.sum(e, axis=-1, keepdims=True)).astype(BF16)
        o_ref[0, :, sl] = jnp.dot(p, mv[:, sl], preferred_element_type=F32).astype(BF16)


def _memattn(qm, mk, mv, tm):
    b, t, _ = qm.shape
    n_mem = mk.shape[1]
    return pl.pallas_call(
        _memattn_kernel,
        grid=(b, t // tm),
        in_specs=[pl.BlockSpec((1, tm, MD), lambda i, j: (i, j, 0)),
                  pl.BlockSpec((1, n_mem, MD), lambda i, j: (i, 0, 0)),
                  pl.BlockSpec((1, n_mem, MD), lambda i, j: (i, 0, 0))],
        out_specs=pl.BlockSpec((1, tm, MD), lambda i, j: (i, j, 0)),
        out_shape=jax.ShapeDtypeStruct((b, t, MD), BF16),
        compiler_params=_params(2),
        name="memattn",
    )(qm, mk, mv)


def _shift_rows(x, k, fill):
    rows = lax.broadcasted_iota(I32, x.shape, 0)
    return jnp.where(rows >= k, pltpu.roll(x, k, 0), fill)


def _rglru_kernel(lx_ref, lg_ref, cs_ref, h0_ref, cw_ref, cb_ref, wa_ref, ba_ref, wi_ref, bi_ref, lam_ref,
                  y_ref, nb_ref, hl_ref, tail_ref, h_ref):
    t = pl.program_id(1)

    @pl.when(t == 0)
    def _():
        tail_ref[...] = cs_ref[0]
        h_ref[...] = h0_ref[0]

    x = lx_ref[0]
    tt = x.shape[0]
    tail = tail_ref[...]
    rows8 = lax.broadcasted_iota(I32, tail.shape, 0)
    conv = cb_ref[...] + x * cw_ref[CONV_W - 1:CONV_W, :]
    for d in range(1, CONV_W):
        xs = pltpu.roll(x, d, 0)
        head = jnp.where(rows8 < d, pltpu.roll(tail, d, 0), xs[:SUBLANES])
        xs = head if tt == SUBLANES else jnp.concatenate([head, xs[SUBLANES:]], axis=0)
        conv = conv + xs * cw_ref[CONV_W - 1 - d:CONV_W - d, :]
    tail_ref[...] = x[tt - SUBLANES:]
    nb_ref[0] = x[tt - SUBLANES:]

    cb16 = conv.astype(BF16)
    r = jax.nn.sigmoid(jnp.dot(cb16, wa_ref[...], preferred_element_type=F32) + ba_ref[...])
    ig = jax.nn.sigmoid(jnp.dot(cb16, wi_ref[...], preferred_element_type=F32) + bi_ref[...])
    nl = -lam_ref[...]
    softplus = jnp.maximum(nl, 0.0) + jnp.log1p(jnp.exp(-jnp.abs(nl)))
    log_a = -LRU_C * r * softplus
    a = jnp.exp(log_a)
    b = jnp.sqrt(-jnp.tanh(log_a) * (jnp.exp(2.0 * log_a) + 1.0)) * (ig * conv)
    rows = lax.broadcasted_iota(I32, x.shape, 0)
    b = b + jnp.where(rows == 0, a * h_ref[SUBLANES - 1:SUBLANES, :], 0.0)
    k = 1
    while k < tt:
        b = a * _shift_rows(b, k, 0.0) + b
        a = a * _shift_rows(a, k, 1.0)
        k *= 2
    h = b
    h_ref[...] = h[tt - SUBLANES:]
    hl_ref[0] = h[tt - SUBLANES:]
    y_ref[0] = (h * jax.nn.gelu(lg_ref[0])).astype(BF16)


def _block_diag(w):
    nb, bs, _ = w.shape
    eye = jnp.eye(nb, dtype=w.dtype)
    return (eye[:, None, :, None] * w[:, :, None, :]).reshape(nb * bs, nb * bs)


def _prep_rglru(conv_w, conv_b, lru_wa, lru_ba, lru_wi, lru_bi, lru_lambda):
    r = lambda v: v.reshape(1, LRU_W)
    return [conv_w, r(conv_b), _block_diag(lru_wa).astype(BF16), r(lru_ba), _block_diag(lru_wi).astype(BF16),
            r(lru_bi), r(lru_lambda)]


def _rglru(lx, lg, conv_state, h0, consts, tt):
    b, t, w = lx.shape
    cs = jnp.concatenate([jnp.zeros((b, SUBLANES - (CONV_W - 1), w), F32), conv_state], axis=1)
    h0p = jnp.concatenate([jnp.zeros((b, SUBLANES - 1, w), F32), h0[:, None, :]], axis=1)
    seq = pl.BlockSpec((1, tt, w), lambda i, j: (i, j, 0))
    st = pl.BlockSpec((1, SUBLANES, w), lambda i, j: (i, 0, 0))
    y, nb, hl = pl.pallas_call(
        _rglru_kernel,
        grid=(b, t // tt),
        in_specs=[seq, seq, st, st] + [_const_spec(c.shape) for c in consts],
        out_specs=[seq, st, st],
        out_shape=[jax.ShapeDtypeStruct((b, t, w), BF16), jax.ShapeDtypeStruct((b, SUBLANES, w), F32),
                   jax.ShapeDtypeStruct((b, SUBLANES, w), F32)],
        scratch_shapes=[pltpu.VMEM((SUBLANES, w), F32), pltpu.VMEM((SUBLANES, w), F32)],
        compiler_params=pltpu.CompilerParams(dimension_semantics=("parallel", "arbitrary"),
                                             vmem_limit_bytes=VMEM_LIMIT),
        name="rglru",
    )(lx, lg, cs, h0p, *consts)
    return y, nb[:, SUBLANES - (CONV_W - 1):], hl[:, SUBLANES - 1]


def _score_key(score):
    bits = lax.bitcast_convert_type(score, I32)
    return bits ^ ((bits >> 31) & 0x7FFFFFFF)


def _radix_threshold(count_ge, rows, topk):
    def bit_step(i, thr_u):
        cand_u = thr_u | (jnp.int32(1) << (31 - i))
        cnt = count_ge(cand_u ^ INT_MIN)
        return jnp.where(cnt >= topk, cand_u, thr_u)

    thr_u = lax.fori_loop(0, 32, bit_step, jnp.zeros((rows, 1), I32))
    return jnp.maximum(thr_u ^ INT_MIN, KEY_NEG_INF + 1)


def _dsa_prompt_kernel(qT_ref, qiT_ref, wT_ref, kw_ref, k_ref, vT_ref, tri_ref, o_ref,
                       sc_ref, qpad_ref, qipad_ref, oT_ref, s_scr, p_scr, *, topk):
    i = pl.program_id(1)
    nk = i + 1
    kf = float(topk)
    kblock = lambda j: pl.ds(pl.multiple_of(j * KB, KB), KB)

    zeros64 = jnp.zeros((D_IDX, TQ), BF16)
    for h in range(H_IDX):
        qipad_ref[h] = jnp.concatenate([qiT_ref[0, h * D_IDX:(h + 1) * D_IDX, :], zeros64], axis=0)
    for h in range(H_ATT):
        qh = qT_ref[0, h * DH_ATT:(h + 1) * DH_ATT, :]
        qpad_ref[h] = jnp.concatenate([qh, zeros64] if h % 2 == 0 else [zeros64, qh], axis=0)
    wT = wT_ref[0]

    def block_scores(j):
        kw = kw_ref[0, kblock(j), :]
        acc = jnp.zeros((KB, TQ), F32)
        for h in range(H_IDX):
            d = jnp.dot(kw, qipad_ref[h], preferred_element_type=F32)
            acc = acc + jnp.maximum(d, 0.0) * wT[h:h + 1, :]
        return acc

    def score_block(j, carry):
        lo, hi = carry
        acc = block_scores(j)
        sc_ref[kblock(j), :] = acc
        return (jnp.minimum(lo, jnp.min(acc, axis=0, keepdims=True)),
                jnp.maximum(hi, jnp.max(acc, axis=0, keepdims=True)))

    init = (jnp.full((1, TQ), jnp.inf, F32), jnp.full((1, TQ), NEG_INF, F32))
    rmin, rmax = lax.fori_loop(0, i, score_block, init)
    acc = block_scores(i)
    krow = lax.broadcasted_iota(I32, (KB, TQ), 0)
    qcol = lax.broadcasted_iota(I32, (KB, TQ), 1)
    sc_ref[kblock(i), :] = jnp.where(krow <= qcol, acc, NEG_INF)
    rmin = jnp.minimum(rmin, jnp.min(acc, axis=0, keepdims=True))
    rmax = jnp.maximum(rmax, jnp.max(acc, axis=0, keepdims=True))

    def count(pred):
        def body(j, cnt):
            m = jnp.where(pred(sc_ref[kblock(j), :]), 1.0, 0.0)
            for r in range(KB // CNT_ROWS):
                cnt = cnt + m[r * CNT_ROWS:(r + 1) * CNT_ROWS]
            return cnt

        cnt = lax.fori_loop(0, nk, body, jnp.zeros((CNT_ROWS, TQ), F32))
        return jnp.sum(cnt, axis=0, keepdims=True)

    def n_open(done):
        return jnp.sum(1.0 - done).astype(I32)

    done0 = jnp.where(count(lambda x: x >= rmin) <= kf, 1.0, 0.0)

    def bisect(c):
        it, lo, hi, done, _ = c
        mid = 0.5 * lo + 0.5 * jnp.minimum(hi, rmax)
        c_mid = count(lambda x: x >= mid)
        ge = c_mid >= kf
        live = done < 0.5
        lo = jnp.where(live & ge, mid, lo)
        hi = jnp.where(live & jnp.logical_not(ge), mid, hi)
        done = jnp.maximum(done, jnp.where(c_mid == kf, 1.0, 0.0))
        return it + 1, lo, hi, done, n_open(done)

    _, lo, hi, done, left = lax.while_loop(
        lambda c: (c[0] < BISECT_STEPS) & (c[4] > 0), bisect,
        (jnp.int32(0), rmin, jnp.full((1, TQ), jnp.inf, F32), done0, n_open(done0)))

    def step_down(c):
        lo, hi, done, _ = c

        def body(j, best):
            x = sc_ref[kblock(j), :]
            return jnp.maximum(best, jnp.max(jnp.where(x < hi, x, NEG_INF), axis=0, keepdims=True))

        cand = lax.fori_loop(0, nk, body, jnp.full((1, TQ), NEG_INF, F32))
        ok = count(lambda x: x >= cand) >= kf
        live = done < 0.5
        lo = jnp.where(live & ok, cand, lo)
        hi = jnp.where(live & jnp.logical_not(ok), cand, hi)
        done = jnp.maximum(done, jnp.where(ok, 1.0, 0.0))
        return lo, hi, done, n_open(done)

    thr, _, _, _ = lax.while_loop(lambda c: c[3] > 0, step_down, (lo, hi, done, left))
    n_tied = jnp.sum(jnp.where(count(lambda x: x >= thr) > kf, 1.0, 0.0)).astype(I32)

    @pl.when(n_tied == 0)
    def _():
        def body(j, carry):
            x = sc_ref[kblock(j), :]
            sc_ref[kblock(j), :] = jnp.where(x >= thr, 0.0, NEG_INF)
            return carry

        lax.fori_loop(0, nk, body, 0)

    @pl.when(n_tied > 0)
    def _():
        quota = kf - count(lambda x: x > thr)

        def body(j, ties_before):
            x = sc_ref[kblock(j), :]
            eq = jnp.where(x == thr, 1.0, 0.0)
            rank = jnp.dot(tri_ref[...], eq.astype(BF16), preferred_element_type=F32) + ties_before
            sel = (x > thr) | ((x == thr) & (rank < quota))
            sc_ref[kblock(j), :] = jnp.where(sel, 0.0, NEG_INF)
            return ties_before + jnp.sum(eq, axis=0, keepdims=True)

        lax.fori_loop(0, nk, body, jnp.zeros((1, TQ), F32))

    def attend(j, carry):
        ms, ls, accs = carry
        new_ms, new_ls, new_accs = [], [], []

        def qk(h):
            pair = slice((h // 2) * LANES, (h // 2 + 1) * LANES)
            return jnp.dot(k_ref[0, kblock(j), pair], qpad_ref[h], preferred_element_type=F32)

        block_max = []
        for h in range(H_ATT):
            s = qk(h) + sc_ref[kblock(j), :]
            s_scr[h] = s
            block_max.append(jnp.max(s, axis=0, keepdims=True))
        alphas = []
        for h in range(H_ATT):
            m_new = jnp.maximum(ms[h], block_max[h])
            m_safe = jnp.where(m_new == NEG_INF, 0.0, m_new)
            alphas.append(jnp.exp2(ms[h] - m_safe))
            p = jnp.exp2(s_scr[h] - m_safe)
            new_ms.append(m_new)
            new_ls.append(alphas[h] * ls[h] + jnp.sum(p, axis=0, keepdims=True))
            p_scr[h] = p.astype(BF16)
        for h in range(H_ATT):
            hrows = slice(h * DH_ATT, (h + 1) * DH_ATT)
            pv = jnp.dot(vT_ref[0, j, hrows, :], p_scr[h], preferred_element_type=F32)
            new_accs.append(alphas[h] * accs[h] + pv)
        return tuple(new_ms), tuple(new_ls), tuple(new_accs)

    init = (tuple(jnp.full((1, TQ), NEG_INF, F32) for _ in range(H_ATT)),
            tuple(jnp.zeros((1, TQ), F32) for _ in range(H_ATT)),
            tuple(jnp.zeros((DH_ATT, TQ), F32) for _ in range(H_ATT)))
    _, ls, accs = lax.fori_loop(0, nk, attend, init)
    for h in range(H_ATT):
        oT_ref[h * DH_ATT:(h + 1) * DH_ATT, :] = accs[h] / ls[h]
    o_ref[0] = oT_ref[...].T.astype(BF16)


def _dsa_prompt(qT, qiT, wT, kwb, kb, vTb):
    b, _, s = qT.shape
    topk = min(TOPK_MAX, s // 4)
    tri = (jnp.arange(KB)[:, None] > jnp.arange(KB)[None, :]).astype(BF16)
    feat = lambda w: pl.BlockSpec((1, w, TQ), lambda i, j: (i, 0, j))
    full = lambda w: pl.BlockSpec((1, s, w), lambda i, j: (i, 0, 0))
    return pl.pallas_call(
        functools.partial(_dsa_prompt_kernel, topk=topk),
        grid=(b, s // TQ),
        in_specs=[feat(HD), feat(HD), feat(H_IDX), full(LANES), full(HD),
                  pl.BlockSpec((1, s // KB, HD, KB), lambda i, j: (i, 0, 0, 0)), _const_spec(tri.shape)],
        out_specs=pl.BlockSpec((1, TQ, HD), lambda i, j: (i, j, 0)),
        out_shape=jax.ShapeDtypeStruct((b, s, HD), BF16),
        scratch_shapes=[pltpu.VMEM((s, TQ), F32), pltpu.VMEM((H_ATT, LANES, TQ), BF16),
                        pltpu.VMEM((H_IDX, LANES, TQ), BF16), pltpu.VMEM((HD, TQ), F32),
                        pltpu.VMEM((H_ATT, KB, TQ), F32), pltpu.VMEM((H_ATT, KB, TQ), BF16)],
        compiler_params=_params(2),
        name="dsa_prompt",
    )(qT, qiT, wT, kwb, kb, vTb, tri)


PAGE_GROUP_IDX = 16
PAGE_GROUP_KV = 8


def _dsa_sample_select_kernel(pt_ref, qi_ref, kwq_ref, tri_ref, *rest, n_pages, topk, group):
    page_refs, (bias_ref, biasn_ref, qiall_ref, wb_ref) = rest[:group], rest[group:]
    pg = pl.program_id(1)
    t = qi_ref.shape[1]
    kwq = kwq_ref[0]

    @pl.when(pg == 0)
    def _():
        qi = qi_ref[0].astype(F32)
        qiall_ref[...] = jnp.concatenate(
            [qi[:, h * D_IDX:(h + 1) * D_IDX] for h in range(H_IDX)], axis=0).astype(BF16)
        for h in range(H_IDX):
            wb_ref[h] = jnp.broadcast_to(kwq[:, D_IDX + h:D_IDX + h + 1], (t, LANES))

    def scores(dots):
        acc = jnp.zeros((t, dots.shape[1]), F32)
        for h in range(H_IDX):
            acc = acc + jnp.maximum(dots[h * t:(h + 1) * t], 0.0) * wb_ref[h]
        return acc

    for g in range(group):
        dots = jnp.dot(qiall_ref[...], page_refs[g][0].astype(BF16), preferred_element_type=F32)
        bias_ref[0, pg * group + g] = _score_key(scores(dots))

    @pl.when(pg == pl.num_programs(1) - 1)
    def _():
        new_keys = jnp.concatenate([kwq[:, :D_IDX], jnp.zeros((LANES - t, D_IDX), F32)], axis=0).astype(BF16)
        row = lax.broadcasted_iota(I32, (t, LANES), 0)
        col = lax.broadcasted_iota(I32, (t, LANES), 1)
        dots_new = lax.dot_general(qiall_ref[...], new_keys, NT_DIMS, preferred_element_type=F32)
        key_new = _score_key(jnp.where(col <= row, scores(dots_new), NEG_INF))

        def count(cmp_fn):
            cnt = jnp.sum(cmp_fn(bias_ref[0]).astype(I32), axis=0) + cmp_fn(key_new).astype(I32)
            return jnp.sum(cnt, axis=1, keepdims=True)

        thr = _radix_threshold(lambda c: count(lambda key: key >= c), t, topk)
        n_inexact = jnp.sum(jnp.where(count(lambda key: key >= thr) == topk, 0, 1))
        as_bias = lambda sel: lax.bitcast_convert_type(jnp.where(sel, 0.0, NEG_INF), I32)

        @pl.when(n_inexact == 0)
        def _():
            bias_ref[0] = as_bias(bias_ref[0] >= thr)
            biasn_ref[0] = as_bias(key_new >= thr)

        @pl.when(n_inexact > 0)
        def _():
            quota = (topk - count(lambda key: key > thr)).astype(F32)

            def select(key, ties_before):
                eq = key == thr
                eqf = eq.astype(F32)
                rank = jnp.dot(eqf.astype(BF16), tri_ref[...], preferred_element_type=F32) + ties_before
                sel = (key > thr) | (eq & (rank < quota))
                return as_bias(sel), ties_before + jnp.sum(eqf, axis=1, keepdims=True)

            def bias_page(p, ties_before):
                bias, ties = select(bias_ref[0, p], ties_before)
                bias_ref[0, p] = bias
                return ties

            ties = lax.fori_loop(0, n_pages, bias_page, jnp.zeros((t, 1), F32))
            biasn_ref[0], _ = select(key_new, ties)


def _dsa_sample_select(page_table, qib, kw32, cache_kidx_t):
    b, t, _ = qib.shape
    n_pages = page_table.shape[1]
    page = cache_kidx_t.shape[2]
    topk = min(TOPK_MAX, (n_pages * page + t) // 4)
    group = PAGE_GROUP_IDX
    tri = (jnp.arange(page)[:, None] < jnp.arange(page)[None, :]).astype(BF16)
    tok = lambda w: pl.BlockSpec((1, t, w), lambda i, j, pt: (i, 0, 0))
    page_specs = [pl.BlockSpec((1, D_IDX, page), lambda i, j, pt, g=g: (pt[i, j * group + g], 0, 0))
                  for g in range(group)]
    grid_spec = pltpu.PrefetchScalarGridSpec(
        num_scalar_prefetch=1,
        grid=(b, n_pages // group),
        in_specs=[tok(H_IDX * D_IDX), tok(LANES), pl.BlockSpec(tri.shape, lambda i, j, pt: (0, 0))] + page_specs,
        out_specs=[pl.BlockSpec((1, n_pages, t, page), lambda i, j, pt: (i, 0, 0, 0)),
                   pl.BlockSpec((1, t, LANES), lambda i, j, pt: (i, 0, 0))],
        scratch_shapes=[pltpu.VMEM((H_IDX * t, D_IDX), BF16), pltpu.VMEM((H_IDX, t, LANES), F32)],
    )
    return pl.pallas_call(
        functools.partial(_dsa_sample_select_kernel, n_pages=n_pages, topk=topk, group=group),
        grid_spec=grid_spec,
        out_shape=[jax.ShapeDtypeStruct((b, n_pages, t, page), I32), jax.ShapeDtypeStruct((b, t, LANES), I32)],
        compiler_params=pltpu.CompilerParams(dimension_semantics=("parallel", "arbitrary"),
                                             vmem_limit_bytes=VMEM_LIMIT),
        name="dsa_sample_select",
    )(page_table, qib, kw32, tri, *([cache_kidx_t] * group))


def _dsa_sample_attend_kernel(pt_ref, q_ref, kn_ref, vn_ref, bias_ref, biasn_ref, *rest, group):
    kT_refs, vT_refs = rest[:group], rest[group:2 * group]
    o_ref, m_ref, l_ref, acc_ref = rest[2 * group:]
    pg = pl.program_id(1)
    t = q_ref.shape[1]
    page = kT_refs[0].shape[2]
    lane_head = lax.broadcasted_iota(I32, (t, HD), 1) // DH_ATT
    q = q_ref[0].astype(F32)
    qbd = jnp.concatenate([jnp.where(lane_head == h, q, 0.0) for h in range(H_ATT)], axis=0).astype(BF16)

    @pl.when(pg == 0)
    def _():
        m_ref[...] = jnp.full(m_ref.shape, NEG_INF, F32)
        l_ref[...] = jnp.zeros(l_ref.shape, F32)
        acc_ref[...] = jnp.zeros(acc_ref.shape, F32)

    def update(s, bias, pv):
        s = s + jnp.concatenate([lax.bitcast_convert_type(bias, F32)] * H_ATT, axis=0)
        m = m_ref[...]
        m_new = jnp.maximum(m, jnp.max(s, axis=1, keepdims=True))
        m_safe = jnp.where(m_new == NEG_INF, 0.0, m_new)
        alpha = jnp.exp(m - m_safe)
        p = jnp.exp(s - m_safe)
        l_ref[...] = alpha * l_ref[...] + jnp.sum(p, axis=1, keepdims=True)
        acc_ref[...] = alpha * acc_ref[...] + pv(p.astype(BF16))
        m_ref[...] = m_new

    s_pages = jnp.concatenate(
        [jnp.dot(qbd, kT_refs[g][0].astype(BF16), preferred_element_type=F32) for g in range(group)], axis=1)
    bias_pages = jnp.concatenate([bias_ref[0, g] for g in range(group)], axis=1)

    def pv_pages(p):
        out = jnp.zeros((H_ATT * t, HD), F32)
        for g in range(group):
            out = out + lax.dot_general(p[:, g * page:(g + 1) * page], vT_refs[g][0].astype(BF16), NT_DIMS,
                                        preferred_element_type=F32)
        return out

    update(s_pages, bias_pages, pv_pages)

    @pl.when(pg == pl.num_programs(1) - 1)
    def _():
        pad = jnp.zeros((LANES - t, HD), F32)
        kn = jnp.concatenate([kn_ref[0], pad], axis=0).astype(BF16)
        vn = jnp.concatenate([vn_ref[0], pad], axis=0).astype(BF16)
        update(lax.dot_general(qbd, kn, NT_DIMS, preferred_element_type=F32), biasn_ref[0],
               lambda p: jnp.dot(p, vn, preferred_element_type=F32))
        o = acc_ref[...] / l_ref[...]
        out = jnp.zeros((t, HD), F32)
        for h in range(H_ATT):
            out = out + jnp.where(lane_head == h, o[h * t:(h + 1) * t], 0.0)
        o_ref[0] = out.astype(BF16)


def _dsa_sample_attend(page_table, qb, k32, v32, bias, bias_new, cache_kt, cache_vt):
    b, t, _ = qb.shape
    n_pages = page_table.shape[1]
    page = cache_kt.shape[2]
    group = PAGE_GROUP_KV
    tok = lambda w: pl.BlockSpec((1, t, w), lambda i, j, pt: (i, 0, 0))
    kv_specs = [pl.BlockSpec((1, HD, page), lambda i, j, pt, g=g: (pt[i, j * group + g], 0, 0))
                for g in range(group)]
    grid_spec = pltpu.PrefetchScalarGridSpec(
        num_scalar_prefetch=1,
        grid=(b, n_pages // group),
        in_specs=[tok(HD), tok(HD), tok(HD),
                  pl.BlockSpec((1, group, t, page), lambda i, j, pt: (i, j, 0, 0)),
                  tok(LANES)] + kv_specs + kv_specs,
        out_specs=tok(HD),
        scratch_shapes=[pltpu.VMEM((H_ATT * t, 1), F32), pltpu.VMEM((H_ATT * t, 1), F32),
                        pltpu.VMEM((H_ATT * t, HD), F32)],
    )
    return pl.pallas_call(
        functools.partial(_dsa_sample_attend_kernel, group=group),
        grid_spec=grid_spec,
        out_shape=jax.ShapeDtypeStruct((b, t, HD), BF16),
        compiler_params=pltpu.CompilerParams(dimension_semantics=("parallel", "arbitrary"),
                                             vmem_limit_bytes=VMEM_LIMIT),
        name="dsa_sample_attend",
    )(page_table, qb, k32, v32, bias, bias_new, *([cache_kt] * group), *([cache_vt] * group))


def _merge_kernel(x_ref, oa_ref, ol_ref, om_ref, g_ref, wg_ref, wa_ref, wl_ref, wm_ref, wo_ref, o_ref):
    x = x_ref[...]
    hn = _rms(x, g_ref[...]).astype(BF16)
    m = jnp.zeros_like(x)
    for idx, (o_r, w_r) in enumerate(((oa_ref, wa_ref), (ol_ref, wl_ref), (om_ref, wm_ref))):
        gate = jax.nn.sigmoid(jnp.dot(hn, wg_ref[:, idx * D_MODEL:(idx + 1) * D_MODEL], preferred_element_type=F32))
        m = m + gate * jnp.dot(o_r[...], w_r[...], preferred_element_type=F32)
    o_ref[...] = x + jnp.dot(m.astype(BF16), wo_ref[...], preferred_element_type=F32)


def _merge(x, o_att, o_lru, o_mem, consts, tm):
    n = x.shape[0]
    row = lambda w: pl.BlockSpec((tm, w), lambda i: (i, 0))
    return pl.pallas_call(
        _merge_kernel,
        grid=(n // tm,),
        in_specs=[row(D_MODEL), row(HD), row(LRU_W), row(MD)] + [_const_spec(c.shape) for c in consts],
        out_specs=row(D_MODEL),
        out_shape=jax.ShapeDtypeStruct((n, D_MODEL), F32),
        compiler_params=_params(1),
        name="merge",
    )(x, o_att, o_lru, o_mem, *consts)


def _token_tile(n):
    return min(n, 512)


def _layer(x, is_prompt, lw, pp, extra):
    b, t, _ = x.shape
    n = b * t
    tm = _token_tile(n)
    x1 = _ffn(x.reshape(n, D_MODEL), pp["ffn1"], tm)
    r3 = lambda a: a.reshape(b, t, a.shape[-1])
    if is_prompt:
        kT32, vT32, kiT32, wT, qT, qiT, kb, kwb, vTb, lx, lg, qmb = _proj_t(r3(x1), pp["proj"], tm)
        o_att = _dsa_prompt(qT, qiT, wT, kwb, kb, vTb)
        k_new = kT32.reshape(b, H_ATT, DH_ATT, t).transpose(0, 3, 1, 2)
        v_new = vT32.reshape(b, H_ATT, DH_ATT, t).transpose(0, 3, 1, 2)
        ki_new = kiT32.transpose(0, 2, 1)
        conv_state = jnp.zeros((b, CONV_W - 1, LRU_W), F32)
        h0 = jnp.zeros((b, LRU_W), F32)
        mem = extra["mem"]
        mk, mv = _memkv(mem.reshape(-1, D_MODEL), lw["norm_mem_g"], lw["w_mem_kv"], lw["mem_k_norm_g"],
                        pp["proj"]["g128"])
        mk = mk.reshape(b, -1, MD)
        mv = mv.reshape(b, -1, MD)
    else:
        k32, v32, kw32, qb, _, _, qib, _, lx, lg, qmb = _proj(x1, pp["proj"], tm)
        pt = extra["page_table"]
        bias, bias_new = _dsa_sample_select(pt, r3(qib), r3(kw32), extra["cache_kidx_t"])
        o_att = _dsa_sample_attend(pt, r3(qb), r3(k32), r3(v32), bias, bias_new, extra["cache_kt"],
                                   extra["cache_vt"])
        k_new = k32.reshape(b, t, H_ATT, DH_ATT)
        v_new = v32.reshape(b, t, H_ATT, DH_ATT)
        ki_new = r3(kw32)[:, :, :D_IDX]
        conv_state, h0 = extra["state_conv"], extra["state_h"]
        mk, mv = extra["cache_mem_k"], extra["cache_mem_v"]
    o_lru, conv_buf, h_last = _rglru(r3(lx), r3(lg), conv_state, h0, pp["rglru"], min(t, 256))
    o_mem = _memattn(r3(qmb), mk, mv, min(t, 512))
    x2 = _merge(x1, o_att.reshape(n, HD), o_lru.reshape(n, LRU_W), o_mem.reshape(n, MD), pp["merge"], tm)
    y = _ffn(x2, pp["ffn2"], tm).reshape(b, t, D_MODEL)
    if is_prompt:
        state = (k_new, v_new, ki_new, mk.reshape(b, -1, H_MEM, DH_MEM), mv.reshape(b, -1, H_MEM, DH_MEM),
                 conv_buf, h_last)
    else:
        state = (k_new, v_new, ki_new, conv_buf, h_last)
    return y, state


def kernel(x_prompt, x_sample, cache_k, cache_v, cache_kidx, page_table, cache_mem_k, cache_mem_v, state_conv, state_h, mem_prompt, norm_ffn1_g, w_ffn1_in, w_ffn1_out, norm_mix_g, w_in, q_norm_g, k_norm_g, w_attn_o, conv_w, conv_b, lru_wa, lru_ba, lru_wi, lru_bi, lru_lambda, w_lru_o, norm_mem_g, w_mem_kv, mem_q_norm_g, mem_k_norm_g, w_mem_o, w_out, norm_ffn2_g, w_ffn2_in, w_ffn2_out):
    depth = w_in.shape[0]
    n_phys, page = cache_k.shape[1], cache_k.shape[2]
    xp, xs = x_prompt, x_sample
    p_states, s_states = [], []
    for l in range(depth):
        proj = _prep_proj(norm_mix_g[l], w_in[l], q_norm_g[l], k_norm_g[l], mem_q_norm_g[l])
        pp = dict(
            ffn1=_prep_ffn(norm_ffn1_g[l], w_ffn1_in[l], w_ffn1_out[l]),
            ffn2=_prep_ffn(norm_ffn2_g[l], w_ffn2_in[l], w_ffn2_out[l]),
            proj=proj,
            rglru=_prep_rglru(conv_w[l], conv_b[l], lru_wa[l], lru_ba[l], lru_wi[l], lru_bi[l], lru_lambda[l]),
            merge=[proj["g"], proj["wgates"], w_attn_o[l].astype(BF16), w_lru_o[l].astype(BF16),
                   w_mem_o[l].astype(BF16), w_out[l].astype(BF16)],
        )
        lw = dict(norm_mem_g=norm_mem_g[l], w_mem_kv=w_mem_kv[l], mem_k_norm_g=mem_k_norm_g[l])
        xp, st_p = _layer(xp, True, lw, pp, dict(mem=mem_prompt))
        xs, st_s = _layer(xs, False, lw, pp, dict(
            page_table=page_table,
            cache_kt=cache_k[l].transpose(0, 2, 3, 1).reshape(n_phys, HD, page),
            cache_vt=cache_v[l].transpose(0, 2, 3, 1).reshape(n_phys, HD, page),
            cache_kidx_t=cache_kidx[l].transpose(0, 2, 1), cache_mem_k=cache_mem_k[l].reshape(-1, cache_mem_k.shape[2], MD),
            cache_mem_v=cache_mem_v[l].reshape(-1, cache_mem_v.shape[2], MD),
            state_conv=state_conv[l], state_h=state_h[l]))
        p_states.append(st_p)
        s_states.append(st_s)
    stack = lambda states, i: jnp.stack([s[i] for s in states])
    return (xp, xs) + tuple(stack(p_states, i) for i in range(7)) + tuple(stack(s_states, i) for i in range(5))
```

```python
import functools

import jax
import jax.numpy as jnp
from jax import lax
from jax.experimental import pallas as pl
from jax.experimental.pallas import tpu as pltpu

F32, BF16, I32 = jnp.float32, jnp.bfloat16, jnp.int32

D_MODEL = 1024
H_ATT, DH_ATT = 8, 64
H_IDX, D_IDX = 8, 64
TOPK_MAX = 256
LRU_W, LRU_BLOCKS, CONV_W, LRU_C = 512, 8, 4, 8.0
H_MEM, DH_MEM = 4, 128
D_FF = 2816
EPS = 1e-6
HD = H_ATT * DH_ATT
MD = H_MEM * DH_MEM

LANES = 128
SUBLANES = 8
VMEM_BYTES_V7X = 64 * 1024 * 1024
VMEM_LIMIT = VMEM_BYTES_V7X - 8 * 1024 * 1024

FF_CHUNK = 256
TQ = 256
KB = 256
CNT_ROWS = 32
BISECT_STEPS = 15
LOG2E = 1.4426950408889634
INT_MIN = -2 ** 31
KEY_NEG_INF = -2139095041
NEG_INF = float("-inf")

NT_DIMS = (((1,), (1,)), ((), ()))


def _params(n_grid, parallel=True):
    sem = ("parallel" if parallel else "arbitrary",) * n_grid
    return pltpu.CompilerParams(dimension_semantics=sem, vmem_limit_bytes=VMEM_LIMIT)


def _const_spec(shape):
    nd = len(shape)
    return pl.BlockSpec(shape, lambda *_: (0,) * nd)


def _rms(x, g):
    ms = jnp.mean(x * x, axis=-1, keepdims=True)
    return x * lax.rsqrt(ms + EPS) * g


def _group_rms(x, gmat, g, group):
    x2 = x * x
    hi = x2.astype(BF16)
    lo = (x2 - hi.astype(F32)).astype(BF16)
    ss = jnp.dot(hi, gmat, preferred_element_type=F32) + jnp.dot(lo, gmat, preferred_element_type=F32)
    return x * lax.rsqrt(ss * (1.0 / group) + EPS) * g


def _group_matrix(width, group):
    idx = jnp.arange(width) // group
    return (idx[:, None] == idx[None, :]).astype(BF16)


def _ffn_kernel(x_ref, g_ref, wg_ref, wu_ref, wo_ref, o_ref):
    x = x_ref[...]
    hn = _rms(x, g_ref[...]).astype(BF16)
    acc = jnp.zeros_like(x)
    for c in range(wg_ref.shape[0]):
        gate = jnp.dot(hn, wg_ref[c], preferred_element_type=F32)
        up = jnp.dot(hn, wu_ref[c], preferred_element_type=F32)
        act = (gate * jax.nn.sigmoid(gate) * up).astype(BF16)
        acc = acc + jnp.dot(act, wo_ref[c], preferred_element_type=F32)
    o_ref[...] = x + 0.5 * acc


def _prep_ffn(g, w_in, w_out):
    nc = D_FF // FF_CHUNK
    wg = w_in[:, :D_FF].reshape(D_MODEL, nc, FF_CHUNK).transpose(1, 0, 2).astype(BF16)
    wu = w_in[:, D_FF:].reshape(D_MODEL, nc, FF_CHUNK).transpose(1, 0, 2).astype(BF16)
    wo = w_out.reshape(nc, FF_CHUNK, D_MODEL).astype(BF16)
    return g.reshape(1, D_MODEL), wg, wu, wo


def _ffn(x, prep, tm):
    g, wg, wu, wo = prep
    n = x.shape[0]
    row = pl.BlockSpec((tm, D_MODEL), lambda i: (i, 0))
    return pl.pallas_call(
        _ffn_kernel,
        grid=(n // tm,),
        in_specs=[row, _const_spec(g.shape), _const_spec(wg.shape), _const_spec(wu.shape), _const_spec(wo.shape)],
        out_specs=row,
        out_shape=jax.ShapeDtypeStruct((n, D_MODEL), F32),
        compiler_params=_params(1),
        name="ffn",
    )(x, g, wg, wu, wo)


def _proj_kernel(x_ref, g_ref, wqkv_ref, wqi_ref, wkw_ref, wl_ref, wqm_ref, qg_ref, kg_ref, mg_ref,
                 kwscale_ref, g64_ref, g128_ref,
                 k32_ref, v32_ref, kw32_ref, qb_ref, kb_ref, vb_ref, qib_ref, kwb_ref, lx_ref, lg_ref, qmb_ref):
    hn = _rms(x_ref[...], g_ref[...]).astype(BF16)
    qkv = jnp.dot(hn, wqkv_ref[...], preferred_element_type=F32)
    q = _group_rms(qkv[:, :HD], g64_ref[...], qg_ref[...], DH_ATT)
    k = _group_rms(qkv[:, HD:2 * HD], g64_ref[...], kg_ref[...], DH_ATT)
    v = qkv[:, 2 * HD:]
    k32_ref[...] = k
    v32_ref[...] = v
    qb_ref[...] = (q * (DH_ATT ** -0.5)).astype(BF16)
    kb_ref[...] = k.astype(BF16)
    vb_ref[...] = v.astype(BF16)
    qib_ref[...] = jnp.dot(hn, wqi_ref[...], preferred_element_type=F32).astype(BF16)
    kw = jnp.dot(hn, wkw_ref[...], preferred_element_type=F32) * kwscale_ref[...]
    kw32_ref[...] = kw
    kwb_ref[...] = kw.astype(BF16)
    lxg = jnp.dot(hn, wl_ref[...], preferred_element_type=F32)
    lx_ref[...] = lxg[:, :LRU_W]
    lg_ref[...] = lxg[:, LRU_W:]
    qm = jnp.dot(hn, wqm_ref[...], preferred_element_type=F32)
    qmb_ref[...] = _group_rms(qm, g128_ref[...], mg_ref[...], DH_MEM).astype(BF16)


def _prep_proj(norm_g, w_in, q_norm_g, k_norm_g, mem_q_norm_g):
    o = 0
    cols = {}
    for name, size in (("q", HD), ("k", HD), ("v", HD), ("qi", H_IDX * D_IDX), ("ki", D_IDX), ("wi", H_IDX),
                       ("lx", LRU_W), ("lg", LRU_W), ("qm", MD), ("gates", 3 * D_MODEL)):
        cols[name] = w_in[:, o:o + size]
        o += size
    pad = jnp.zeros((D_MODEL, LANES - D_IDX - H_IDX), w_in.dtype)
    wqkv = jnp.concatenate([cols["q"], cols["k"], cols["v"]], axis=1).astype(BF16)
    wkw = jnp.concatenate([cols["ki"], cols["wi"], pad], axis=1).astype(BF16)
    wl = jnp.concatenate([cols["lx"], cols["lg"]], axis=1).astype(BF16)
    kwscale = jnp.concatenate([jnp.ones((D_IDX,), F32),
                               jnp.full((H_IDX,), H_IDX ** -0.5 * D_IDX ** -0.5, F32),
                               jnp.zeros((LANES - D_IDX - H_IDX,), F32)]).reshape(1, LANES)
    return dict(
        g=norm_g.reshape(1, D_MODEL), wqkv=wqkv, wqi=cols["qi"].astype(BF16), wkw=wkw, wl=wl,
        wqm=cols["qm"].astype(BF16),
        qg=jnp.tile(q_norm_g, H_ATT).reshape(1, HD), kg=jnp.tile(k_norm_g, H_ATT).reshape(1, HD),
        mg=jnp.tile(mem_q_norm_g, H_MEM).reshape(1, MD), kwscale=kwscale,
        g64=_group_matrix(HD, DH_ATT), g128=_group_matrix(MD, DH_MEM),
        wgates=cols["gates"].astype(BF16),
        wqkT=wqkv[:, :2 * HD].T, wvT=wqkv[:, 2 * HD:].T, wqiT=cols["qi"].astype(BF16).T, wkwT=wkw.T,
        qg_col=jnp.tile(q_norm_g, H_ATT).reshape(HD, 1), kg_col=jnp.tile(k_norm_g, H_ATT).reshape(HD, 1),
        kwscale_col=kwscale.reshape(LANES, 1),
    )


def _group_rms_t(xt, gmat, g, group):
    x2 = xt * xt
    hi = x2.astype(BF16)
    lo = (x2 - hi.astype(F32)).astype(BF16)
    ss = jnp.dot(gmat, hi, preferred_element_type=F32) + jnp.dot(gmat, lo, preferred_element_type=F32)
    return xt * lax.rsqrt(ss * (1.0 / group) + EPS) * g


def _proj_t_kernel(x_ref, g_ref, wqkT_ref, wvT_ref, wqiT_ref, wkwT_ref, wl_ref, wqm_ref, qgT_ref, kgT_ref, mg_ref,
                   kwscaleT_ref, g64_ref, g128_ref,
                   kT_ref, vT_ref, kiT_ref, wT_ref, qT_ref, qiT_ref, kb_ref, kwb_ref, vTb_ref, lx_ref, lg_ref,
                   qmb_ref):
    hn = _rms(x_ref[0], g_ref[...]).astype(BF16)
    nt = lambda w_ref: lax.dot_general(w_ref[...], hn, NT_DIMS, preferred_element_type=F32)
    qkT = nt(wqkT_ref)
    qT = _group_rms_t(qkT[:HD], g64_ref[...], qgT_ref[...], DH_ATT)
    kT = _group_rms_t(qkT[HD:], g64_ref[...], kgT_ref[...], DH_ATT)
    kT_ref[0] = kT
    kb_ref[0] = kT.T.astype(BF16)
    qT_ref[0] = (qT * (DH_ATT ** -0.5 * LOG2E)).astype(BF16)
    vT = nt(wvT_ref)
    vT_ref[0] = vT
    for c in range(vTb_ref.shape[1]):
        vTb_ref[0, c] = vT[:, c * KB:(c + 1) * KB].astype(BF16)
    qiT_ref[0] = nt(wqiT_ref).astype(BF16)
    kwT = nt(wkwT_ref) * kwscaleT_ref[...]
    kiT_ref[0] = kwT[:D_IDX]
    wT_ref[0] = kwT[D_IDX:D_IDX + H_IDX]
    kwb_ref[0] = kwT.T.astype(BF16)
    lxg = jnp.dot(hn, wl_ref[...], preferred_element_type=F32)
    lx_ref[0] = lxg[:, :LRU_W]
    lg_ref[0] = lxg[:, LRU_W:]
    qm = jnp.dot(hn, wqm_ref[...], preferred_element_type=F32)
    qmb_ref[0] = _group_rms(qm, g128_ref[...], mg_ref[...], DH_MEM).astype(BF16)


def _proj_t(x, p, tm):
    b, s, _ = x.shape
    bc = lambda col: jnp.broadcast_to(col, (col.shape[0], tm))
    consts = [p["g"], p["wqkT"], p["wvT"], p["wqiT"], p["wkwT"], p["wl"], p["wqm"], bc(p["qg_col"]),
              bc(p["kg_col"]), p["mg"], bc(p["kwscale_col"]), p["g64"], p["g128"]]
    tok = lambda w: pl.BlockSpec((1, tm, w), lambda i, j: (i, j, 0))
    feat = lambda w: pl.BlockSpec((1, w, tm), lambda i, j: (i, 0, j))
    outs = [
        (feat(HD), (b, HD, s), F32), (feat(HD), (b, HD, s), F32), (feat(D_IDX), (b, D_IDX, s), F32),
        (feat(H_IDX), (b, H_IDX, s), F32), (feat(HD), (b, HD, s), BF16), (feat(HD), (b, HD, s), BF16),
        (tok(HD), (b, s, HD), BF16), (tok(LANES), (b, s, LANES), BF16),
        (pl.BlockSpec((1, tm // KB, HD, KB), lambda i, j: (i, j, 0, 0)), (b, s // KB, HD, KB), BF16),
        (tok(LRU_W), (b, s, LRU_W), F32), (tok(LRU_W), (b, s, LRU_W), F32), (tok(MD), (b, s, MD), BF16),
    ]
    return pl.pallas_call(
        _proj_t_kernel,
        grid=(b, s // tm),
        in_specs=[tok(D_MODEL)] + [_const_spec(c.shape) for c in consts],
        out_specs=[o[0] for o in outs],
        out_shape=[jax.ShapeDtypeStruct(o[1], o[2]) for o in outs],
        compiler_params=_params(2),
        name="proj_t",
    )(x, *consts)


def _proj(x, p, tm):
    n = x.shape[0]
    consts = [p[k] for k in ("g", "wqkv", "wqi", "wkw", "wl", "wqm", "qg", "kg", "mg", "kwscale", "g64", "g128")]

    def row(w):
        return pl.BlockSpec((tm, w), lambda i: (i, 0))

    outs = [(HD, F32), (HD, F32), (LANES, F32), (HD, BF16), (HD, BF16), (HD, BF16), (HD, BF16), (LANES, BF16),
            (LRU_W, F32), (LRU_W, F32), (MD, BF16)]
    return pl.pallas_call(
        _proj_kernel,
        grid=(n // tm,),
        in_specs=[row(D_MODEL)] + [_const_spec(c.shape) for c in consts],
        out_specs=[row(w) for w, _ in outs],
        out_shape=[jax.ShapeDtypeStruct((n, w), dt) for w, dt in outs],
        compiler_params=_params(1),
        name="proj",
    )(x, *consts)


def _memkv_kernel(m_ref, g_ref, w_ref, kg_ref, g128_ref, mk_ref, mv_ref):
    hn = _rms(m_ref[...], g_ref[...]).astype(BF16)
    kv = jnp.dot(hn, w_ref[...], preferred_element_type=F32)
    mk_ref[...] = _group_rms(kv[:, :MD], g128_ref[...], kg_ref[...], DH_MEM)
    mv_ref[...] = kv[:, MD:]


def _memkv(mem, norm_g, w_mem_kv, mem_k_norm_g, g128):
    n = mem.shape[0]
    tm = min(n, 512)
    consts = [norm_g.reshape(1, D_MODEL), w_mem_kv.astype(BF16), jnp.tile(mem_k_norm_g, H_MEM).reshape(1, MD), g128]
    row = lambda w: pl.BlockSpec((tm, w), lambda i: (i, 0))
    return pl.pallas_call(
        _memkv_kernel,
        grid=(n // tm,),
        in_specs=[row(D_MODEL)] + [_const_spec(c.shape) for c in consts],
        out_specs=[row(MD), row(MD)],
        out_shape=[jax.ShapeDtypeStruct((n, MD), F32)] * 2,
        compiler_params=_params(1),
        name="memkv",
    )(mem, *consts)


def _memattn_kernel(q_ref, mk_ref, mv_ref, o_ref):
    q = q_ref[0]
    mk = mk_ref[0].astype(BF16)
    mv = mv_ref[0].astype(BF16)
    for h in range(H_MEM):
        sl = slice(h * DH_MEM, (h + 1) * DH_MEM)
        s = lax.dot_general(q[:, sl], mk[:, sl], NT_DIMS, preferred_element_type=F32) * (DH_MEM ** -0.5)
        m = jnp.max(s, axis=-1, keepdims=True)
        e = jnp.exp(s - m)
        p = (e / jnp.sum(e, axis=-1, keepdims=True)).astype(BF16)
        o_ref[0, :, sl] = jnp.dot(p, mv[:, sl], preferred_element_type=F32).astype(BF16)


def _memattn(qm, mk, mv, tm):
    b, t, _ = qm.shape
    n_mem = mk.shape[1]
    return pl.pallas_call(
        _memattn_kernel,
        grid=(b, t // tm),
        in_specs=[pl.BlockSpec((1, tm, MD), lambda i, j: (i, j, 0)),
                  pl.BlockSpec((1, n_mem, MD), lambda i, j: (i, 0, 0)),
                  pl.BlockSpec((1, n_mem, MD), lambda i, j: (i, 0, 0))],
        out_specs=pl.BlockSpec((1, tm, MD), lambda i, j: (i, j, 0)),
        out_shape=jax.ShapeDtypeStruct((b, t, MD), BF16),
        compiler_params=_params(2),
        name="memattn",
    )(qm, mk, mv)


def _shift_rows(x, k, fill):
    rows = lax.broadcasted_iota(I32, x.shape, 0)
    return jnp.where(rows >= k, pltpu.roll(x, k, 0), fill)


def _rglru_kernel(lx_ref, lg_ref, cs_ref, h0_ref, cw_ref, cb_ref, wa_ref, ba_ref, wi_ref, bi_ref, lam_ref,
                  y_ref, nb_ref, hl_ref, tail_ref, h_ref):
    t = pl.program_id(1)

    @pl.when(t == 0)
    def _():
        tail_ref[...] = cs_ref[0]
        h_ref[...] = h0_ref[0]

    x = lx_ref[0]
    tt = x.shape[0]
    tail = tail_ref[...]
    rows8 = lax.broadcasted_iota(I32, tail.shape, 0)
    conv = cb_ref[...] + x * cw_ref[CONV_W - 1:CONV_W, :]
    for d in range(1, CONV_W):
        xs = pltpu.roll(x, d, 0)
        head = jnp.where(rows8 < d, pltpu.roll(tail, d, 0), xs[:SUBLANES])
        xs = head if tt == SUBLANES else jnp.concatenate([head, xs[SUBLANES:]], axis=0)
        conv = conv + xs * cw_ref[CONV_W - 1 - d:CONV_W - d, :]
    tail_ref[...] = x[tt - SUBLANES:]
    nb_ref[0] = x[tt - SUBLANES:]

    cb16 = conv.astype(BF16)
    r = jax.nn.sigmoid(jnp.dot(cb16, wa_ref[...], preferred_element_type=F32) + ba_ref[...])
    ig = jax.nn.sigmoid(jnp.dot(cb16, wi_ref[...], preferred_element_type=F32) + bi_ref[...])
    nl = -lam_ref[...]
    softplus = jnp.maximum(nl, 0.0) + jnp.log1p(jnp.exp(-jnp.abs(nl)))
    log_a = -LRU_C * r * softplus
    a = jnp.exp(log_a)
    b = jnp.sqrt(-jnp.tanh(log_a) * (jnp.exp(2.0 * log_a) + 1.0)) * (ig * conv)
    rows = lax.broadcasted_iota(I32, x.shape, 0)
    b = b + jnp.where(rows == 0, a * h_ref[SUBLANES - 1:SUBLANES, :], 0.0)
    k = 1
    while k < tt:
        b = a * _shift_rows(b, k, 0.0) + b
        a = a * _shift_rows(a, k, 1.0)
        k *= 2
    h = b
    h_ref[...] = h[tt - SUBLANES:]
    hl_ref[0] = h[tt - SUBLANES:]
    y_ref[0] = (h * jax.nn.gelu(lg_ref[0])).astype(BF16)


def _block_diag(w):
    nb, bs, _ = w.shape
    eye = jnp.eye(nb, dtype=w.dtype)
    return (eye[:, None, :, None] * w[:, :, None, :]).reshape(nb * bs, nb * bs)


def _prep_rglru(conv_w, conv_b, lru_wa, lru_ba, lru_wi, lru_bi, lru_lambda):
    r = lambda v: v.reshape(1, LRU_W)
    return [conv_w, r(conv_b), _block_diag(lru_wa).astype(BF16), r(lru_ba), _block_diag(lru_wi).astype(BF16),
            r(lru_bi), r(lru_lambda)]


def _rglru(lx, lg, conv_state, h0, consts, tt):
    b, t, w = lx.shape
    cs = jnp.concatenate([jnp.zeros((b, SUBLANES - (CONV_W - 1), w), F32), conv_state], axis=1)
    h0p = jnp.concatenate([jnp.zeros((b, SUBLANES - 1, w), F32), h0[:, None, :]], axis=1)
    seq = pl.BlockSpec((1, tt, w), lambda i, j: (i, j, 0))
    st = pl.BlockSpec((1, SUBLANES, w), lambda i, j: (i, 0, 0))
    y, nb, hl = pl.pallas_call(
        _rglru_kernel,
        grid=(b, t // tt),
        in_specs=[seq, seq, st, st] + [_const_spec(c.shape) for c in consts],
        out_specs=[seq, st, st],
        out_shape=[jax.ShapeDtypeStruct((b, t, w), BF16), jax.ShapeDtypeStruct((b, SUBLANES, w), F32),
                   jax.ShapeDtypeStruct((b, SUBLANES, w), F32)],
        scratch_shapes=[pltpu.VMEM((SUBLANES, w), F32), pltpu.VMEM((SUBLANES, w), F32)],
        compiler_params=pltpu.CompilerParams(dimension_semantics=("parallel", "arbitrary"),
                                             vmem_limit_bytes=VMEM_LIMIT),
        name="rglru",
    )(lx, lg, cs, h0p, *consts)
    return y, nb[:, SUBLANES - (CONV_W - 1):], hl[:, SUBLANES - 1]


def _score_key(score):
    bits = lax.bitcast_convert_type(score, I32)
    return bits ^ ((bits >> 31) & 0x7FFFFFFF)


def _radix_threshold(count_ge, rows, topk):
    def bit_step(i, thr_u):
        cand_u = thr_u | (jnp.int32(1) << (31 - i))
        cnt = count_ge(cand_u ^ INT_MIN)
        return jnp.where(cnt >= topk, cand_u, thr_u)

    thr_u = lax.fori_loop(0, 32, bit_step, jnp.zeros((rows, 1), I32))
    return jnp.maximum(thr_u ^ INT_MIN, KEY_NEG_INF + 1)


def _dsa_prompt_kernel(qT_ref, qiT_ref, wT_ref, kw_ref, k_ref, vT_ref, tri_ref, o_ref,
                       sc_ref, qpad_ref, qipad_ref, oT_ref, s_scr, s2_scr, p_scr, *, topk):
    i = pl.program_id(1)
    nk = i + 1
    kf = float(topk)
    kblock = lambda j: pl.ds(pl.multiple_of(j * KB, KB), KB)

    zeros64 = jnp.zeros((D_IDX, TQ), BF16)
    for h in range(H_IDX):
        qipad_ref[h] = jnp.concatenate([qiT_ref[0, h * D_IDX:(h + 1) * D_IDX, :], zeros64], axis=0)
    for h in range(H_ATT):
        qh = qT_ref[0, h * DH_ATT:(h + 1) * DH_ATT, :]
        qpad_ref[h] = jnp.concatenate([qh, zeros64] if h % 2 == 0 else [zeros64, qh], axis=0)
    wT = wT_ref[0]

    def block_scores(j):
        kw = kw_ref[0, kblock(j), :]
        acc = jnp.zeros((KB, TQ), F32)
        for h in range(H_IDX):
            d = jnp.dot(kw, qipad_ref[h], preferred_element_type=F32)
            acc = acc + jnp.maximum(d, 0.0) * wT[h:h + 1, :]
        return acc

    def score_block(j, carry):
        lo, hi = carry
        acc = block_scores(j)
        sc_ref[kblock(j), :] = acc
        return (jnp.minimum(lo, jnp.min(acc, axis=0, keepdims=True)),
                jnp.maximum(hi, jnp.max(acc, axis=0, keepdims=True)))

    init = (jnp.full((1, TQ), jnp.inf, F32), jnp.full((1, TQ), NEG_INF, F32))
    rmin, rmax = lax.fori_loop(0, i, score_block, init)
    acc = block_scores(i)
    krow = lax.broadcasted_iota(I32, (KB, TQ), 0)
    qcol = lax.broadcasted_iota(I32, (KB, TQ), 1)
    sc_ref[kblock(i), :] = jnp.where(krow <= qcol, acc, NEG_INF)
    rmin = jnp.minimum(rmin, jnp.min(acc, axis=0, keepdims=True))
    rmax = jnp.maximum(rmax, jnp.max(acc, axis=0, keepdims=True))

    def count(pred):
        def body(j, cnt):
            m = jnp.where(pred(sc_ref[kblock(j), :]), 1.0, 0.0)
            for r in range(KB // CNT_ROWS):
                cnt = cnt + m[r * CNT_ROWS:(r + 1) * CNT_ROWS]
            return cnt

        cnt = lax.fori_loop(0, nk, body, jnp.zeros((CNT_ROWS, TQ), F32))
        return jnp.sum(cnt, axis=0, keepdims=True)

    def n_open(done):
        return jnp.sum(1.0 - done).astype(I32)

    done0 = jnp.where(count(lambda x: x >= rmin) <= kf, 1.0, 0.0)

    def bisect(_, c):
        lo, hi, done = c
        mid = 0.5 * lo + 0.5 * jnp.minimum(hi, rmax)
        c_mid = count(lambda x: x >= mid)
        ge = c_mid >= kf
        live = done < 0.5
        lo = jnp.where(live & ge, mid, lo)
        hi = jnp.where(live & jnp.logical_not(ge), mid, hi)
        done = jnp.maximum(done, jnp.where(c_mid == kf, 1.0, 0.0))
        return lo, hi, done

    lo, hi, done = lax.fori_loop(0, BISECT_STEPS, bisect, (rmin, jnp.full((1, TQ), jnp.inf, F32), done0))
    left = n_open(done)

    def step_down(c):
        lo, hi, done, _ = c

        def body(j, best):
            x = sc_ref[kblock(j), :]
            return jnp.maximum(best, jnp.max(jnp.where(x < hi, x, NEG_INF), axis=0, keepdims=True))

        cand = lax.fori_loop(0, nk, body, jnp.full((1, TQ), NEG_INF, F32))
        ok = count(lambda x: x >= cand) >= kf
        live = done < 0.5
        lo = jnp.where(live & ok, cand, lo)
        hi = jnp.where(live & jnp.logical_not(ok), cand, hi)
        done = jnp.maximum(done, jnp.where(ok, 1.0, 0.0))
        return lo, hi, done, n_open(done)

    thr, _, _, _ = lax.while_loop(lambda c: c[3] > 0, step_down, (lo, hi, done, left))
    n_tied = jnp.sum(jnp.where(count(lambda x: x >= thr) > kf, 1.0, 0.0)).astype(I32)

    @pl.when(n_tied == 0)
    def _():
        def body(j, carry):
            x = sc_ref[kblock(j), :]
            sc_ref[kblock(j), :] = jnp.where(x >= thr, 0.0, NEG_INF)
            return carry

        lax.fori_loop(0, nk, body, 0)

    @pl.when(n_tied > 0)
    def _():
        quota = kf - count(lambda x: x > thr)

        def body(j, ties_before):
            x = sc_ref[kblock(j), :]
            eq = jnp.where(x == thr, 1.0, 0.0)
            rank = jnp.dot(tri_ref[...], eq.astype(BF16), preferred_element_type=F32) + ties_before
            sel = (x > thr) | ((x == thr) & (rank < quota))
            sc_ref[kblock(j), :] = jnp.where(sel, 0.0, NEG_INF)
            return ties_before + jnp.sum(eq, axis=0, keepdims=True)

        lax.fori_loop(0, nk, body, jnp.zeros((1, TQ), F32))

    n_pairs = (nk + 1) // 2

    @pl.when(nk % 2 == 1)
    def _():
        sc_ref[kblock(nk), :] = jnp.full((KB, TQ), NEG_INF, F32)

    def score_phase(j, s_out):
        block_max = []
        for h in range(H_ATT):
            pair = slice((h // 2) * LANES, (h // 2 + 1) * LANES)
            s = jnp.dot(k_ref[0, kblock(j), pair], qpad_ref[h], preferred_element_type=F32)
            s = s + sc_ref[kblock(j), :]
            s_out[h] = s
            block_max.append(jnp.max(s, axis=0, keepdims=True))
        return tuple(block_max)

    def block_step(j, next_j, s_in, s_out, carry):
        ms, ls, accs, block_max = carry
        next_max = score_phase(next_j, s_out)
        new_ms, new_ls, new_accs, alphas = [], [], [], []
        for h in range(H_ATT):
            m_new = jnp.maximum(ms[h], block_max[h])
            m_safe = jnp.where(m_new == NEG_INF, 0.0, m_new)
            alphas.append(jnp.exp2(ms[h] - m_safe))
            p = jnp.exp2(s_in[h] - m_safe)
            new_ms.append(m_new)
            new_ls.append(alphas[h] * ls[h] + jnp.sum(p, axis=0, keepdims=True))
            p_scr[h] = p.astype(BF16)
        for h in range(H_ATT):
            hrows = slice(h * DH_ATT, (h + 1) * DH_ATT)
            pv = jnp.dot(vT_ref[0, j, hrows, :], p_scr[h], preferred_element_type=F32)
            new_accs.append(alphas[h] * accs[h] + pv)
        return tuple(new_ms), tuple(new_ls), tuple(new_accs), next_max

    def attend_pair(jj, carry):
        j0 = 2 * jj
        carry = block_step(j0, j0 + 1, s_scr, s2_scr, carry)
        return block_step(j0 + 1, jnp.minimum(j0 + 2, 2 * n_pairs - 1), s2_scr, s_scr, carry)

    init = (tuple(jnp.full((1, TQ), NEG_INF, F32) for _ in range(H_ATT)),
            tuple(jnp.zeros((1, TQ), F32) for _ in range(H_ATT)),
            tuple(jnp.zeros((DH_ATT, TQ), F32) for _ in range(H_ATT)),
            score_phase(0, s_scr))
    _, ls, accs, _ = lax.fori_loop(0, n_pairs, attend_pair, init)
    for h in range(H_ATT):
        oT_ref[h * DH_ATT:(h + 1) * DH_ATT, :] = accs[h] / ls[h]
    o_ref[0] = oT_ref[...].T.astype(BF16)


def _dsa_prompt(qT, qiT, wT, kwb, kb, vTb):
    b, _, s = qT.shape
    assert s % (2 * KB) == 0 and TQ == KB
    topk = min(TOPK_MAX, s // 4)
    tri = (jnp.arange(KB)[:, None] > jnp.arange(KB)[None, :]).astype(BF16)
    feat = lambda w: pl.BlockSpec((1, w, TQ), lambda i, j: (i, 0, j))
    full = lambda w: pl.BlockSpec((1, s, w), lambda i, j: (i, 0, 0))
    return pl.pallas_call(
        functools.partial(_dsa_prompt_kernel, topk=topk),
        grid=(b, s // TQ),
        in_specs=[feat(HD), feat(HD), feat(H_IDX), full(LANES), full(HD),
                  pl.BlockSpec((1, s // KB, HD, KB), lambda i, j: (i, 0, 0, 0)), _const_spec(tri.shape)],
        out_specs=pl.BlockSpec((1, TQ, HD), lambda i, j: (i, j, 0)),
        out_shape=jax.ShapeDtypeStruct((b, s, HD), BF16),
        scratch_shapes=[pltpu.VMEM((s, TQ), F32), pltpu.VMEM((H_ATT, LANES, TQ), BF16),
                        pltpu.VMEM((H_IDX, LANES, TQ), BF16), pltpu.VMEM((HD, TQ), F32),
                        pltpu.VMEM((H_ATT, KB, TQ), F32), pltpu.VMEM((H_ATT, KB, TQ), F32),
                        pltpu.VMEM((H_ATT, KB, TQ), BF16)],
        compiler_params=_params(2),
        name="dsa_prompt",
    )(qT, qiT, wT, kwb, kb, vTb, tri)


PAGE_GROUP_IDX = 16
PAGE_GROUP_KV = 8


def _dsa_sample_select_kernel(pt_ref, qi_ref, kwq_ref, tri_ref, *rest, n_pages, topk, group):
    page_refs, (bias_ref, biasn_ref, qiall_ref, wb_ref) = rest[:group], rest[group:]
    pg = pl.program_id(1)
    t = qi_ref.shape[1]
    kwq = kwq_ref[0]

    @pl.when(pg == 0)
    def _():
        qi = qi_ref[0].astype(F32)
        qiall_ref[...] = jnp.concatenate(
            [qi[:, h * D_IDX:(h + 1) * D_IDX] for h in range(H_IDX)], axis=0).astype(BF16)
        for h in range(H_IDX):
            wb_ref[h] = jnp.broadcast_to(kwq[:, D_IDX + h:D_IDX + h + 1], (t, LANES))

    def scores(dots):
        acc = jnp.zeros((t, dots.shape[1]), F32)
        for h in range(H_IDX):
            acc = acc + jnp.maximum(dots[h * t:(h + 1) * t], 0.0) * wb_ref[h]
        return acc

    for g in range(group):
        dots = jnp.dot(qiall_ref[...], page_refs[g][0].astype(BF16), preferred_element_type=F32)
        bias_ref[0, pg * group + g] = _score_key(scores(dots))

    @pl.when(pg == pl.num_programs(1) - 1)
    def _():
        new_keys = jnp.concatenate([kwq[:, :D_IDX], jnp.zeros((LANES - t, D_IDX), F32)], axis=0).astype(BF16)
        row = lax.broadcasted_iota(I32, (t, LANES), 0)
        col = lax.broadcasted_iota(I32, (t, LANES), 1)
        dots_new = lax.dot_general(qiall_ref[...], new_keys, NT_DIMS, preferred_element_type=F32)
        key_new = _score_key(jnp.where(col <= row, scores(dots_new), NEG_INF))

        def count(cmp_fn):
            cnt = jnp.sum(cmp_fn(bias_ref[0]).astype(I32), axis=0) + cmp_fn(key_new).astype(I32)
            return jnp.sum(cnt, axis=1, keepdims=True)

        thr = _radix_threshold(lambda c: count(lambda key: key >= c), t, topk)
        n_inexact = jnp.sum(jnp.where(count(lambda key: key >= thr) == topk, 0, 1))
        as_bias = lambda sel: lax.bitcast_convert_type(jnp.where(sel, 0.0, NEG_INF), I32)

        @pl.when(n_inexact == 0)
        def _():
            bias_ref[0] = as_bias(bias_ref[0] >= thr)
            biasn_ref[0] = as_bias(key_new >= thr)

        @pl.when(n_inexact > 0)
        def _():
            quota = (topk - count(lambda key: key > thr)).astype(F32)

            def select(key, ties_before):
                eq = key == thr
                eqf = eq.astype(F32)
                rank = jnp.dot(eqf.astype(BF16), tri_ref[...], preferred_element_type=F32) + ties_before
                sel = (key > thr) | (eq & (rank < quota))
                return as_bias(sel), ties_before + jnp.sum(eqf, axis=1, keepdims=True)

            def bias_page(p, ties_before):
                bias, ties = select(bias_ref[0, p], ties_before)
                bias_ref[0, p] = bias
                return ties

            ties = lax.fori_loop(0, n_pages, bias_page, jnp.zeros((t, 1), F32))
            biasn_ref[0], _ = select(key_new, ties)


def _dsa_sample_select(page_table, qib, kw32, cache_kidx_t):
    b, t, _ = qib.shape
    n_pages = page_table.shape[1]
    page = cache_kidx_t.shape[2]
    topk = min(TOPK_MAX, (n_pages * page + t) // 4)
    group = PAGE_GROUP_IDX
    tri = (jnp.arange(page)[:, None] < jnp.arange(page)[None, :]).astype(BF16)
    tok = lambda w: pl.BlockSpec((1, t, w), lambda i, j, pt: (i, 0, 0))
    page_specs = [pl.BlockSpec((1, D_IDX, page), lambda i, j, pt, g=g: (pt[i, j * group + g], 0, 0))
                  for g in range(group)]
    grid_spec = pltpu.PrefetchScalarGridSpec(
        num_scalar_prefetch=1,
        grid=(b, n_pages // group),
        in_specs=[tok(H_IDX * D_IDX), tok(LANES), pl.BlockSpec(tri.shape, lambda i, j, pt: (0, 0))] + page_specs,
        out_specs=[pl.BlockSpec((1, n_pages, t, page), lambda i, j, pt: (i, 0, 0, 0)),
                   pl.BlockSpec((1, t, LANES), lambda i, j, pt: (i, 0, 0))],
        scratch_shapes=[pltpu.VMEM((H_IDX * t, D_IDX), BF16), pltpu.VMEM((H_IDX, t, LANES), F32)],
    )
    return pl.pallas_call(
        functools.partial(_dsa_sample_select_kernel, n_pages=n_pages, topk=topk, group=group),
        grid_spec=grid_spec,
        out_shape=[jax.ShapeDtypeStruct((b, n_pages, t, page), I32), jax.ShapeDtypeStruct((b, t, LANES), I32)],
        compiler_params=pltpu.CompilerParams(dimension_semantics=("parallel", "arbitrary"),
                                             vmem_limit_bytes=VMEM_LIMIT),
        name="dsa_sample_select",
    )(page_table, qib, kw32, tri, *([cache_kidx_t] * group))


def _dsa_sample_attend_kernel(pt_ref, q_ref, kn_ref, vn_ref, bias_ref, biasn_ref, *rest, group):
    kT_refs, vT_refs = rest[:group], rest[group:2 * group]
    o_ref, m_ref, l_ref, acc_ref = rest[2 * group:]
    pg = pl.program_id(1)
    t = q_ref.shape[1]
    page = kT_refs[0].shape[2]
    lane_head = lax.broadcasted_iota(I32, (t, HD), 1) // DH_ATT
    q = q_ref[0].astype(F32)
    qbd = jnp.concatenate([jnp.where(lane_head == h, q, 0.0) for h in range(H_ATT)], axis=0).astype(BF16)

    @pl.when(pg == 0)
    def _():
        m_ref[...] = jnp.full(m_ref.shape, NEG_INF, F32)
        l_ref[...] = jnp.zeros(l_ref.shape, F32)
        acc_ref[...] = jnp.zeros(acc_ref.shape, F32)

    def update(s, bias, pv):
        s = s + jnp.concatenate([lax.bitcast_convert_type(bias, F32)] * H_ATT, axis=0)
        m = m_ref[...]
        m_new = jnp.maximum(m, jnp.max(s, axis=1, keepdims=True))
        m_safe = jnp.where(m_new == NEG_INF, 0.0, m_new)
        alpha = jnp.exp(m - m_safe)
        p = jnp.exp(s - m_safe)
        l_ref[...] = alpha * l_ref[...] + jnp.sum(p, axis=1, keepdims=True)
        acc_ref[...] = alpha * acc_ref[...] + pv(p.astype(BF16))
        m_ref[...] = m_new

    s_pages = jnp.concatenate(
        [jnp.dot(qbd, kT_refs[g][0].astype(BF16), preferred_element_type=F32) for g in range(group)], axis=1)
    bias_pages = jnp.concatenate([bias_ref[0, g] for g in range(group)], axis=1)

    def pv_pages(p):
        out = jnp.zeros((H_ATT * t, HD), F32)
        for g in range(group):
            out = out + lax.dot_general(p[:, g * page:(g + 1) * page], vT_refs[g][0].astype(BF16), NT_DIMS,
                                        preferred_element_type=F32)
        return out

    update(s_pages, bias_pages, pv_pages)

    @pl.when(pg == pl.num_programs(1) - 1)
    def _():
        pad = jnp.zeros((LANES - t, HD), F32)
        kn = jnp.concatenate([kn_ref[0], pad], axis=0).astype(BF16)
        vn = jnp.concatenate([vn_ref[0], pad], axis=0).astype(BF16)
        update(lax.dot_general(qbd, kn, NT_DIMS, preferred_element_type=F32), biasn_ref[0],
               lambda p: jnp.dot(p, vn, preferred_element_type=F32))
        o = acc_ref[...] / l_ref[...]
        out = jnp.zeros((t, HD), F32)
        for h in range(H_ATT):
            out = out + jnp.where(lane_head == h, o[h * t:(h + 1) * t], 0.0)
        o_ref[0] = out.astype(BF16)


def _dsa_sample_attend(page_table, qb, k32, v32, bias, bias_new, cache_kt, cache_vt):
    b, t, _ = qb.shape
    n_pages = page_table.shape[1]
    page = cache_kt.shape[2]
    group = PAGE_GROUP_KV
    tok = lambda w: pl.BlockSpec((1, t, w), lambda i, j, pt: (i, 0, 0))
    kv_specs = [pl.BlockSpec((1, HD, page), lambda i, j, pt, g=g: (pt[i, j * group + g], 0, 0))
                for g in range(group)]
    grid_spec = pltpu.PrefetchScalarGridSpec(
        num_scalar_prefetch=1,
        grid=(b, n_pages // group),
        in_specs=[tok(HD), tok(HD), tok(HD),
                  pl.BlockSpec((1, group, t, page), lambda i, j, pt: (i, j, 0, 0)),
                  tok(LANES)] + kv_specs + kv_specs,
        out_specs=tok(HD),
        scratch_shapes=[pltpu.VMEM((H_ATT * t, 1), F32), pltpu.VMEM((H_ATT * t, 1), F32),
                        pltpu.VMEM((H_ATT * t, HD), F32)],
    )
    return pl.pallas_call(
        functools.partial(_dsa_sample_attend_kernel, group=group),
        grid_spec=grid_spec,
        out_shape=jax.ShapeDtypeStruct((b, t, HD), BF16),
        compiler_params=pltpu.CompilerParams(dimension_semantics=("parallel", "arbitrary"),
                                             vmem_limit_bytes=VMEM_LIMIT),
        name="dsa_sample_attend",
    )(page_table, qb, k32, v32, bias, bias_new, *([cache_kt] * group), *([cache_vt] * group))


def _merge_kernel(x_ref, oa_ref, ol_ref, om_ref, g_ref, wg_ref, wa_ref, wl_ref, wm_ref, wo_ref, o_ref):
    x = x_ref[...]
    hn = _rms(x, g_ref[...]).astype(BF16)
    m = jnp.zeros_like(x)
    for idx, (o_r, w_r) in enumerate(((oa_ref, wa_ref), (ol_ref, wl_ref), (om_ref, wm_ref))):
        gate = jax.nn.sigmoid(jnp.dot(hn, wg_ref[:, idx * D_MODEL:(idx + 1) * D_MODEL], preferred_element_type=F32))
        m = m + gate * jnp.dot(o_r[...], w_r[...], preferred_element_type=F32)
    o_ref[...] = x + jnp.dot(m.astype(BF16), wo_ref[...], preferred_element_type=F32)


def _merge(x, o_att, o_lru, o_mem, consts, tm):
    n = x.shape[0]
    row = lambda w: pl.BlockSpec((tm, w), lambda i: (i, 0))
    return pl.pallas_call(
        _merge_kernel,
        grid=(n // tm,),
        in_specs=[row(D_MODEL), row(HD), row(LRU_W), row(MD)] + [_const_spec(c.shape) for c in consts],
        out_specs=row(D_MODEL),
        out_shape=jax.ShapeDtypeStruct((n, D_MODEL), F32),
        compiler_params=_params(1),
        name="merge",
    )(x, o_att, o_lru, o_mem, *consts)


def _token_tile(n):
    return min(n, 512)


def _layer(x, is_prompt, lw, pp, extra):
    b, t, _ = x.shape
    n = b * t
    tm = _token_tile(n)
    x1 = _ffn(x.reshape(n, D_MODEL), pp["ffn1"], tm)
    r3 = lambda a: a.reshape(b, t, a.shape[-1])
    if is_prompt:
        kT32, vT32, kiT32, wT, qT, qiT, kb, kwb, vTb, lx, lg, qmb = _proj_t(r3(x1), pp["proj"], tm)
        o_att = _dsa_prompt(qT, qiT, wT, kwb, kb, vTb)
        k_new = kT32.reshape(b, H_ATT, DH_ATT, t).transpose(0, 3, 1, 2)
        v_new = vT32.reshape(b, H_ATT, DH_ATT, t).transpose(0, 3, 1, 2)
        ki_new = kiT32.transpose(0, 2, 1)
        conv_state = jnp.zeros((b, CONV_W - 1, LRU_W), F32)
        h0 = jnp.zeros((b, LRU_W), F32)
        mem = extra["mem"]
        mk, mv = _memkv(mem.reshape(-1, D_MODEL), lw["norm_mem_g"], lw["w_mem_kv"], lw["mem_k_norm_g"],
                        pp["proj"]["g128"])
        mk = mk.reshape(b, -1, MD)
        mv = mv.reshape(b, -1, MD)
    else:
        k32, v32, kw32, qb, _, _, qib, _, lx, lg, qmb = _proj(x1, pp["proj"], tm)
        pt = extra["page_table"]
        bias, bias_new = _dsa_sample_select(pt, r3(qib), r3(kw32), extra["cache_kidx_t"])
        o_att = _dsa_sample_attend(pt, r3(qb), r3(k32), r3(v32), bias, bias_new, extra["cache_kt"],
                                   extra["cache_vt"])
        k_new = k32.reshape(b, t, H_ATT, DH_ATT)
        v_new = v32.reshape(b, t, H_ATT, DH_ATT)
        ki_new = r3(kw32)[:, :, :D_IDX]
        conv_state, h0 = extra["state_conv"], extra["state_h"]
        mk, mv = extra["cache_mem_k"], extra["cache_mem_v"]
    o_lru, conv_buf, h_last = _rglru(r3(lx), r3(lg), conv_state, h0, pp["rglru"], min(t, 256))
    o_mem = _memattn(r3(qmb), mk, mv, min(t, 512))
    x2 = _merge(x1, o_att.reshape(n, HD), o_lru.reshape(n, LRU_W), o_mem.reshape(n, MD), pp["merge"], tm)
    y = _ffn(x2, pp["ffn2"], tm).reshape(b, t, D_MODEL)
    if is_prompt:
        state = (k_new, v_new, ki_new, mk.reshape(b, -1, H_MEM, DH_MEM), mv.reshape(b, -1, H_MEM, DH_MEM),
                 conv_buf, h_last)
    else:
        state = (k_new, v_new, ki_new, conv_buf, h_last)
    return y, state


def kernel(x_prompt, x_sample, cache_k, cache_v, cache_kidx, page_table, cache_mem_k, cache_mem_v, state_conv, state_h, mem_prompt, norm_ffn1_g, w_ffn1_in, w_ffn1_out, norm_mix_g, w_in, q_norm_g, k_norm_g, w_attn_o, conv_w, conv_b, lru_wa, lru_ba, lru_wi, lru_bi, lru_lambda, w_lru_o, norm_mem_g, w_mem_kv, mem_q_norm_g, mem_k_norm_g, w_mem_o, w_out, norm_ffn2_g, w_ffn2_in, w_ffn2_out):
    depth = w_in.shape[0]
    n_phys, page = cache_k.shape[1], cache_k.shape[2]
    xp, xs = x_prompt, x_sample
    p_states, s_states = [], []
    for l in range(depth):
        proj = _prep_proj(norm_mix_g[l], w_in[l], q_norm_g[l], k_norm_g[l], mem_q_norm_g[l])
        pp = dict(
            ffn1=_prep_ffn(norm_ffn1_g[l], w_ffn1_in[l], w_ffn1_out[l]),
            ffn2=_prep_ffn(norm_ffn2_g[l], w_ffn2_in[l], w_ffn2_out[l]),
            proj=proj,
            rglru=_prep_rglru(conv_w[l], conv_b[l], lru_wa[l], lru_ba[l], lru_wi[l], lru_bi[l], lru_lambda[l]),
            merge=[proj["g"], proj["wgates"], w_attn_o[l].astype(BF16), w_lru_o[l].astype(BF16),
                   w_mem_o[l].astype(BF16), w_out[l].astype(BF16)],
        )
        lw = dict(norm_mem_g=norm_mem_g[l], w_mem_kv=w_mem_kv[l], mem_k_norm_g=mem_k_norm_g[l])
        xp, st_p = _layer(xp, True, lw, pp, dict(mem=mem_prompt))
        xs, st_s = _layer(xs, False, lw, pp, dict(
            page_table=page_table,
            cache_kt=cache_k[l].transpose(0, 2, 3, 1).reshape(n_phys, HD, page),
            cache_vt=cache_v[l].transpose(0, 2, 3, 1).reshape(n_phys, HD, page),
            cache_kidx_t=cache_kidx[l].transpose(0, 2, 1), cache_mem_k=cache_mem_k[l].reshape(-1, cache_mem_k.shape[2], MD),
            cache_mem_v=cache_mem_v[l].reshape(-1, cache_mem_v.shape[2], MD),
            state_conv=state_conv[l], state_h=state_h[l]))
        p_states.append(st_p)
        s_states.append(st_s)
    stack = lambda states, i: jnp.stack([s[i] for s in states])
    return (xp, xs) + tuple(stack(p_states, i) for i in range(7)) + tuple(stack(s_states, i) for i in range(5))
```

```python
import functools

import jax
import jax.numpy as jnp
from jax import lax
from jax.experimental import pallas as pl
from jax.experimental.pallas import tpu as pltpu

F32, BF16, I32 = jnp.float32, jnp.bfloat16, jnp.int32

D_MODEL = 1024
H_ATT, DH_ATT = 8, 64
H_IDX, D_IDX = 8, 64
TOPK_MAX = 256
LRU_W, LRU_BLOCKS, CONV_W, LRU_C = 512, 8, 4, 8.0
H_MEM, DH_MEM = 4, 128
D_FF = 2816
EPS = 1e-6
HD = H_ATT * DH_ATT
MD = H_MEM * DH_MEM

LANES = 128
SUBLANES = 8
VMEM_BYTES_V7X = 64 * 1024 * 1024
VMEM_LIMIT = VMEM_BYTES_V7X - 8 * 1024 * 1024

FF_CHUNK = 256
TQ = 256
KB = 256
CNT_ROWS = 32
SUM_ROWS = 16
BISECT_STEPS = 15
LOG2E = 1.4426950408889634
INT_MIN = -2 ** 31
KEY_NEG_INF = -2139095041
NEG_INF = float("-inf")

NT_DIMS = (((1,), (1,)), ((), ()))


def _params(n_grid, parallel=True):
    sem = ("parallel" if parallel else "arbitrary",) * n_grid
    return pltpu.CompilerParams(dimension_semantics=sem, vmem_limit_bytes=VMEM_LIMIT)


def _const_spec(shape):
    nd = len(shape)
    return pl.BlockSpec(shape, lambda *_: (0,) * nd)


def _rms(x, g):
    ms = jnp.mean(x * x, axis=-1, keepdims=True)
    return x * lax.rsqrt(ms + EPS) * g


def _group_rms(x, gmat, g, group):
    x2 = x * x
    hi = x2.astype(BF16)
    lo = (x2 - hi.astype(F32)).astype(BF16)
    ss = jnp.dot(hi, gmat, preferred_element_type=F32) + jnp.dot(lo, gmat, preferred_element_type=F32)
    return x * lax.rsqrt(ss * (1.0 / group) + EPS) * g


def _group_matrix(width, group):
    idx = jnp.arange(width) // group
    return (idx[:, None] == idx[None, :]).astype(BF16)


def _ffn_kernel(x_ref, g_ref, wg_ref, wu_ref, wo_ref, o_ref):
    x = x_ref[...]
    hn = _rms(x, g_ref[...]).astype(BF16)
    acc = jnp.zeros_like(x)
    for c in range(wg_ref.shape[0]):
        gate = jnp.dot(hn, wg_ref[c], preferred_element_type=F32)
        up = jnp.dot(hn, wu_ref[c], preferred_element_type=F32)
        act = (gate * jax.nn.sigmoid(gate) * up).astype(BF16)
        acc = acc + jnp.dot(act, wo_ref[c], preferred_element_type=F32)
    o_ref[...] = x + 0.5 * acc


def _prep_ffn(g, w_in, w_out):
    nc = D_FF // FF_CHUNK
    wg = w_in[:, :D_FF].reshape(D_MODEL, nc, FF_CHUNK).transpose(1, 0, 2).astype(BF16)
    wu = w_in[:, D_FF:].reshape(D_MODEL, nc, FF_CHUNK).transpose(1, 0, 2).astype(BF16)
    wo = w_out.reshape(nc, FF_CHUNK, D_MODEL).astype(BF16)
    return g.reshape(1, D_MODEL), wg, wu, wo


def _ffn(x, prep, tm):
    g, wg, wu, wo = prep
    n = x.shape[0]
    row = pl.BlockSpec((tm, D_MODEL), lambda i: (i, 0))
    return pl.pallas_call(
        _ffn_kernel,
        grid=(n // tm,),
        in_specs=[row, _const_spec(g.shape), _const_spec(wg.shape), _const_spec(wu.shape), _const_spec(wo.shape)],
        out_specs=row,
        out_shape=jax.ShapeDtypeStruct((n, D_MODEL), F32),
        compiler_params=_params(1),
        name="ffn",
    )(x, g, wg, wu, wo)


def _proj_kernel(x_ref, g_ref, wqkv_ref, wqi_ref, wkw_ref, wl_ref, wqm_ref, qg_ref, kg_ref, mg_ref,
                 kwscale_ref, g64_ref, g128_ref,
                 k32_ref, v32_ref, kw32_ref, qb_ref, kb_ref, vb_ref, qib_ref, kwb_ref, lx_ref, lg_ref, qmb_ref):
    hn = _rms(x_ref[...], g_ref[...]).astype(BF16)
    qkv = jnp.dot(hn, wqkv_ref[...], preferred_element_type=F32)
    q = _group_rms(qkv[:, :HD], g64_ref[...], qg_ref[...], DH_ATT)
    k = _group_rms(qkv[:, HD:2 * HD], g64_ref[...], kg_ref[...], DH_ATT)
    v = qkv[:, 2 * HD:]
    k32_ref[...] = k
    v32_ref[...] = v
    qb_ref[...] = (q * (DH_ATT ** -0.5)).astype(BF16)
    kb_ref[...] = k.astype(BF16)
    vb_ref[...] = v.astype(BF16)
    qib_ref[...] = jnp.dot(hn, wqi_ref[...], preferred_element_type=F32).astype(BF16)
    kw = jnp.dot(hn, wkw_ref[...], preferred_element_type=F32) * kwscale_ref[...]
    kw32_ref[...] = kw
    kwb_ref[...] = kw.astype(BF16)
    lxg = jnp.dot(hn, wl_ref[...], preferred_element_type=F32)
    lx_ref[...] = lxg[:, :LRU_W]
    lg_ref[...] = lxg[:, LRU_W:]
    qm = jnp.dot(hn, wqm_ref[...], preferred_element_type=F32)
    qmb_ref[...] = _group_rms(qm, g128_ref[...], mg_ref[...], DH_MEM).astype(BF16)


def _prep_proj(norm_g, w_in, q_norm_g, k_norm_g, mem_q_norm_g):
    o = 0
    cols = {}
    for name, size in (("q", HD), ("k", HD), ("v", HD), ("qi", H_IDX * D_IDX), ("ki", D_IDX), ("wi", H_IDX),
                       ("lx", LRU_W), ("lg", LRU_W), ("qm", MD), ("gates", 3 * D_MODEL)):
        cols[name] = w_in[:, o:o + size]
        o += size
    pad = jnp.zeros((D_MODEL, LANES - D_IDX - H_IDX), w_in.dtype)
    wqkv = jnp.concatenate([cols["q"], cols["k"], cols["v"]], axis=1).astype(BF16)
    wkw = jnp.concatenate([cols["ki"], cols["wi"], pad], axis=1).astype(BF16)
    wl = jnp.concatenate([cols["lx"], cols["lg"]], axis=1).astype(BF16)
    kwscale = jnp.concatenate([jnp.ones((D_IDX,), F32),
                               jnp.full((H_IDX,), H_IDX ** -0.5 * D_IDX ** -0.5, F32),
                               jnp.zeros((LANES - D_IDX - H_IDX,), F32)]).reshape(1, LANES)
    return dict(
        g=norm_g.reshape(1, D_MODEL), wqkv=wqkv, wqi=cols["qi"].astype(BF16), wkw=wkw, wl=wl,
        wqm=cols["qm"].astype(BF16),
        qg=jnp.tile(q_norm_g, H_ATT).reshape(1, HD), kg=jnp.tile(k_norm_g, H_ATT).reshape(1, HD),
        mg=jnp.tile(mem_q_norm_g, H_MEM).reshape(1, MD), kwscale=kwscale,
        g64=_group_matrix(HD, DH_ATT), g128=_group_matrix(MD, DH_MEM),
        wgates=cols["gates"].astype(BF16),
        wqkT=wqkv[:, :2 * HD].T, wvT=wqkv[:, 2 * HD:].T, wqiT=cols["qi"].astype(BF16).T, wkwT=wkw.T,
        qg_col=jnp.tile(q_norm_g, H_ATT).reshape(HD, 1), kg_col=jnp.tile(k_norm_g, H_ATT).reshape(HD, 1),
        kwscale_col=kwscale.reshape(LANES, 1),
    )


def _group_rms_t(xt, gmat, g, group):
    x2 = xt * xt
    hi = x2.astype(BF16)
    lo = (x2 - hi.astype(F32)).astype(BF16)
    ss = jnp.dot(gmat, hi, preferred_element_type=F32) + jnp.dot(gmat, lo, preferred_element_type=F32)
    return xt * lax.rsqrt(ss * (1.0 / group) + EPS) * g


def _proj_t_kernel(x_ref, g_ref, wqkT_ref, wvT_ref, wqiT_ref, wkwT_ref, wl_ref, wqm_ref, qgT_ref, kgT_ref, mg_ref,
                   kwscaleT_ref, g64_ref, g128_ref,
                   kT_ref, vT_ref, kiT_ref, wT_ref, qT_ref, qiT_ref, kb_ref, kwb_ref, vTb_ref, lx_ref, lg_ref,
                   qmb_ref):
    hn = _rms(x_ref[0], g_ref[...]).astype(BF16)
    nt = lambda w_ref: lax.dot_general(w_ref[...], hn, NT_DIMS, preferred_element_type=F32)
    qkT = nt(wqkT_ref)
    qT = _group_rms_t(qkT[:HD], g64_ref[...], qgT_ref[...], DH_ATT)
    kT = _group_rms_t(qkT[HD:], g64_ref[...], kgT_ref[...], DH_ATT)
    kT_ref[0] = kT
    kb_ref[0] = kT.T.astype(BF16)
    qT_ref[0] = (qT * (DH_ATT ** -0.5 * LOG2E)).astype(BF16)
    vT = nt(wvT_ref)
    vT_ref[0] = vT
    for c in range(vTb_ref.shape[1]):
        vTb_ref[0, c] = vT[:, c * KB:(c + 1) * KB].astype(BF16)
    qiT_ref[0] = nt(wqiT_ref).astype(BF16)
    kwT = nt(wkwT_ref) * kwscaleT_ref[...]
    kiT_ref[0] = kwT[:D_IDX]
    wT_ref[0] = kwT[D_IDX:D_IDX + H_IDX]
    kwb_ref[0] = kwT.T.astype(BF16)
    lxg = jnp.dot(hn, wl_ref[...], preferred_element_type=F32)
    lx_ref[0] = lxg[:, :LRU_W]
    lg_ref[0] = lxg[:, LRU_W:]
    qm = jnp.dot(hn, wqm_ref[...], preferred_element_type=F32)
    qmb_ref[0] = _group_rms(qm, g128_ref[...], mg_ref[...], DH_MEM).astype(BF16)


def _proj_t(x, p, tm):
    b, s, _ = x.shape
    bc = lambda col: jnp.broadcast_to(col, (col.shape[0], tm))
    consts = [p["g"], p["wqkT"], p["wvT"], p["wqiT"], p["wkwT"], p["wl"], p["wqm"], bc(p["qg_col"]),
              bc(p["kg_col"]), p["mg"], bc(p["kwscale_col"]), p["g64"], p["g128"]]
    tok = lambda w: pl.BlockSpec((1, tm, w), lambda i, j: (i, j, 0))
    feat = lambda w: pl.BlockSpec((1, w, tm), lambda i, j: (i, 0, j))
    outs = [
        (feat(HD), (b, HD, s), F32), (feat(HD), (b, HD, s), F32), (feat(D_IDX), (b, D_IDX, s), F32),
        (feat(H_IDX), (b, H_IDX, s), F32), (feat(HD), (b, HD, s), BF16), (feat(HD), (b, HD, s), BF16),
        (tok(HD), (b, s, HD), BF16), (tok(LANES), (b, s, LANES), BF16),
        (pl.BlockSpec((1, tm // KB, HD, KB), lambda i, j: (i, j, 0, 0)), (b, s // KB, HD, KB), BF16),
        (tok(LRU_W), (b, s, LRU_W), F32), (tok(LRU_W), (b, s, LRU_W), F32), (tok(MD), (b, s, MD), BF16),
    ]
    return pl.pallas_call(
        _proj_t_kernel,
        grid=(b, s // tm),
        in_specs=[tok(D_MODEL)] + [_const_spec(c.shape) for c in consts],
        out_specs=[o[0] for o in outs],
        out_shape=[jax.ShapeDtypeStruct(o[1], o[2]) for o in outs],
        compiler_params=_params(2),
        name="proj_t",
    )(x, *consts)


def _proj(x, p, tm):
    n = x.shape[0]
    consts = [p[k] for k in ("g", "wqkv", "wqi", "wkw", "wl", "wqm", "qg", "kg", "mg", "kwscale", "g64", "g128")]

    def row(w):
        return pl.BlockSpec((tm, w), lambda i: (i, 0))

    outs = [(HD, F32), (HD, F32), (LANES, F32), (HD, BF16), (HD, BF16), (HD, BF16), (HD, BF16), (LANES, BF16),
            (LRU_W, F32), (LRU_W, F32), (MD, BF16)]
    return pl.pallas_call(
        _proj_kernel,
        grid=(n // tm,),
        in_specs=[row(D_MODEL)] + [_const_spec(c.shape) for c in consts],
        out_specs=[row(w) for w, _ in outs],
        out_shape=[jax.ShapeDtypeStruct((n, w), dt) for w, dt in outs],
        compiler_params=_params(1),
        name="proj",
    )(x, *consts)


def _memkv_kernel(m_ref, g_ref, w_ref, kg_ref, g128_ref, mk_ref, mv_ref):
    hn = _rms(m_ref[...], g_ref[...]).astype(BF16)
    kv = jnp.dot(hn, w_ref[...], preferred_element_type=F32)
    mk_ref[...] = _group_rms(kv[:, :MD], g128_ref[...], kg_ref[...], DH_MEM)
    mv_ref[...] = kv[:, MD:]


def _memkv(mem, norm_g, w_mem_kv, mem_k_norm_g, g128):
    n = mem.shape[0]
    tm = min(n, 512)
    consts = [norm_g.reshape(1, D_MODEL), w_mem_kv.astype(BF16), jnp.tile(mem_k_norm_g, H_MEM).reshape(1, MD), g128]
    row = lambda w: pl.BlockSpec((tm, w), lambda i: (i, 0))
    return pl.pallas_call(
        _memkv_kernel,
        grid=(n // tm,),
        in_specs=[row(D_MODEL)] + [_const_spec(c.shape) for c in consts],
        out_specs=[row(MD), row(MD)],
        out_shape=[jax.ShapeDtypeStruct((n, MD), F32)] * 2,
        compiler_params=_params(1),
        name="memkv",
    )(mem, *consts)


def _memattn_kernel(q_ref, mk_ref, mv_ref, o_ref):
    q = q_ref[0]
    mk = mk_ref[0].astype(BF16)
    mv = mv_ref[0].astype(BF16)
    for h in range(H_MEM):
        sl = slice(h * DH_MEM, (h + 1) * DH_MEM)
        s = lax.dot_general(q[:, sl], mk[:, sl], NT_DIMS, preferred_element_type=F32) * (DH_MEM ** -0.5)
        m = jnp.max(s, axis=-1, keepdims=True)
        e = jnp.exp(s - m)
        p = (e / jnp.sum(e, axis=-1, keepdims=True)).astype(BF16)
        o_ref[0, :, sl] = jnp.dot(p, mv[:, sl], preferred_element_type=F32).astype(BF16)


def _memattn(qm, mk, mv, tm):
    b, t, _ = qm.shape
    n_mem = mk.shape[1]
    return pl.pallas_call(
        _memattn_kernel,
        grid=(b, t // tm),
        in_specs=[pl.BlockSpec((1, tm, MD), lambda i, j: (i, j, 0)),
                  pl.BlockSpec((1, n_mem, MD), lambda i, j: (i, 0, 0)),
                  pl.BlockSpec((1, n_mem, MD), lambda i, j: (i, 0, 0))],
        out_specs=pl.BlockSpec((1, tm, MD), lambda i, j: (i, j, 0)),
        out_shape=jax.ShapeDtypeStruct((b, t, MD), BF16),
        compiler_params=_params(2),
        name="memattn",
    )(qm, mk, mv)


def _shift_rows(x, k, fill):
    rows = lax.broadcasted_iota(I32, x.shape, 0)
    return jnp.where(rows >= k, pltpu.roll(x, k, 0), fill)


def _rglru_kernel(lx_ref, lg_ref, cs_ref, h0_ref, cw_ref, cb_ref, wa_ref, ba_ref, wi_ref, bi_ref, lam_ref,
                  y_ref, nb_ref, hl_ref, tail_ref, h_ref):
    t = pl.program_id(1)

    @pl.when(t == 0)
    def _():
        tail_ref[...] = cs_ref[0]
        h_ref[...] = h0_ref[0]

    x = lx_ref[0]
    tt = x.shape[0]
    tail = tail_ref[...]
    rows8 = lax.broadcasted_iota(I32, tail.shape, 0)
    conv = cb_ref[...] + x * cw_ref[CONV_W - 1:CONV_W, :]
    for d in range(1, CONV_W):
        xs = pltpu.roll(x, d, 0)
        head = jnp.where(rows8 < d, pltpu.roll(tail, d, 0), xs[:SUBLANES])
        xs = head if tt == SUBLANES else jnp.concatenate([head, xs[SUBLANES:]], axis=0)
        conv = conv + xs * cw_ref[CONV_W - 1 - d:CONV_W - d, :]
    tail_ref[...] = x[tt - SUBLANES:]
    nb_ref[0] = x[tt - SUBLANES:]

    cb16 = conv.astype(BF16)
    r = jax.nn.sigmoid(jnp.dot(cb16, wa_ref[...], preferred_element_type=F32) + ba_ref[...])
    ig = jax.nn.sigmoid(jnp.dot(cb16, wi_ref[...], preferred_element_type=F32) + bi_ref[...])
    nl = -lam_ref[...]
    softplus = jnp.maximum(nl, 0.0) + jnp.log1p(jnp.exp(-jnp.abs(nl)))
    log_a = -LRU_C * r * softplus
    a = jnp.exp(log_a)
    b = jnp.sqrt(-jnp.tanh(log_a) * (jnp.exp(2.0 * log_a) + 1.0)) * (ig * conv)
    rows = lax.broadcasted_iota(I32, x.shape, 0)
    b = b + jnp.where(rows == 0, a * h_ref[SUBLANES - 1:SUBLANES, :], 0.0)
    k = 1
    while k < tt:
        b = a * _shift_rows(b, k, 0.0) + b
        a = a * _shift_rows(a, k, 1.0)
        k *= 2
    h = b
    h_ref[...] = h[tt - SUBLANES:]
    hl_ref[0] = h[tt - SUBLANES:]
    y_ref[0] = (h * jax.nn.gelu(lg_ref[0])).astype(BF16)


def _block_diag(w):
    nb, bs, _ = w.shape
    eye = jnp.eye(nb, dtype=w.dtype)
    return (eye[:, None, :, None] * w[:, :, None, :]).reshape(nb * bs, nb * bs)


def _prep_rglru(conv_w, conv_b, lru_wa, lru_ba, lru_wi, lru_bi, lru_lambda):
    r = lambda v: v.reshape(1, LRU_W)
    return [conv_w, r(conv_b), _block_diag(lru_wa).astype(BF16), r(lru_ba), _block_diag(lru_wi).astype(BF16),
            r(lru_bi), r(lru_lambda)]


def _rglru(lx, lg, conv_state, h0, consts, tt):
    b, t, w = lx.shape
    cs = jnp.concatenate([jnp.zeros((b, SUBLANES - (CONV_W - 1), w), F32), conv_state], axis=1)
    h0p = jnp.concatenate([jnp.zeros((b, SUBLANES - 1, w), F32), h0[:, None, :]], axis=1)
    seq = pl.BlockSpec((1, tt, w), lambda i, j: (i, j, 0))
    st = pl.BlockSpec((1, SUBLANES, w), lambda i, j: (i, 0, 0))
    y, nb, hl = pl.pallas_call(
        _rglru_kernel,
        grid=(b, t // tt),
        in_specs=[seq, seq, st, st] + [_const_spec(c.shape) for c in consts],
        out_specs=[seq, st, st],
        out_shape=[jax.ShapeDtypeStruct((b, t, w), BF16), jax.ShapeDtypeStruct((b, SUBLANES, w), F32),
                   jax.ShapeDtypeStruct((b, SUBLANES, w), F32)],
        scratch_shapes=[pltpu.VMEM((SUBLANES, w), F32), pltpu.VMEM((SUBLANES, w), F32)],
        compiler_params=pltpu.CompilerParams(dimension_semantics=("parallel", "arbitrary"),
                                             vmem_limit_bytes=VMEM_LIMIT),
        name="rglru",
    )(lx, lg, cs, h0p, *consts)
    return y, nb[:, SUBLANES - (CONV_W - 1):], hl[:, SUBLANES - 1]


def _score_key(score):
    bits = lax.bitcast_convert_type(score, I32)
    return bits ^ ((bits >> 31) & 0x7FFFFFFF)


def _radix_threshold(count_ge, rows, topk):
    def bit_step(i, thr_u):
        cand_u = thr_u | (jnp.int32(1) << (31 - i))
        cnt = count_ge(cand_u ^ INT_MIN)
        return jnp.where(cnt >= topk, cand_u, thr_u)

    thr_u = lax.fori_loop(0, 32, bit_step, jnp.zeros((rows, 1), I32))
    return jnp.maximum(thr_u ^ INT_MIN, KEY_NEG_INF + 1)


def _dsa_prompt_kernel(qT_ref, qiT_ref, wT_ref, kw_ref, k_ref, vT_ref, tri_ref, o_ref,
                       sc_ref, qpad_ref, qipad_ref, oT_ref, s_scr, s2_scr, p_scr, *, topk):
    i = pl.program_id(1)
    nk = i + 1
    kf = float(topk)
    kblock = lambda j: pl.ds(pl.multiple_of(j * KB, KB), KB)

    zeros64 = jnp.zeros((D_IDX, TQ), BF16)
    for h in range(H_IDX):
        qipad_ref[h] = jnp.concatenate([qiT_ref[0, h * D_IDX:(h + 1) * D_IDX, :], zeros64], axis=0)
    for h in range(H_ATT):
        qh = qT_ref[0, h * DH_ATT:(h + 1) * DH_ATT, :]
        qpad_ref[h] = jnp.concatenate([qh, zeros64] if h % 2 == 0 else [zeros64, qh], axis=0)
    wT = wT_ref[0]

    def block_scores(j):
        kw = kw_ref[0, kblock(j), :]
        acc = jnp.zeros((KB, TQ), F32)
        for h in range(H_IDX):
            d = jnp.dot(kw, qipad_ref[h], preferred_element_type=F32)
            acc = acc + jnp.maximum(d, 0.0) * wT[h:h + 1, :]
        return acc

    def score_block(j, carry):
        lo, hi = carry
        acc = block_scores(j)
        sc_ref[kblock(j), :] = acc
        return (jnp.minimum(lo, jnp.min(acc, axis=0, keepdims=True)),
                jnp.maximum(hi, jnp.max(acc, axis=0, keepdims=True)))

    init = (jnp.full((1, TQ), jnp.inf, F32), jnp.full((1, TQ), NEG_INF, F32))
    rmin, rmax = lax.fori_loop(0, i, score_block, init)
    acc = block_scores(i)
    krow = lax.broadcasted_iota(I32, (KB, TQ), 0)
    qcol = lax.broadcasted_iota(I32, (KB, TQ), 1)
    sc_ref[kblock(i), :] = jnp.where(krow <= qcol, acc, NEG_INF)
    rmin = jnp.minimum(rmin, jnp.min(acc, axis=0, keepdims=True))
    rmax = jnp.maximum(rmax, jnp.max(acc, axis=0, keepdims=True))

    def count(pred):
        def body(j, cnt):
            m = jnp.where(pred(sc_ref[kblock(j), :]), 1.0, 0.0)
            for r in range(KB // CNT_ROWS):
                cnt = cnt + m[r * CNT_ROWS:(r + 1) * CNT_ROWS]
            return cnt

        cnt = lax.fori_loop(0, nk, body, jnp.zeros((CNT_ROWS, TQ), F32))
        return jnp.sum(cnt, axis=0, keepdims=True)

    def n_open(done):
        return jnp.sum(1.0 - done).astype(I32)

    n_valid = i * TQ + lax.broadcasted_iota(I32, (1, TQ), 1) + 1
    done0 = jnp.where(n_valid <= topk, 1.0, 0.0)

    def bisect(_, c):
        lo, hi, done = c
        mid = 0.5 * lo + 0.5 * jnp.minimum(hi, rmax)
        c_mid = count(lambda x: x >= mid)
        ge = c_mid >= kf
        live = done < 0.5
        lo = jnp.where(live & ge, mid, lo)
        hi = jnp.where(live & jnp.logical_not(ge), mid, hi)
        done = jnp.maximum(done, jnp.where(c_mid == kf, 1.0, 0.0))
        return lo, hi, done

    lo, hi, done = lax.fori_loop(0, BISECT_STEPS, bisect, (rmin, jnp.full((1, TQ), jnp.inf, F32), done0))
    left = n_open(done)

    def step_down(c):
        lo, hi, done, _ = c

        def body(j, best):
            x = sc_ref[kblock(j), :]
            return jnp.maximum(best, jnp.max(jnp.where(x < hi, x, NEG_INF), axis=0, keepdims=True))

        cand = lax.fori_loop(0, nk, body, jnp.full((1, TQ), NEG_INF, F32))
        ok = count(lambda x: x >= cand) >= kf
        live = done < 0.5
        lo = jnp.where(live & ok, cand, lo)
        hi = jnp.where(live & jnp.logical_not(ok), cand, hi)
        done = jnp.maximum(done, jnp.where(ok, 1.0, 0.0))
        return lo, hi, done, n_open(done)

    thr, _, _, _ = lax.while_loop(lambda c: c[3] > 0, step_down, (lo, hi, done, left))
    n_tied = jnp.sum(jnp.where(count(lambda x: x >= thr) > kf, 1.0, 0.0)).astype(I32)

    @pl.when(n_tied == 0)
    def _():
        def body(j, carry):
            x = sc_ref[kblock(j), :]
            sc_ref[kblock(j), :] = jnp.where(x >= thr, 0.0, NEG_INF)
            return carry

        lax.fori_loop(0, nk, body, 0)

    @pl.when(n_tied > 0)
    def _():
        quota = kf - count(lambda x: x > thr)

        def body(j, ties_before):
            x = sc_ref[kblock(j), :]
            eq = jnp.where(x == thr, 1.0, 0.0)
            rank = jnp.dot(tri_ref[...], eq.astype(BF16), preferred_element_type=F32) + ties_before
            sel = (x > thr) | ((x == thr) & (rank < quota))
            sc_ref[kblock(j), :] = jnp.where(sel, 0.0, NEG_INF)
            return ties_before + jnp.sum(eq, axis=0, keepdims=True)

        lax.fori_loop(0, nk, body, jnp.zeros((1, TQ), F32))

    n_pairs = (nk + 1) // 2

    @pl.when(nk % 2 == 1)
    def _():
        sc_ref[kblock(nk), :] = jnp.full((KB, TQ), NEG_INF, F32)

    def score_phase(j, s_out):
        block_max = []
        for h in range(H_ATT):
            pair = slice((h // 2) * LANES, (h // 2 + 1) * LANES)
            s = jnp.dot(k_ref[0, kblock(j), pair], qpad_ref[h], preferred_element_type=F32)
            s = s + sc_ref[kblock(j), :]
            s_out[h] = s
            block_max.append(jnp.max(s, axis=0, keepdims=True))
        return tuple(block_max)

    ones_rows = (lax.broadcasted_iota(I32, (SUM_ROWS, KB), 0) == 0).astype(BF16)

    def block_step(j, next_j, s_in, s_out, carry):
        ms, accs, block_max = carry
        next_max = score_phase(next_j, s_out)
        new_ms, new_accs, alphas = [], [], []
        for h in range(H_ATT):
            m_new = jnp.maximum(ms[h], block_max[h])
            m_safe = jnp.where(m_new == NEG_INF, 0.0, m_new)
            alphas.append(jnp.exp2(ms[h] - m_safe))
            new_ms.append(m_new)
            p_scr[h] = jnp.exp2(s_in[h] - m_safe).astype(BF16)
        for h in range(H_ATT):
            hrows = slice(h * DH_ATT, (h + 1) * DH_ATT)
            v_aug = jnp.concatenate([vT_ref[0, j, hrows, :], ones_rows], axis=0)
            pv = jnp.dot(v_aug, p_scr[h], preferred_element_type=F32)
            new_accs.append(alphas[h] * accs[h] + pv)
        return tuple(new_ms), tuple(new_accs), next_max

    def attend_pair(jj, carry):
        j0 = 2 * jj
        carry = block_step(j0, j0 + 1, s_scr, s2_scr, carry)
        return block_step(j0 + 1, jnp.minimum(j0 + 2, 2 * n_pairs - 1), s2_scr, s_scr, carry)

    init = (tuple(jnp.full((1, TQ), NEG_INF, F32) for _ in range(H_ATT)),
            tuple(jnp.zeros((DH_ATT + SUM_ROWS, TQ), F32) for _ in range(H_ATT)),
            score_phase(0, s_scr))
    _, accs, _ = lax.fori_loop(0, n_pairs, attend_pair, init)
    for h in range(H_ATT):
        oT_ref[h * DH_ATT:(h + 1) * DH_ATT, :] = accs[h][:DH_ATT] / accs[h][DH_ATT:DH_ATT + 1]
    o_ref[0] = oT_ref[...].T.astype(BF16)


def _dsa_prompt(qT, qiT, wT, kwb, kb, vTb):
    b, _, s = qT.shape
    assert s % (2 * KB) == 0 and TQ == KB
    topk = min(TOPK_MAX, s // 4)
    tri = (jnp.arange(KB)[:, None] > jnp.arange(KB)[None, :]).astype(BF16)
    feat = lambda w: pl.BlockSpec((1, w, TQ), lambda i, j: (i, 0, j))
    full = lambda w: pl.BlockSpec((1, s, w), lambda i, j: (i, 0, 0))
    return pl.pallas_call(
        functools.partial(_dsa_prompt_kernel, topk=topk),
        grid=(b, s // TQ),
        in_specs=[feat(HD), feat(HD), feat(H_IDX), full(LANES), full(HD),
                  pl.BlockSpec((1, s // KB, HD, KB), lambda i, j: (i, 0, 0, 0)), _const_spec(tri.shape)],
        out_specs=pl.BlockSpec((1, TQ, HD), lambda i, j: (i, j, 0)),
        out_shape=jax.ShapeDtypeStruct((b, s, HD), BF16),
        scratch_shapes=[pltpu.VMEM((s, TQ), F32), pltpu.VMEM((H_ATT, LANES, TQ), BF16),
                        pltpu.VMEM((H_IDX, LANES, TQ), BF16), pltpu.VMEM((HD, TQ), F32),
                        pltpu.VMEM((H_ATT, KB, TQ), F32), pltpu.VMEM((H_ATT, KB, TQ), F32),
                        pltpu.VMEM((H_ATT, KB, TQ), BF16)],
        compiler_params=_params(2),
        name="dsa_prompt",
    )(qT, qiT, wT, kwb, kb, vTb, tri)


PAGE_GROUP_IDX = 32
PAGE_GROUP_KV = 32
COUNT_LANES = 4


def _dsa_sample_select_kernel(pt_ref, qi_ref, kwq_ref, tri_ref, *rest, n_pages, topk, group):
    page_refs, (bias_ref, biasn_ref, qiall_ref, wb_ref) = rest[:group], rest[group:]
    pg = pl.program_id(1)
    t = qi_ref.shape[1]
    kwq = kwq_ref[0]

    @pl.when(pg == 0)
    def _():
        qi = qi_ref[0].astype(F32)
        qiall_ref[...] = jnp.concatenate(
            [qi[:, h * D_IDX:(h + 1) * D_IDX] for h in range(H_IDX)], axis=0).astype(BF16)
        for h in range(H_IDX):
            wb_ref[h] = jnp.broadcast_to(kwq[:, D_IDX + h:D_IDX + h + 1], (t, LANES))

    def scores(dots):
        acc = jnp.zeros((t, dots.shape[1]), F32)
        for h in range(H_IDX):
            acc = acc + jnp.maximum(dots[h * t:(h + 1) * t], 0.0) * wb_ref[h]
        return acc

    for g in range(group):
        dots = jnp.dot(qiall_ref[...], page_refs[g][0].astype(BF16), preferred_element_type=F32)
        bias_ref[0, pg * group + g] = _score_key(scores(dots))

    @pl.when(pg == pl.num_programs(1) - 1)
    def _():
        new_keys = jnp.concatenate([kwq[:, :D_IDX], jnp.zeros((LANES - t, D_IDX), F32)], axis=0).astype(BF16)
        row = lax.broadcasted_iota(I32, (t, LANES), 0)
        col = lax.broadcasted_iota(I32, (t, LANES), 1)
        dots_new = lax.dot_general(qiall_ref[...], new_keys, NT_DIMS, preferred_element_type=F32)
        key_new = _score_key(jnp.where(col <= row, scores(dots_new), NEG_INF))

        def count(cmp_fn):
            hits = cmp_fn(bias_ref[0]).astype(I32).reshape(COUNT_LANES, n_pages // COUNT_LANES, t, -1)
            cnt = jnp.sum(jnp.sum(hits, axis=1), axis=0) + cmp_fn(key_new).astype(I32)
            return jnp.sum(cnt, axis=1, keepdims=True)

        thr = _radix_threshold(lambda c: count(lambda key: key >= c), t, topk)
        n_inexact = jnp.sum(jnp.where(count(lambda key: key >= thr) == topk, 0, 1))
        as_bias = lambda sel: lax.bitcast_convert_type(jnp.where(sel, 0.0, NEG_INF), I32)

        @pl.when(n_inexact == 0)
        def _():
            bias_ref[0] = as_bias(bias_ref[0] >= thr)
            biasn_ref[0] = as_bias(key_new >= thr)

        @pl.when(n_inexact > 0)
        def _():
            quota = (topk - count(lambda key: key > thr)).astype(F32)

            def select(key, ties_before):
                eq = key == thr
                eqf = eq.astype(F32)
                rank = jnp.dot(eqf.astype(BF16), tri_ref[...], preferred_element_type=F32) + ties_before
                sel = (key > thr) | (eq & (rank < quota))
                return as_bias(sel), ties_before + jnp.sum(eqf, axis=1, keepdims=True)

            def bias_page(p, ties_before):
                bias, ties = select(bias_ref[0, p], ties_before)
                bias_ref[0, p] = bias
                return ties

            ties = lax.fori_loop(0, n_pages, bias_page, jnp.zeros((t, 1), F32))
            biasn_ref[0], _ = select(key_new, ties)


def _dsa_sample_select(page_table, qib, kw32, cache_kidx_t):
    b, t, _ = qib.shape
    n_pages = page_table.shape[1]
    page = cache_kidx_t.shape[2]
    topk = min(TOPK_MAX, (n_pages * page + t) // 4)
    group = min(PAGE_GROUP_IDX, n_pages)
    assert n_pages % group == 0 and n_pages % COUNT_LANES == 0
    tri = (jnp.arange(page)[:, None] < jnp.arange(page)[None, :]).astype(BF16)
    tok = lambda w: pl.BlockSpec((1, t, w), lambda i, j, pt: (i, 0, 0))
    page_specs = [pl.BlockSpec((1, D_IDX, page), lambda i, j, pt, g=g: (pt[i, j * group + g], 0, 0))
                  for g in range(group)]
    grid_spec = pltpu.PrefetchScalarGridSpec(
        num_scalar_prefetch=1,
        grid=(b, n_pages // group),
        in_specs=[tok(H_IDX * D_IDX), tok(LANES), pl.BlockSpec(tri.shape, lambda i, j, pt: (0, 0))] + page_specs,
        out_specs=[pl.BlockSpec((1, n_pages, t, page), lambda i, j, pt: (i, 0, 0, 0)),
                   pl.BlockSpec((1, t, LANES), lambda i, j, pt: (i, 0, 0))],
        scratch_shapes=[pltpu.VMEM((H_IDX * t, D_IDX), BF16), pltpu.VMEM((H_IDX, t, LANES), F32)],
    )
    return pl.pallas_call(
        functools.partial(_dsa_sample_select_kernel, n_pages=n_pages, topk=topk, group=group),
        grid_spec=grid_spec,
        out_shape=[jax.ShapeDtypeStruct((b, n_pages, t, page), I32), jax.ShapeDtypeStruct((b, t, LANES), I32)],
        compiler_params=pltpu.CompilerParams(dimension_semantics=("parallel", "arbitrary"),
                                             vmem_limit_bytes=VMEM_LIMIT),
        name="dsa_sample_select",
    )(page_table, qib, kw32, tri, *([cache_kidx_t] * group))


def _dsa_sample_attend_kernel(pt_ref, q_ref, kn_ref, vn_ref, bias_ref, biasn_ref, *rest, group):
    kT_refs, vT_refs = rest[:group], rest[group:2 * group]
    o_ref, m_ref, l_ref, acc_ref = rest[2 * group:]
    pg = pl.program_id(1)
    t = q_ref.shape[1]
    page = kT_refs[0].shape[2]
    lane_head = lax.broadcasted_iota(I32, (t, HD), 1) // DH_ATT
    q = q_ref[0].astype(F32)
    qbd = jnp.concatenate([jnp.where(lane_head == h, q, 0.0) for h in range(H_ATT)], axis=0).astype(BF16)

    @pl.when(pg == 0)
    def _():
        m_ref[...] = jnp.full(m_ref.shape, NEG_INF, F32)
        l_ref[...] = jnp.zeros(l_ref.shape, F32)
        acc_ref[...] = jnp.zeros(acc_ref.shape, F32)

    def update(s, bias, pv):
        s = s + jnp.concatenate([lax.bitcast_convert_type(bias, F32)] * H_ATT, axis=0)
        m = m_ref[...]
        m_new = jnp.maximum(m, jnp.max(s, axis=1, keepdims=True))
        m_safe = jnp.where(m_new == NEG_INF, 0.0, m_new)
        alpha = jnp.exp(m - m_safe)
        p = jnp.exp(s - m_safe)
        l_ref[...] = alpha * l_ref[...] + jnp.sum(p, axis=1, keepdims=True)
        acc_ref[...] = alpha * acc_ref[...] + pv(p.astype(BF16))
        m_ref[...] = m_new

    s_pages = jnp.concatenate(
        [jnp.dot(qbd, kT_refs[g][0].astype(BF16), preferred_element_type=F32) for g in range(group)], axis=1)
    bias_pages = jnp.concatenate([bias_ref[0, g] for g in range(group)], axis=1)

    def pv_pages(p):
        out = jnp.zeros((H_ATT * t, HD), F32)
        for g in range(group):
            out = out + lax.dot_general(p[:, g * page:(g + 1) * page], vT_refs[g][0].astype(BF16), NT_DIMS,
                                        preferred_element_type=F32)
        return out

    update(s_pages, bias_pages, pv_pages)

    @pl.when(pg == pl.num_programs(1) - 1)
    def _():
        pad = jnp.zeros((LANES - t, HD), F32)
        kn = jnp.concatenate([kn_ref[0], pad], axis=0).astype(BF16)
        vn = jnp.concatenate([vn_ref[0], pad], axis=0).astype(BF16)
        update(lax.dot_general(qbd, kn, NT_DIMS, preferred_element_type=F32), biasn_ref[0],
               lambda p: jnp.dot(p, vn, preferred_element_type=F32))
        o = acc_ref[...] / l_ref[...]
        out = jnp.zeros((t, HD), F32)
        for h in range(H_ATT):
            out = out + jnp.where(lane_head == h, o[h * t:(h + 1) * t], 0.0)
        o_ref[0] = out.astype(BF16)


def _dsa_sample_attend(page_table, qb, k32, v32, bias, bias_new, cache_kt, cache_vt):
    b, t, _ = qb.shape
    n_pages = page_table.shape[1]
    page = cache_kt.shape[2]
    group = min(PAGE_GROUP_KV, n_pages)
    assert n_pages % group == 0
    tok = lambda w: pl.BlockSpec((1, t, w), lambda i, j, pt: (i, 0, 0))
    kv_specs = [pl.BlockSpec((1, HD, page), lambda i, j, pt, g=g: (pt[i, j * group + g], 0, 0))
                for g in range(group)]
    grid_spec = pltpu.PrefetchScalarGridSpec(
        num_scalar_prefetch=1,
        grid=(b, n_pages // group),
        in_specs=[tok(HD), tok(HD), tok(HD),
                  pl.BlockSpec((1, group, t, page), lambda i, j, pt: (i, j, 0, 0)),
                  tok(LANES)] + kv_specs + kv_specs,
        out_specs=tok(HD),
        scratch_shapes=[pltpu.VMEM((H_ATT * t, 1), F32), pltpu.VMEM((H_ATT * t, 1), F32),
                        pltpu.VMEM((H_ATT * t, HD), F32)],
    )
    return pl.pallas_call(
        functools.partial(_dsa_sample_attend_kernel, group=group),
        grid_spec=grid_spec,
        out_shape=jax.ShapeDtypeStruct((b, t, HD), BF16),
        compiler_params=pltpu.CompilerParams(dimension_semantics=("parallel", "arbitrary"),
                                             vmem_limit_bytes=VMEM_LIMIT),
        name="dsa_sample_attend",
    )(page_table, qb, k32, v32, bias, bias_new, *([cache_kt] * group), *([cache_vt] * group))


def _merge_kernel(x_ref, oa_ref, ol_ref, om_ref, g_ref, wg_ref, wa_ref, wl_ref, wm_ref, wo_ref, o_ref):
    x = x_ref[...]
    hn = _rms(x, g_ref[...]).astype(BF16)
    m = jnp.zeros_like(x)
    for idx, (o_r, w_r) in enumerate(((oa_ref, wa_ref), (ol_ref, wl_ref), (om_ref, wm_ref))):
        gate = jax.nn.sigmoid(jnp.dot(hn, wg_ref[:, idx * D_MODEL:(idx + 1) * D_MODEL], preferred_element_type=F32))
        m = m + gate * jnp.dot(o_r[...], w_r[...], preferred_element_type=F32)
    o_ref[...] = x + jnp.dot(m.astype(BF16), wo_ref[...], preferred_element_type=F32)


def _merge(x, o_att, o_lru, o_mem, consts, tm):
    n = x.shape[0]
    row = lambda w: pl.BlockSpec((tm, w), lambda i: (i, 0))
    return pl.pallas_call(
        _merge_kernel,
        grid=(n // tm,),
        in_specs=[row(D_MODEL), row(HD), row(LRU_W), row(MD)] + [_const_spec(c.shape) for c in consts],
        out_specs=row(D_MODEL),
        out_shape=jax.ShapeDtypeStruct((n, D_MODEL), F32),
        compiler_params=_params(1),
        name="merge",
    )(x, o_att, o_lru, o_mem, *consts)


def _token_tile(n):
    return min(n, 512)


def _layer(x, is_prompt, lw, pp, extra):
    b, t, _ = x.shape
    n = b * t
    tm = _token_tile(n)
    x1 = _ffn(x.reshape(n, D_MODEL), pp["ffn1"], tm)
    r3 = lambda a: a.reshape(b, t, a.shape[-1])
    if is_prompt:
        kT32, vT32, kiT32, wT, qT, qiT, kb, kwb, vTb, lx, lg, qmb = _proj_t(r3(x1), pp["proj"], tm)
        o_att = _dsa_prompt(qT, qiT, wT, kwb, kb, vTb)
        k_new = kT32.reshape(b, H_ATT, DH_ATT, t).transpose(0, 3, 1, 2)
        v_new = vT32.reshape(b, H_ATT, DH_ATT, t).transpose(0, 3, 1, 2)
        ki_new = kiT32.transpose(0, 2, 1)
        conv_state = jnp.zeros((b, CONV_W - 1, LRU_W), F32)
        h0 = jnp.zeros((b, LRU_W), F32)
        mem = extra["mem"]
        mk, mv = _memkv(mem.reshape(-1, D_MODEL), lw["norm_mem_g"], lw["w_mem_kv"], lw["mem_k_norm_g"],
                        pp["proj"]["g128"])
        mk = mk.reshape(b, -1, MD)
        mv = mv.reshape(b, -1, MD)
    else:
        k32, v32, kw32, qb, _, _, qib, _, lx, lg, qmb = _proj(x1, pp["proj"], tm)
        pt = extra["page_table"]
        bias, bias_new = _dsa_sample_select(pt, r3(qib), r3(kw32), extra["cache_kidx_t"])
        o_att = _dsa_sample_attend(pt, r3(qb), r3(k32), r3(v32), bias, bias_new, extra["cache_kt"],
                                   extra["cache_vt"])
        k_new = k32.reshape(b, t, H_ATT, DH_ATT)
        v_new = v32.reshape(b, t, H_ATT, DH_ATT)
        ki_new = r3(kw32)[:, :, :D_IDX]
        conv_state, h0 = extra["state_conv"], extra["state_h"]
        mk, mv = extra["cache_mem_k"], extra["cache_mem_v"]
    o_lru, conv_buf, h_last = _rglru(r3(lx), r3(lg), conv_state, h0, pp["rglru"], min(t, 256))
    o_mem = _memattn(r3(qmb), mk, mv, min(t, 512))
    x2 = _merge(x1, o_att.reshape(n, HD), o_lru.reshape(n, LRU_W), o_mem.reshape(n, MD), pp["merge"], tm)
    y = _ffn(x2, pp["ffn2"], tm).reshape(b, t, D_MODEL)
    if is_prompt:
        state = (k_new, v_new, ki_new, mk.reshape(b, -1, H_MEM, DH_MEM), mv.reshape(b, -1, H_MEM, DH_MEM),
                 conv_buf, h_last)
    else:
        state = (k_new, v_new, ki_new, conv_buf, h_last)
    return y, state


def kernel(x_prompt, x_sample, cache_k, cache_v, cache_kidx, page_table, cache_mem_k, cache_mem_v, state_conv, state_h, mem_prompt, norm_ffn1_g, w_ffn1_in, w_ffn1_out, norm_mix_g, w_in, q_norm_g, k_norm_g, w_attn_o, conv_w, conv_b, lru_wa, lru_ba, lru_wi, lru_bi, lru_lambda, w_lru_o, norm_mem_g, w_mem_kv, mem_q_norm_g, mem_k_norm_g, w_mem_o, w_out, norm_ffn2_g, w_ffn2_in, w_ffn2_out):
    depth = w_in.shape[0]
    n_phys, page = cache_k.shape[1], cache_k.shape[2]
    xp, xs = x_prompt, x_sample
    p_states, s_states = [], []
    for l in range(depth):
        proj = _prep_proj(norm_mix_g[l], w_in[l], q_norm_g[l], k_norm_g[l], mem_q_norm_g[l])
        pp = dict(
            ffn1=_prep_ffn(norm_ffn1_g[l], w_ffn1_in[l], w_ffn1_out[l]),
            ffn2=_prep_ffn(norm_ffn2_g[l], w_ffn2_in[l], w_ffn2_out[l]),
            proj=proj,
            rglru=_prep_rglru(conv_w[l], conv_b[l], lru_wa[l], lru_ba[l], lru_wi[l], lru_bi[l], lru_lambda[l]),
            merge=[proj["g"], proj["wgates"], w_attn_o[l].astype(BF16), w_lru_o[l].astype(BF16),
                   w_mem_o[l].astype(BF16), w_out[l].astype(BF16)],
        )
        lw = dict(norm_mem_g=norm_mem_g[l], w_mem_kv=w_mem_kv[l], mem_k_norm_g=mem_k_norm_g[l])
        xp, st_p = _layer(xp, True, lw, pp, dict(mem=mem_prompt))
        xs, st_s = _layer(xs, False, lw, pp, dict(
            page_table=page_table,
            cache_kt=cache_k[l].transpose(0, 2, 3, 1).reshape(n_phys, HD, page),
            cache_vt=cache_v[l].transpose(0, 2, 3, 1).reshape(n_phys, HD, page),
            cache_kidx_t=cache_kidx[l].transpose(0, 2, 1), cache_mem_k=cache_mem_k[l].reshape(-1, cache_mem_k.shape[2], MD),
            cache_mem_v=cache_mem_v[l].reshape(-1, cache_mem_v.shape[2], MD),
            state_conv=state_conv[l], state_h=state_h[l]))
        p_states.append(st_p)
        s_states.append(st_s)
    stack = lambda states, i: jnp.stack([s[i] for s in states])
    return (xp, xs) + tuple(stack(p_states, i) for i in range(7)) + tuple(stack(s_states, i) for i in range(5))
```

```python
import functools

import jax
import jax.numpy as jnp
from jax import lax
from jax.experimental import pallas as pl
from jax.experimental.pallas import tpu as pltpu

F32, BF16, I32 = jnp.float32, jnp.bfloat16, jnp.int32

D_MODEL = 1024
H_ATT, DH_ATT = 8, 64
H_IDX, D_IDX = 8, 64
TOPK_MAX = 256
LRU_W, LRU_BLOCKS, CONV_W, LRU_C = 512, 8, 4, 8.0
H_MEM, DH_MEM = 4, 128
D_FF = 2816
EPS = 1e-6
HD = H_ATT * DH_ATT
MD = H_MEM * DH_MEM

LANES = 128
SUBLANES = 8
VMEM_BYTES_V7X = 64 * 1024 * 1024
VMEM_LIMIT = VMEM_BYTES_V7X - 8 * 1024 * 1024

FF_CHUNK = 256
TQ = 256
KB = 256
CNT_ROWS = 32
SUM_ROWS = 16
BISECT_STEPS = 15
LOG2E = 1.4426950408889634
INT_MIN = -2 ** 31
KEY_NEG_INF = -2139095041
NEG_INF = float("-inf")

NT_DIMS = (((1,), (1,)), ((), ()))


def _params(n_grid, parallel=True):
    sem = ("parallel" if parallel else "arbitrary",) * n_grid
    return pltpu.CompilerParams(dimension_semantics=sem, vmem_limit_bytes=VMEM_LIMIT)


def _const_spec(shape):
    nd = len(shape)
    return pl.BlockSpec(shape, lambda *_: (0,) * nd)


def _rms(x, g):
    ms = jnp.mean(x * x, axis=-1, keepdims=True)
    return x * lax.rsqrt(ms + EPS) * g


def _group_rms(x, gmat, g, group):
    x2 = x * x
    hi = x2.astype(BF16)
    lo = (x2 - hi.astype(F32)).astype(BF16)
    ss = jnp.dot(hi, gmat, preferred_element_type=F32) + jnp.dot(lo, gmat, preferred_element_type=F32)
    return x * lax.rsqrt(ss * (1.0 / group) + EPS) * g


def _group_matrix(width, group):
    idx = jnp.arange(width) // group
    return (idx[:, None] == idx[None, :]).astype(BF16)


def _ffn_kernel(x_ref, g_ref, wg_ref, wu_ref, wo_ref, o_ref):
    x = x_ref[...]
    hn = _rms(x, g_ref[...]).astype(BF16)
    acc = jnp.zeros_like(x)
    for c in range(wg_ref.shape[0]):
        gate = jnp.dot(hn, wg_ref[c], preferred_element_type=F32)
        up = jnp.dot(hn, wu_ref[c], preferred_element_type=F32)
        act = (gate * jax.nn.sigmoid(gate) * up).astype(BF16)
        acc = acc + jnp.dot(act, wo_ref[c], preferred_element_type=F32)
    o_ref[...] = x + 0.5 * acc


def _prep_ffn(g, w_in, w_out):
    nc = D_FF // FF_CHUNK
    wg = w_in[:, :D_FF].reshape(D_MODEL, nc, FF_CHUNK).transpose(1, 0, 2).astype(BF16)
    wu = w_in[:, D_FF:].reshape(D_MODEL, nc, FF_CHUNK).transpose(1, 0, 2).astype(BF16)
    wo = w_out.reshape(nc, FF_CHUNK, D_MODEL).astype(BF16)
    return g.reshape(1, D_MODEL), wg, wu, wo


def _ffn(x, prep, tm):
    g, wg, wu, wo = prep
    n = x.shape[0]
    row = pl.BlockSpec((tm, D_MODEL), lambda i: (i, 0))
    return pl.pallas_call(
        _ffn_kernel,
        grid=(n // tm,),
        in_specs=[row, _const_spec(g.shape), _const_spec(wg.shape), _const_spec(wu.shape), _const_spec(wo.shape)],
        out_specs=row,
        out_shape=jax.ShapeDtypeStruct((n, D_MODEL), F32),
        compiler_params=_params(1),
        name="ffn",
    )(x, g, wg, wu, wo)


def _proj_kernel(x_ref, g_ref, wqkv_ref, wqi_ref, wkw_ref, wl_ref, wqm_ref, qg_ref, kg_ref, mg_ref,
                 kwscale_ref, g64_ref, g128_ref,
                 k32_ref, v32_ref, kw32_ref, qb_ref, kb_ref, vb_ref, qib_ref, kwb_ref, lx_ref, lg_ref, qmb_ref):
    hn = _rms(x_ref[...], g_ref[...]).astype(BF16)
    qkv = jnp.dot(hn, wqkv_ref[...], preferred_element_type=F32)
    q = _group_rms(qkv[:, :HD], g64_ref[...], qg_ref[...], DH_ATT)
    k = _group_rms(qkv[:, HD:2 * HD], g64_ref[...], kg_ref[...], DH_ATT)
    v = qkv[:, 2 * HD:]
    k32_ref[...] = k
    v32_ref[...] = v
    qb_ref[...] = (q * (DH_ATT ** -0.5)).astype(BF16)
    kb_ref[...] = k.astype(BF16)
    vb_ref[...] = v.astype(BF16)
    qib_ref[...] = jnp.dot(hn, wqi_ref[...], preferred_element_type=F32).astype(BF16)
    kw = jnp.dot(hn, wkw_ref[...], preferred_element_type=F32) * kwscale_ref[...]
    kw32_ref[...] = kw
    kwb_ref[...] = kw.astype(BF16)
    lxg = jnp.dot(hn, wl_ref[...], preferred_element_type=F32)
    lx_ref[...] = lxg[:, :LRU_W]
    lg_ref[...] = lxg[:, LRU_W:]
    qm = jnp.dot(hn, wqm_ref[...], preferred_element_type=F32)
    qmb_ref[...] = _group_rms(qm, g128_ref[...], mg_ref[...], DH_MEM).astype(BF16)


def _prep_proj(norm_g, w_in, q_norm_g, k_norm_g, mem_q_norm_g):
    o = 0
    cols = {}
    for name, size in (("q", HD), ("k", HD), ("v", HD), ("qi", H_IDX * D_IDX), ("ki", D_IDX), ("wi", H_IDX),
                       ("lx", LRU_W), ("lg", LRU_W), ("qm", MD), ("gates", 3 * D_MODEL)):
        cols[name] = w_in[:, o:o + size]
        o += size
    pad = jnp.zeros((D_MODEL, LANES - D_IDX - H_IDX), w_in.dtype)
    wqkv = jnp.concatenate([cols["q"], cols["k"], cols["v"]], axis=1).astype(BF16)
    wkw = jnp.concatenate([cols["ki"], cols["wi"], pad], axis=1).astype(BF16)
    wl = jnp.concatenate([cols["lx"], cols["lg"]], axis=1).astype(BF16)
    kwscale = jnp.concatenate([jnp.ones((D_IDX,), F32),
                               jnp.full((H_IDX,), H_IDX ** -0.5 * D_IDX ** -0.5, F32),
                               jnp.zeros((LANES - D_IDX - H_IDX,), F32)]).reshape(1, LANES)
    return dict(
        g=norm_g.reshape(1, D_MODEL), wqkv=wqkv, wqi=cols["qi"].astype(BF16), wkw=wkw, wl=wl,
        wqm=cols["qm"].astype(BF16),
        qg=jnp.tile(q_norm_g, H_ATT).reshape(1, HD), kg=jnp.tile(k_norm_g, H_ATT).reshape(1, HD),
        mg=jnp.tile(mem_q_norm_g, H_MEM).reshape(1, MD), kwscale=kwscale,
        g64=_group_matrix(HD, DH_ATT), g128=_group_matrix(MD, DH_MEM),
        wgates=cols["gates"].astype(BF16),
        wqkT=wqkv[:, :2 * HD].T, wvT=wqkv[:, 2 * HD:].T, wqiT=cols["qi"].astype(BF16).T, wkwT=wkw.T,
        qg_col=jnp.tile(q_norm_g, H_ATT).reshape(HD, 1), kg_col=jnp.tile(k_norm_g, H_ATT).reshape(HD, 1),
        kwscale_col=kwscale.reshape(LANES, 1),
    )


def _group_rms_t(xt, gmat, g, group):
    x2 = xt * xt
    hi = x2.astype(BF16)
    lo = (x2 - hi.astype(F32)).astype(BF16)
    ss = jnp.dot(gmat, hi, preferred_element_type=F32) + jnp.dot(gmat, lo, preferred_element_type=F32)
    return xt * lax.rsqrt(ss * (1.0 / group) + EPS) * g


def _proj_t_kernel(x_ref, g_ref, wqkT_ref, wvT_ref, wqiT_ref, wkwT_ref, wl_ref, wqm_ref, qgT_ref, kgT_ref, mg_ref,
                   kwscaleT_ref, g64_ref, g128_ref,
                   kT_ref, vT_ref, kiT_ref, wT_ref, qT_ref, qiT_ref, kb_ref, kwb_ref, vTb_ref, lx_ref, lg_ref,
                   qmb_ref):
    hn = _rms(x_ref[0], g_ref[...]).astype(BF16)
    nt = lambda w_ref: lax.dot_general(w_ref[...], hn, NT_DIMS, preferred_element_type=F32)
    qkT = nt(wqkT_ref)
    qT = _group_rms_t(qkT[:HD], g64_ref[...], qgT_ref[...], DH_ATT)
    kT = _group_rms_t(qkT[HD:], g64_ref[...], kgT_ref[...], DH_ATT)
    kT_ref[0] = kT
    kb_ref[0] = kT.T.astype(BF16)
    qT_ref[0] = (qT * (DH_ATT ** -0.5 * LOG2E)).astype(BF16)
    vT = nt(wvT_ref)
    vT_ref[0] = vT
    for c in range(vTb_ref.shape[1]):
        vTb_ref[0, c] = vT[:, c * KB:(c + 1) * KB].astype(BF16)
    qiT_ref[0] = nt(wqiT_ref).astype(BF16)
    kwT = nt(wkwT_ref) * kwscaleT_ref[...]
    kiT_ref[0] = kwT[:D_IDX]
    wT_ref[0] = kwT[D_IDX:D_IDX + H_IDX]
    kwb_ref[0] = kwT.T.astype(BF16)
    lxg = jnp.dot(hn, wl_ref[...], preferred_element_type=F32)
    lx_ref[0] = lxg[:, :LRU_W]
    lg_ref[0] = lxg[:, LRU_W:]
    qm = jnp.dot(hn, wqm_ref[...], preferred_element_type=F32)
    qmb_ref[0] = _group_rms(qm, g128_ref[...], mg_ref[...], DH_MEM).astype(BF16)


def _proj_t(x, p, tm):
    b, s, _ = x.shape
    bc = lambda col: jnp.broadcast_to(col, (col.shape[0], tm))
    consts = [p["g"], p["wqkT"], p["wvT"], p["wqiT"], p["wkwT"], p["wl"], p["wqm"], bc(p["qg_col"]),
              bc(p["kg_col"]), p["mg"], bc(p["kwscale_col"]), p["g64"], p["g128"]]
    tok = lambda w: pl.BlockSpec((1, tm, w), lambda i, j: (i, j, 0))
    feat = lambda w: pl.BlockSpec((1, w, tm), lambda i, j: (i, 0, j))
    outs = [
        (feat(HD), (b, HD, s), F32), (feat(HD), (b, HD, s), F32), (feat(D_IDX), (b, D_IDX, s), F32),
        (feat(H_IDX), (b, H_IDX, s), F32), (feat(HD), (b, HD, s), BF16), (feat(HD), (b, HD, s), BF16),
        (tok(HD), (b, s, HD), BF16), (tok(LANES), (b, s, LANES), BF16),
        (pl.BlockSpec((1, tm // KB, HD, KB), lambda i, j: (i, j, 0, 0)), (b, s // KB, HD, KB), BF16),
        (tok(LRU_W), (b, s, LRU_W), F32), (tok(LRU_W), (b, s, LRU_W), F32), (tok(MD), (b, s, MD), BF16),
    ]
    return pl.pallas_call(
        _proj_t_kernel,
        grid=(b, s // tm),
        in_specs=[tok(D_MODEL)] + [_const_spec(c.shape) for c in consts],
        out_specs=[o[0] for o in outs],
        out_shape=[jax.ShapeDtypeStruct(o[1], o[2]) for o in outs],
        compiler_params=_params(2),
        name="proj_t",
    )(x, *consts)


def _proj(x, p, tm):
    n = x.shape[0]
    consts = [p[k] for k in ("g", "wqkv", "wqi", "wkw", "wl", "wqm", "qg", "kg", "mg", "kwscale", "g64", "g128")]

    def row(w):
        return pl.BlockSpec((tm, w), lambda i: (i, 0))

    outs = [(HD, F32), (HD, F32), (LANES, F32), (HD, BF16), (HD, BF16), (HD, BF16), (HD, BF16), (LANES, BF16),
            (LRU_W, F32), (LRU_W, F32), (MD, BF16)]
    return pl.pallas_call(
        _proj_kernel,
        grid=(n // tm,),
        in_specs=[row(D_MODEL)] + [_const_spec(c.shape) for c in consts],
        out_specs=[row(w) for w, _ in outs],
        out_shape=[jax.ShapeDtypeStruct((n, w), dt) for w, dt in outs],
        compiler_params=_params(1),
        name="proj",
    )(x, *consts)


def _memkv_kernel(m_ref, g_ref, w_ref, kg_ref, g128_ref, mk_ref, mv_ref):
    hn = _rms(m_ref[...], g_ref[...]).astype(BF16)
    kv = jnp.dot(hn, w_ref[...], preferred_element_type=F32)
    mk_ref[...] = _group_rms(kv[:, :MD], g128_ref[...], kg_ref[...], DH_MEM)
    mv_ref[...] = kv[:, MD:]


def _memkv(mem, norm_g, w_mem_kv, mem_k_norm_g, g128):
    n = mem.shape[0]
    tm = min(n, 512)
    consts = [norm_g.reshape(1, D_MODEL), w_mem_kv.astype(BF16), jnp.tile(mem_k_norm_g, H_MEM).reshape(1, MD), g128]
    row = lambda w: pl.BlockSpec((tm, w), lambda i: (i, 0))
    return pl.pallas_call(
        _memkv_kernel,
        grid=(n // tm,),
        in_specs=[row(D_MODEL)] + [_const_spec(c.shape) for c in consts],
        out_specs=[row(MD), row(MD)],
        out_shape=[jax.ShapeDtypeStruct((n, MD), F32)] * 2,
        compiler_params=_params(1),
        name="memkv",
    )(mem, *consts)


def _memattn_kernel(q_ref, mk_ref, mv_ref, o_ref):
    q = q_ref[0]
    mk = mk_ref[0].astype(BF16)
    mv = mv_ref[0].astype(BF16)
    for h in range(H_MEM):
        sl = slice(h * DH_MEM, (h + 1) * DH_MEM)
        s = lax.dot_general(q[:, sl], mk[:, sl], NT_DIMS, preferred_element_type=F32) * (DH_MEM ** -0.5)
        m = jnp.max(s, axis=-1, keepdims=True)
        e = jnp.exp(s - m)
        p = (e / jnp.sum(e, axis=-1, keepdims=True)).astype(BF16)
        o_ref[0, :, sl] = jnp.dot(p, mv[:, sl], preferred_element_type=F32).astype(BF16)


def _memattn(qm, mk, mv, tm):
    b, t, _ = qm.shape
    n_mem = mk.shape[1]
    return pl.pallas_call(
        _memattn_kernel,
        grid=(b, t // tm),
        in_specs=[pl.BlockSpec((1, tm, MD), lambda i, j: (i, j, 0)),
                  pl.BlockSpec((1, n_mem, MD), lambda i, j: (i, 0, 0)),
                  pl.BlockSpec((1, n_mem, MD), lambda i, j: (i, 0, 0))],
        out_specs=pl.BlockSpec((1, tm, MD), lambda i, j: (i, j, 0)),
        out_shape=jax.ShapeDtypeStruct((b, t, MD), BF16),
        compiler_params=_params(2),
        name="memattn",
    )(qm, mk, mv)


def _shift_rows(x, k, fill):
    rows = lax.broadcasted_iota(I32, x.shape, 0)
    return jnp.where(rows >= k, pltpu.roll(x, k, 0), fill)


def _rglru_kernel(lx_ref, lg_ref, cs_ref, h0_ref, cw_ref, cb_ref, wa_ref, ba_ref, wi_ref, bi_ref, lam_ref,
                  y_ref, nb_ref, hl_ref, tail_ref, h_ref, xe_ref):
    t = pl.program_id(1)

    @pl.when(t == 0)
    def _():
        tail_ref[...] = cs_ref[0]
        h_ref[...] = h0_ref[0]

    x = lx_ref[0]
    tt = x.shape[0]
    xe_ref[:SUBLANES] = tail_ref[...]
    xe_ref[SUBLANES:] = x
    conv = cb_ref[...] + x * cw_ref[CONV_W - 1:CONV_W, :]
    for d in range(1, CONV_W):
        conv = conv + xe_ref[SUBLANES - d:SUBLANES - d + tt, :] * cw_ref[CONV_W - 1 - d:CONV_W - d, :]
    tail_ref[...] = x[tt - SUBLANES:]
    nb_ref[0] = x[tt - SUBLANES:]

    cb16 = conv.astype(BF16)
    r = jax.nn.sigmoid(jnp.dot(cb16, wa_ref[...], preferred_element_type=F32) + ba_ref[...])
    ig = jax.nn.sigmoid(jnp.dot(cb16, wi_ref[...], preferred_element_type=F32) + bi_ref[...])
    nl = -lam_ref[...]
    softplus = jnp.maximum(nl, 0.0) + jnp.log1p(jnp.exp(-jnp.abs(nl)))
    log_a = -LRU_C * r * softplus
    a = jnp.exp(log_a)
    b = jnp.sqrt(-jnp.tanh(log_a) * (a * a + 1.0)) * (ig * conv)
    in_group = lax.broadcasted_iota(I32, x.shape, 0) % SUBLANES
    k = 1
    while k < SUBLANES:
        keep = in_group >= k
        b = a * jnp.where(keep, pltpu.roll(b, k, 0), 0.0) + b
        a = a * jnp.where(keep, pltpu.roll(a, k, 0), 1.0)
        k *= 2
    h_prev = h_ref[SUBLANES - 1:SUBLANES, :]
    groups = []
    for g in range(tt // SUBLANES):
        rows = slice(g * SUBLANES, (g + 1) * SUBLANES)
        groups.append(b[rows] + a[rows] * h_prev)
        h_prev = groups[-1][SUBLANES - 1:SUBLANES, :]
    h = groups[0] if len(groups) == 1 else jnp.concatenate(groups, axis=0)
    h_ref[...] = h[tt - SUBLANES:]
    hl_ref[0] = h[tt - SUBLANES:]
    y_ref[0] = (h * jax.nn.gelu(lg_ref[0])).astype(BF16)


def _block_diag(w):
    nb, bs, _ = w.shape
    eye = jnp.eye(nb, dtype=w.dtype)
    return (eye[:, None, :, None] * w[:, :, None, :]).reshape(nb * bs, nb * bs)


def _prep_rglru(conv_w, conv_b, lru_wa, lru_ba, lru_wi, lru_bi, lru_lambda):
    r = lambda v: v.reshape(1, LRU_W)
    return [conv_w, r(conv_b), _block_diag(lru_wa).astype(BF16), r(lru_ba), _block_diag(lru_wi).astype(BF16),
            r(lru_bi), r(lru_lambda)]


def _rglru(lx, lg, conv_state, h0, consts, tt):
    b, t, w = lx.shape
    cs = jnp.concatenate([jnp.zeros((b, SUBLANES - (CONV_W - 1), w), F32), conv_state], axis=1)
    h0p = jnp.concatenate([jnp.zeros((b, SUBLANES - 1, w), F32), h0[:, None, :]], axis=1)
    seq = pl.BlockSpec((1, tt, w), lambda i, j: (i, j, 0))
    st = pl.BlockSpec((1, SUBLANES, w), lambda i, j: (i, 0, 0))
    y, nb, hl = pl.pallas_call(
        _rglru_kernel,
        grid=(b, t // tt),
        in_specs=[seq, seq, st, st] + [_const_spec(c.shape) for c in consts],
        out_specs=[seq, st, st],
        out_shape=[jax.ShapeDtypeStruct((b, t, w), BF16), jax.ShapeDtypeStruct((b, SUBLANES, w), F32),
                   jax.ShapeDtypeStruct((b, SUBLANES, w), F32)],
        scratch_shapes=[pltpu.VMEM((SUBLANES, w), F32), pltpu.VMEM((SUBLANES, w), F32),
                        pltpu.VMEM((SUBLANES + tt, w), F32)],
        compiler_params=pltpu.CompilerParams(dimension_semantics=("parallel", "arbitrary"),
                                             vmem_limit_bytes=VMEM_LIMIT),
        name="rglru",
    )(lx, lg, cs, h0p, *consts)
    return y, nb[:, SUBLANES - (CONV_W - 1):], hl[:, SUBLANES - 1]


def _score_key(score):
    bits = lax.bitcast_convert_type(score, I32)
    return bits ^ ((bits >> 31) & 0x7FFFFFFF)


def _radix_threshold(count_ge, rows, topk):
    def bit_step(i, thr_u):
        cand_u = thr_u | (jnp.int32(1) << (31 - i))
        cnt = count_ge(cand_u ^ INT_MIN)
        return jnp.where(cnt >= topk, cand_u, thr_u)

    thr_u = lax.fori_loop(0, 32, bit_step, jnp.zeros((rows, 1), I32))
    return jnp.maximum(thr_u ^ INT_MIN, KEY_NEG_INF + 1)


def _dsa_prompt_kernel(qT_ref, qiT_ref, wT_ref, kw_ref, k_ref, vT_ref, tri_ref, o_ref,
                       sc_ref, qpad_ref, qipad_ref, oT_ref, s_scr, s2_scr, p_scr, *, topk):
    i = pl.program_id(1)
    nk = i + 1
    kf = float(topk)
    kblock = lambda j: pl.ds(pl.multiple_of(j * KB, KB), KB)

    zeros64 = jnp.zeros((D_IDX, TQ), BF16)
    for h in range(H_IDX):
        qipad_ref[h] = jnp.concatenate([qiT_ref[0, h * D_IDX:(h + 1) * D_IDX, :], zeros64], axis=0)
    for h in range(H_ATT):
        qh = qT_ref[0, h * DH_ATT:(h + 1) * DH_ATT, :]
        qpad_ref[h] = jnp.concatenate([qh, zeros64] if h % 2 == 0 else [zeros64, qh], axis=0)
    wT = wT_ref[0]

    def block_scores(j, n_blocks):
        rows = pl.ds(pl.multiple_of(j * KB, KB), n_blocks * KB)
        kw = kw_ref[0, rows, :]
        acc = jnp.zeros((n_blocks * KB, TQ), F32)
        for h in range(H_IDX):
            d = jnp.dot(kw, qipad_ref[h], preferred_element_type=F32)
            acc = acc + jnp.maximum(d, 0.0) * wT[h:h + 1, :]
        return rows, acc

    def score_blocks(n_blocks):
        def body(jj, carry):
            lo, hi = carry
            rows, acc = block_scores(jj * n_blocks, n_blocks)
            sc_ref[rows, :] = acc
            return (jnp.minimum(lo, jnp.min(acc, axis=0, keepdims=True)),
                    jnp.maximum(hi, jnp.max(acc, axis=0, keepdims=True)))
        return body

    init = (jnp.full((1, TQ), jnp.inf, F32), jnp.full((1, TQ), NEG_INF, F32))
    init = lax.fori_loop(0, i // 2, score_blocks(2), init)
    rmin, rmax = lax.fori_loop(2 * (i // 2), i, score_blocks(1), init)
    _, acc = block_scores(i, 1)
    krow = lax.broadcasted_iota(I32, (KB, TQ), 0)
    qcol = lax.broadcasted_iota(I32, (KB, TQ), 1)
    sc_ref[kblock(i), :] = jnp.where(krow <= qcol, acc, NEG_INF)
    rmin = jnp.minimum(rmin, jnp.min(acc, axis=0, keepdims=True))
    rmax = jnp.maximum(rmax, jnp.max(acc, axis=0, keepdims=True))

    def count(pred):
        def body(j, cnt):
            m = jnp.where(pred(sc_ref[kblock(j), :]), 1.0, 0.0)
            for r in range(KB // CNT_ROWS):
                cnt = cnt + m[r * CNT_ROWS:(r + 1) * CNT_ROWS]
            return cnt

        cnt = lax.fori_loop(0, nk, body, jnp.zeros((CNT_ROWS, TQ), F32))
        return jnp.sum(cnt, axis=0, keepdims=True)

    def n_open(done):
        return jnp.sum(1.0 - done).astype(I32)

    n_valid = i * TQ + lax.broadcasted_iota(I32, (1, TQ), 1) + 1
    done0 = jnp.where(n_valid <= topk, 1.0, 0.0)

    def bisect(_, c):
        lo, hi, done = c
        mid = 0.5 * lo + 0.5 * jnp.minimum(hi, rmax)
        c_mid = count(lambda x: x >= mid)
        ge = c_mid >= kf
        live = done < 0.5
        lo = jnp.where(live & ge, mid, lo)
        hi = jnp.where(live & jnp.logical_not(ge), mid, hi)
        done = jnp.maximum(done, jnp.where(c_mid == kf, 1.0, 0.0))
        return lo, hi, done

    lo, hi, done = lax.fori_loop(0, BISECT_STEPS, bisect, (rmin, jnp.full((1, TQ), jnp.inf, F32), done0))
    left = n_open(done)

    def step_down(c):
        lo, hi, done, _ = c

        def body(j, best):
            x = sc_ref[kblock(j), :]
            return jnp.maximum(best, jnp.max(jnp.where(x < hi, x, NEG_INF), axis=0, keepdims=True))

        cand = lax.fori_loop(0, nk, body, jnp.full((1, TQ), NEG_INF, F32))
        ok = count(lambda x: x >= cand) >= kf
        live = done < 0.5
        lo = jnp.where(live & ok, cand, lo)
        hi = jnp.where(live & jnp.logical_not(ok), cand, hi)
        done = jnp.maximum(done, jnp.where(ok, 1.0, 0.0))
        return lo, hi, done, n_open(done)

    thr, _, _, _ = lax.while_loop(lambda c: c[3] > 0, step_down, (lo, hi, done, left))
    n_tied = jnp.sum(jnp.where(count(lambda x: x >= thr) > kf, 1.0, 0.0)).astype(I32)

    @pl.when(n_tied == 0)
    def _():
        def body(j, carry):
            x = sc_ref[kblock(j), :]
            sc_ref[kblock(j), :] = jnp.where(x >= thr, 0.0, NEG_INF)
            return carry

        lax.fori_loop(0, nk, body, 0)

    @pl.when(n_tied > 0)
    def _():
        quota = kf - count(lambda x: x > thr)

        def body(j, ties_before):
            x = sc_ref[kblock(j), :]
            eq = jnp.where(x == thr, 1.0, 0.0)
            rank = jnp.dot(tri_ref[...], eq.astype(BF16), preferred_element_type=F32) + ties_before
            sel = (x > thr) | ((x == thr) & (rank < quota))
            sc_ref[kblock(j), :] = jnp.where(sel, 0.0, NEG_INF)
            return ties_before + jnp.sum(eq, axis=0, keepdims=True)

        lax.fori_loop(0, nk, body, jnp.zeros((1, TQ), F32))

    n_pairs = (nk + 1) // 2

    @pl.when(nk % 2 == 1)
    def _():
        sc_ref[kblock(nk), :] = jnp.full((KB, TQ), NEG_INF, F32)

    def score_phase(j, s_out):
        block_max = []
        for h in range(H_ATT):
            pair = slice((h // 2) * LANES, (h // 2 + 1) * LANES)
            s = jnp.dot(k_ref[0, kblock(j), pair], qpad_ref[h], preferred_element_type=F32)
            s = s + sc_ref[kblock(j), :]
            s_out[h] = s
            block_max.append(jnp.max(s, axis=0, keepdims=True))
        return tuple(block_max)

    ones_rows = (lax.broadcasted_iota(I32, (SUM_ROWS, KB), 0) == 0).astype(BF16)

    def block_step(j, next_j, s_in, s_out, carry):
        ms, accs, block_max = carry
        next_max = score_phase(next_j, s_out)
        new_ms, new_accs, alphas = [], [], []
        for h in range(H_ATT):
            m_new = jnp.maximum(ms[h], block_max[h])
            m_safe = jnp.where(m_new == NEG_INF, 0.0, m_new)
            alphas.append(jnp.exp2(ms[h] - m_safe))
            new_ms.append(m_new)
            p_scr[h] = jnp.exp2(s_in[h] - m_safe).astype(BF16)
        for h in range(H_ATT):
            hrows = slice(h * DH_ATT, (h + 1) * DH_ATT)
            v_aug = jnp.concatenate([vT_ref[0, j, hrows, :], ones_rows], axis=0)
            pv = jnp.dot(v_aug, p_scr[h], preferred_element_type=F32)
            new_accs.append(alphas[h] * accs[h] + pv)
        return tuple(new_ms), tuple(new_accs), next_max

    def attend_pair(jj, carry):
        j0 = 2 * jj
        carry = block_step(j0, j0 + 1, s_scr, s2_scr, carry)
        return block_step(j0 + 1, jnp.minimum(j0 + 2, 2 * n_pairs - 1), s2_scr, s_scr, carry)

    init = (tuple(jnp.full((1, TQ), NEG_INF, F32) for _ in range(H_ATT)),
            tuple(jnp.zeros((DH_ATT + SUM_ROWS, TQ), F32) for _ in range(H_ATT)),
            score_phase(0, s_scr))
    _, accs, _ = lax.fori_loop(0, n_pairs, attend_pair, init)
    for h in range(H_ATT):
        oT_ref[h * DH_ATT:(h + 1) * DH_ATT, :] = accs[h][:DH_ATT] / accs[h][DH_ATT:DH_ATT + 1]
    o_ref[0] = oT_ref[...].T.astype(BF16)


def _dsa_prompt(qT, qiT, wT, kwb, kb, vTb):
    b, _, s = qT.shape
    assert s % (2 * KB) == 0 and TQ == KB
    topk = min(TOPK_MAX, s // 4)
    tri = (jnp.arange(KB)[:, None] > jnp.arange(KB)[None, :]).astype(BF16)
    feat = lambda w: pl.BlockSpec((1, w, TQ), lambda i, j: (i, 0, j))
    full = lambda w: pl.BlockSpec((1, s, w), lambda i, j: (i, 0, 0))
    return pl.pallas_call(
        functools.partial(_dsa_prompt_kernel, topk=topk),
        grid=(b, s // TQ),
        in_specs=[feat(HD), feat(HD), feat(H_IDX), full(LANES), full(HD),
                  pl.BlockSpec((1, s // KB, HD, KB), lambda i, j: (i, 0, 0, 0)), _const_spec(tri.shape)],
        out_specs=pl.BlockSpec((1, TQ, HD), lambda i, j: (i, j, 0)),
        out_shape=jax.ShapeDtypeStruct((b, s, HD), BF16),
        scratch_shapes=[pltpu.VMEM((s, TQ), F32), pltpu.VMEM((H_ATT, LANES, TQ), BF16),
                        pltpu.VMEM((H_IDX, LANES, TQ), BF16), pltpu.VMEM((HD, TQ), F32),
                        pltpu.VMEM((H_ATT, KB, TQ), F32), pltpu.VMEM((H_ATT, KB, TQ), F32),
                        pltpu.VMEM((H_ATT, KB, TQ), BF16)],
        compiler_params=_params(2),
        name="dsa_prompt",
    )(qT, qiT, wT, kwb, kb, vTb, tri)


PAGE_GROUP_IDX = 32
PAGE_GROUP_KV = 32
COUNT_LANES = 8


def _dsa_sample_select_kernel(pt_ref, qi_ref, kwq_ref, tri_ref, *rest, n_pages, topk, group):
    page_refs, (bias_ref, biasn_ref, qiall_ref, wb_ref) = rest[:group], rest[group:]
    pg = pl.program_id(1)
    t = qi_ref.shape[1]
    kwq = kwq_ref[0]

    @pl.when(pg == 0)
    def _():
        qi = qi_ref[0].astype(F32)
        qiall_ref[...] = jnp.concatenate(
            [qi[:, h * D_IDX:(h + 1) * D_IDX] for h in range(H_IDX)], axis=0).astype(BF16)
        for h in range(H_IDX):
            wb_ref[h] = jnp.broadcast_to(kwq[:, D_IDX + h:D_IDX + h + 1], (t, LANES))

    def scores(dots):
        acc = jnp.zeros((t, dots.shape[1]), F32)
        for h in range(H_IDX):
            acc = acc + jnp.maximum(dots[h * t:(h + 1) * t], 0.0) * wb_ref[h]
        return acc

    for g in range(group):
        dots = jnp.dot(qiall_ref[...], page_refs[g][0].astype(BF16), preferred_element_type=F32)
        bias_ref[0, pg * group + g] = _score_key(scores(dots))

    @pl.when(pg == pl.num_programs(1) - 1)
    def _():
        new_keys = jnp.concatenate([kwq[:, :D_IDX], jnp.zeros((LANES - t, D_IDX), F32)], axis=0).astype(BF16)
        row = lax.broadcasted_iota(I32, (t, LANES), 0)
        col = lax.broadcasted_iota(I32, (t, LANES), 1)
        dots_new = lax.dot_general(qiall_ref[...], new_keys, NT_DIMS, preferred_element_type=F32)
        key_new = _score_key(jnp.where(col <= row, scores(dots_new), NEG_INF))

        def count(cmp_fn):
            parts = [cmp_fn(key_new).astype(I32)] + [jnp.zeros_like(key_new)] * (COUNT_LANES - 1)
            for p in range(n_pages):
                parts[p % COUNT_LANES] = parts[p % COUNT_LANES] + cmp_fn(bias_ref[0, p]).astype(I32)
            cnt = functools.reduce(lambda a, b: a + b, parts)
            return jnp.sum(cnt, axis=1, keepdims=True)

        thr = _radix_threshold(lambda c: count(lambda key: key >= c), t, topk)
        n_inexact = jnp.sum(jnp.where(count(lambda key: key >= thr) == topk, 0, 1))
        as_bias = lambda sel: lax.bitcast_convert_type(jnp.where(sel, 0.0, NEG_INF), I32)

        @pl.when(n_inexact == 0)
        def _():
            bias_ref[0] = as_bias(bias_ref[0] >= thr)
            biasn_ref[0] = as_bias(key_new >= thr)

        @pl.when(n_inexact > 0)
        def _():
            quota = (topk - count(lambda key: key > thr)).astype(F32)

            def select(key, ties_before):
                eq = key == thr
                eqf = eq.astype(F32)
                rank = jnp.dot(eqf.astype(BF16), tri_ref[...], preferred_element_type=F32) + ties_before
                sel = (key > thr) | (eq & (rank < quota))
                return as_bias(sel), ties_before + jnp.sum(eqf, axis=1, keepdims=True)

            def bias_page(p, ties_before):
                bias, ties = select(bias_ref[0, p], ties_before)
                bias_ref[0, p] = bias
                return ties

            ties = lax.fori_loop(0, n_pages, bias_page, jnp.zeros((t, 1), F32))
            biasn_ref[0], _ = select(key_new, ties)


def _dsa_sample_select(page_table, qib, kw32, cache_kidx_t):
    b, t, _ = qib.shape
    n_pages = page_table.shape[1]
    page = cache_kidx_t.shape[2]
    topk = min(TOPK_MAX, (n_pages * page + t) // 4)
    group = min(PAGE_GROUP_IDX, n_pages)
    assert n_pages % group == 0 and n_pages % COUNT_LANES == 0
    tri = (jnp.arange(page)[:, None] < jnp.arange(page)[None, :]).astype(BF16)
    tok = lambda w: pl.BlockSpec((1, t, w), lambda i, j, pt: (i, 0, 0))
    page_specs = [pl.BlockSpec((1, D_IDX, page), lambda i, j, pt, g=g: (pt[i, j * group + g], 0, 0))
                  for g in range(group)]
    grid_spec = pltpu.PrefetchScalarGridSpec(
        num_scalar_prefetch=1,
        grid=(b, n_pages // group),
        in_specs=[tok(H_IDX * D_IDX), tok(LANES), pl.BlockSpec(tri.shape, lambda i, j, pt: (0, 0))] + page_specs,
        out_specs=[pl.BlockSpec((1, n_pages, t, page), lambda i, j, pt: (i, 0, 0, 0)),
                   pl.BlockSpec((1, t, LANES), lambda i, j, pt: (i, 0, 0))],
        scratch_shapes=[pltpu.VMEM((H_IDX * t, D_IDX), BF16), pltpu.VMEM((H_IDX, t, LANES), F32)],
    )
    return pl.pallas_call(
        functools.partial(_dsa_sample_select_kernel, n_pages=n_pages, topk=topk, group=group),
        grid_spec=grid_spec,
        out_shape=[jax.ShapeDtypeStruct((b, n_pages, t, page), I32), jax.ShapeDtypeStruct((b, t, LANES), I32)],
        compiler_params=pltpu.CompilerParams(dimension_semantics=("parallel", "arbitrary"),
                                             vmem_limit_bytes=VMEM_LIMIT),
        name="dsa_sample_select",
    )(page_table, qib, kw32, tri, *([cache_kidx_t] * group))


def _dsa_sample_attend_kernel(pt_ref, q_ref, kn_ref, vn_ref, bias_ref, biasn_ref, *rest, group):
    kT_refs, vT_refs = rest[:group], rest[group:2 * group]
    o_ref, m_ref, l_ref, acc_ref = rest[2 * group:]
    pg = pl.program_id(1)
    t = q_ref.shape[1]
    page = kT_refs[0].shape[2]
    lane_head = lax.broadcasted_iota(I32, (t, HD), 1) // DH_ATT
    q = q_ref[0].astype(F32)
    qbd = jnp.concatenate([jnp.where(lane_head == h, q, 0.0) for h in range(H_ATT)], axis=0).astype(BF16)

    @pl.when(pg == 0)
    def _():
        m_ref[...] = jnp.full(m_ref.shape, NEG_INF, F32)
        l_ref[...] = jnp.zeros(l_ref.shape, F32)
        acc_ref[...] = jnp.zeros(acc_ref.shape, F32)

    def update(s, bias, pv):
        s = s + jnp.concatenate([lax.bitcast_convert_type(bias, F32)] * H_ATT, axis=0)
        m = m_ref[...]
        m_new = jnp.maximum(m, jnp.max(s, axis=1, keepdims=True))
        m_safe = jnp.where(m_new == NEG_INF, 0.0, m_new)
        alpha = jnp.exp(m - m_safe)
        p = jnp.exp(s - m_safe)
        l_ref[...] = alpha * l_ref[...] + jnp.sum(p, axis=1, keepdims=True)
        acc_ref[...] = alpha * acc_ref[...] + pv(p.astype(BF16))
        m_ref[...] = m_new

    s_pages = jnp.concatenate(
        [jnp.dot(qbd, kT_refs[g][0].astype(BF16), preferred_element_type=F32) for g in range(group)], axis=1)
    bias_pages = jnp.concatenate([bias_ref[0, g] for g in range(group)], axis=1)

    def pv_pages(p):
        out = jnp.zeros((H_ATT * t, HD), F32)
        for g in range(group):
            out = out + lax.dot_general(p[:, g * page:(g + 1) * page], vT_refs[g][0].astype(BF16), NT_DIMS,
                                        preferred_element_type=F32)
        return out

    update(s_pages, bias_pages, pv_pages)

    @pl.when(pg == pl.num_programs(1) - 1)
    def _():
        pad = jnp.zeros((LANES - t, HD), F32)
        kn = jnp.concatenate([kn_ref[0], pad], axis=0).astype(BF16)
        vn = jnp.concatenate([vn_ref[0], pad], axis=0).astype(BF16)
        update(lax.dot_general(qbd, kn, NT_DIMS, preferred_element_type=F32), biasn_ref[0],
               lambda p: jnp.dot(p, vn, preferred_element_type=F32))
        o = acc_ref[...] / l_ref[...]
        out = jnp.zeros((t, HD), F32)
        for h in range(H_ATT):
            out = out + jnp.where(lane_head == h, o[h * t:(h + 1) * t], 0.0)
        o_ref[0] = out.astype(BF16)


def _dsa_sample_attend(page_table, qb, k32, v32, bias, bias_new, cache_kt, cache_vt):
    b, t, _ = qb.shape
    n_pages = page_table.shape[1]
    page = cache_kt.shape[2]
    group = min(PAGE_GROUP_KV, n_pages)
    assert n_pages % group == 0
    tok = lambda w: pl.BlockSpec((1, t, w), lambda i, j, pt: (i, 0, 0))
    kv_specs = [pl.BlockSpec((1, HD, page), lambda i, j, pt, g=g: (pt[i, j * group + g], 0, 0))
                for g in range(group)]
    grid_spec = pltpu.PrefetchScalarGridSpec(
        num_scalar_prefetch=1,
        grid=(b, n_pages // group),
        in_specs=[tok(HD), tok(HD), tok(HD),
                  pl.BlockSpec((1, group, t, page), lambda i, j, pt: (i, j, 0, 0)),
                  tok(LANES)] + kv_specs + kv_specs,
        out_specs=tok(HD),
        scratch_shapes=[pltpu.VMEM((H_ATT * t, 1), F32), pltpu.VMEM((H_ATT * t, 1), F32),
                        pltpu.VMEM((H_ATT * t, HD), F32)],
    )
    return pl.pallas_call(
        functools.partial(_dsa_sample_attend_kernel, group=group),
        grid_spec=grid_spec,
        out_shape=jax.ShapeDtypeStruct((b, t, HD), BF16),
        compiler_params=pltpu.CompilerParams(dimension_semantics=("parallel", "arbitrary"),
                                             vmem_limit_bytes=VMEM_LIMIT),
        name="dsa_sample_attend",
    )(page_table, qb, k32, v32, bias, bias_new, *([cache_kt] * group), *([cache_vt] * group))


def _merge_kernel(x_ref, oa_ref, ol_ref, om_ref, g_ref, wg_ref, wa_ref, wl_ref, wm_ref, wo_ref, o_ref):
    x = x_ref[...]
    hn = _rms(x, g_ref[...]).astype(BF16)
    m = jnp.zeros_like(x)
    for idx, (o_r, w_r) in enumerate(((oa_ref, wa_ref), (ol_ref, wl_ref), (om_ref, wm_ref))):
        gate = jax.nn.sigmoid(jnp.dot(hn, wg_ref[:, idx * D_MODEL:(idx + 1) * D_MODEL], preferred_element_type=F32))
        m = m + gate * jnp.dot(o_r[...], w_r[...], preferred_element_type=F32)
    o_ref[...] = x + jnp.dot(m.astype(BF16), wo_ref[...], preferred_element_type=F32)


def _merge(x, o_att, o_lru, o_mem, consts, tm):
    n = x.shape[0]
    row = lambda w: pl.BlockSpec((tm, w), lambda i: (i, 0))
    return pl.pallas_call(
        _merge_kernel,
        grid=(n // tm,),
        in_specs=[row(D_MODEL), row(HD), row(LRU_W), row(MD)] + [_const_spec(c.shape) for c in consts],
        out_specs=row(D_MODEL),
        out_shape=jax.ShapeDtypeStruct((n, D_MODEL), F32),
        compiler_params=_params(1),
        name="merge",
    )(x, o_att, o_lru, o_mem, *consts)


def _token_tile(n):
    return min(n, 512)


def _layer(x, is_prompt, lw, pp, extra):
    b, t, _ = x.shape
    n = b * t
    tm = _token_tile(n)
    x1 = _ffn(x.reshape(n, D_MODEL), pp["ffn1"], tm)
    r3 = lambda a: a.reshape(b, t, a.shape[-1])
    if is_prompt:
        kT32, vT32, kiT32, wT, qT, qiT, kb, kwb, vTb, lx, lg, qmb = _proj_t(r3(x1), pp["proj"], tm)
        o_att = _dsa_prompt(qT, qiT, wT, kwb, kb, vTb)
        k_new = kT32.reshape(b, H_ATT, DH_ATT, t).transpose(0, 3, 1, 2)
        v_new = vT32.reshape(b, H_ATT, DH_ATT, t).transpose(0, 3, 1, 2)
        ki_new = kiT32.transpose(0, 2, 1)
        conv_state = jnp.zeros((b, CONV_W - 1, LRU_W), F32)
        h0 = jnp.zeros((b, LRU_W), F32)
        mem = extra["mem"]
        mk, mv = _memkv(mem.reshape(-1, D_MODEL), lw["norm_mem_g"], lw["w_mem_kv"], lw["mem_k_norm_g"],
                        pp["proj"]["g128"])
        mk = mk.reshape(b, -1, MD)
        mv = mv.reshape(b, -1, MD)
    else:
        k32, v32, kw32, qb, _, _, qib, _, lx, lg, qmb = _proj(x1, pp["proj"], tm)
        pt = extra["page_table"]
        bias, bias_new = _dsa_sample_select(pt, r3(qib), r3(kw32), extra["cache_kidx_t"])
        o_att = _dsa_sample_attend(pt, r3(qb), r3(k32), r3(v32), bias, bias_new, extra["cache_kt"],
                                   extra["cache_vt"])
        k_new = k32.reshape(b, t, H_ATT, DH_ATT)
        v_new = v32.reshape(b, t, H_ATT, DH_ATT)
        ki_new = r3(kw32)[:, :, :D_IDX]
        conv_state, h0 = extra["state_conv"], extra["state_h"]
        mk, mv = extra["cache_mem_k"], extra["cache_mem_v"]
    o_lru, conv_buf, h_last = _rglru(r3(lx), r3(lg), conv_state, h0, pp["rglru"], min(t, 256))
    o_mem = _memattn(r3(qmb), mk, mv, min(t, 512))
    x2 = _merge(x1, o_att.reshape(n, HD), o_lru.reshape(n, LRU_W), o_mem.reshape(n, MD), pp["merge"], tm)
    y = _ffn(x2, pp["ffn2"], tm).reshape(b, t, D_MODEL)
    if is_prompt:
        state = (k_new, v_new, ki_new, mk.reshape(b, -1, H_MEM, DH_MEM), mv.reshape(b, -1, H_MEM, DH_MEM),
                 conv_buf, h_last)
    else:
        state = (k_new, v_new, ki_new, conv_buf, h_last)
    return y, state


def kernel(x_prompt, x_sample, cache_k, cache_v, cache_kidx, page_table, cache_mem_k, cache_mem_v, state_conv, state_h, mem_prompt, norm_ffn1_g, w_ffn1_in, w_ffn1_out, norm_mix_g, w_in, q_norm_g, k_norm_g, w_attn_o, conv_w, conv_b, lru_wa, lru_ba, lru_wi, lru_bi, lru_lambda, w_lru_o, norm_mem_g, w_mem_kv, mem_q_norm_g, mem_k_norm_g, w_mem_o, w_out, norm_ffn2_g, w_ffn2_in, w_ffn2_out):
    depth = w_in.shape[0]
    n_phys, page = cache_k.shape[1], cache_k.shape[2]
    xp, xs = x_prompt, x_sample
    p_states, s_states = [], []
    for l in range(depth):
        proj = _prep_proj(norm_mix_g[l], w_in[l], q_norm_g[l], k_norm_g[l], mem_q_norm_g[l])
        pp = dict(
            ffn1=_prep_ffn(norm_ffn1_g[l], w_ffn1_in[l], w_ffn1_out[l]),
            ffn2=_prep_ffn(norm_ffn2_g[l], w_ffn2_in[l], w_ffn2_out[l]),
            proj=proj,
            rglru=_prep_rglru(conv_w[l], conv_b[l], lru_wa[l], lru_ba[l], lru_wi[l], lru_bi[l], lru_lambda[l]),
            merge=[proj["g"], proj["wgates"], w_attn_o[l].astype(BF16), w_lru_o[l].astype(BF16),
                   w_mem_o[l].astype(BF16), w_out[l].astype(BF16)],
        )
        lw = dict(norm_mem_g=norm_mem_g[l], w_mem_kv=w_mem_kv[l], mem_k_norm_g=mem_k_norm_g[l])
        xp, st_p = _layer(xp, True, lw, pp, dict(mem=mem_prompt))
        xs, st_s = _layer(xs, False, lw, pp, dict(
            page_table=page_table,
            cache_kt=cache_k[l].transpose(0, 2, 3, 1).reshape(n_phys, HD, page),
            cache_vt=cache_v[l].transpose(0, 2, 3, 1).reshape(n_phys, HD, page),
            cache_kidx_t=cache_kidx[l].transpose(0, 2, 1), cache_mem_k=cache_mem_k[l].reshape(-1, cache_mem_k.shape[2], MD),
            cache_mem_v=cache_mem_v[l].reshape(-1, cache_mem_v.shape[2], MD),
            state_conv=state_conv[l], state_h=state_h[l]))
        p_states.append(st_p)
        s_states.append(st_s)
    stack = lambda states, i: jnp.stack([s[i] for s in states])
    return (xp, xs) + tuple(stack(p_states, i) for i in range(7)) + tuple(stack(s_states, i) for i in range(5))
```

```python
import functools

import jax
import jax.numpy as jnp
from jax import lax
from jax.experimental import pallas as pl
from jax.experimental.pallas import tpu as pltpu

F32, BF16, I32 = jnp.float32, jnp.bfloat16, jnp.int32

D_MODEL = 1024
H_ATT, DH_ATT = 8, 64
H_IDX, D_IDX = 8, 64
TOPK_MAX = 256
LRU_W, LRU_BLOCKS, CONV_W, LRU_C = 512, 8, 4, 8.0
H_MEM, DH_MEM = 4, 128
D_FF = 2816
EPS = 1e-6
HD = H_ATT * DH_ATT
MD = H_MEM * DH_MEM

LANES = 128
SUBLANES = 8
VMEM_BYTES_V7X = 64 * 1024 * 1024
VMEM_LIMIT = VMEM_BYTES_V7X - 8 * 1024 * 1024

FF_CHUNK = 256
TQ = 256
KB = 256
SCORE_BLOCKS = (4, 2, 1)
CNT_ROWS = 32
SUM_ROWS = 16
BISECT_STEPS = 15
LOG2E = 1.4426950408889634
INT_MIN = -2 ** 31
KEY_NEG_INF = -2139095041
NEG_INF = float("-inf")

NT_DIMS = (((1,), (1,)), ((), ()))


def _params(n_grid, parallel=True):
    sem = ("parallel" if parallel else "arbitrary",) * n_grid
    return pltpu.CompilerParams(dimension_semantics=sem, vmem_limit_bytes=VMEM_LIMIT)


def _const_spec(shape):
    nd = len(shape)
    return pl.BlockSpec(shape, lambda *_: (0,) * nd)


def _rms(x, g):
    ms = jnp.mean(x * x, axis=-1, keepdims=True)
    return x * lax.rsqrt(ms + EPS) * g


def _group_rms(x, gmat, g, group):
    x2 = x * x
    hi = x2.astype(BF16)
    lo = (x2 - hi.astype(F32)).astype(BF16)
    ss = jnp.dot(hi, gmat, preferred_element_type=F32) + jnp.dot(lo, gmat, preferred_element_type=F32)
    return x * lax.rsqrt(ss * (1.0 / group) + EPS) * g


def _group_matrix(width, group):
    idx = jnp.arange(width) // group
    return (idx[:, None] == idx[None, :]).astype(BF16)


def _ffn_kernel(x_ref, g_ref, wg_ref, wu_ref, wo_ref, o_ref):
    x = x_ref[...]
    hn = _rms(x, g_ref[...]).astype(BF16)
    acc = jnp.zeros_like(x)
    for c in range(wg_ref.shape[0]):
        gate = jnp.dot(hn, wg_ref[c], preferred_element_type=F32)
        up = jnp.dot(hn, wu_ref[c], preferred_element_type=F32)
        act = (gate * jax.nn.sigmoid(gate) * up).astype(BF16)
        acc = acc + jnp.dot(act, wo_ref[c], preferred_element_type=F32)
    o_ref[...] = x + 0.5 * acc


def _prep_ffn(g, w_in, w_out):
    nc = D_FF // FF_CHUNK
    wg = w_in[:, :D_FF].reshape(D_MODEL, nc, FF_CHUNK).transpose(1, 0, 2).astype(BF16)
    wu = w_in[:, D_FF:].reshape(D_MODEL, nc, FF_CHUNK).transpose(1, 0, 2).astype(BF16)
    wo = w_out.reshape(nc, FF_CHUNK, D_MODEL).astype(BF16)
    return g.reshape(1, D_MODEL), wg, wu, wo


def _ffn(x, prep, tm):
    g, wg, wu, wo = prep
    n = x.shape[0]
    row = pl.BlockSpec((tm, D_MODEL), lambda i: (i, 0))
    return pl.pallas_call(
        _ffn_kernel,
        grid=(n // tm,),
        in_specs=[row, _const_spec(g.shape), _const_spec(wg.shape), _const_spec(wu.shape), _const_spec(wo.shape)],
        out_specs=row,
        out_shape=jax.ShapeDtypeStruct((n, D_MODEL), F32),
        compiler_params=_params(1),
        name="ffn",
    )(x, g, wg, wu, wo)


def _proj_kernel(x_ref, g_ref, wqkv_ref, wqi_ref, wkw_ref, wl_ref, wqm_ref, qg_ref, kg_ref, mg_ref,
                 kwscale_ref, g64_ref, g128_ref,
                 k32_ref, v32_ref, kw32_ref, qb_ref, kb_ref, vb_ref, qib_ref, kwb_ref, lx_ref, lg_ref, qmb_ref):
    hn = _rms(x_ref[...], g_ref[...]).astype(BF16)
    qkv = jnp.dot(hn, wqkv_ref[...], preferred_element_type=F32)
    q = _group_rms(qkv[:, :HD], g64_ref[...], qg_ref[...], DH_ATT)
    k = _group_rms(qkv[:, HD:2 * HD], g64_ref[...], kg_ref[...], DH_ATT)
    v = qkv[:, 2 * HD:]
    k32_ref[...] = k
    v32_ref[...] = v
    qb_ref[...] = (q * (DH_ATT ** -0.5)).astype(BF16)
    kb_ref[...] = k.astype(BF16)
    vb_ref[...] = v.astype(BF16)
    qib_ref[...] = jnp.dot(hn, wqi_ref[...], preferred_element_type=F32).astype(BF16)
    kw = jnp.dot(hn, wkw_ref[...], preferred_element_type=F32) * kwscale_ref[...]
    kw32_ref[...] = kw
    kwb_ref[...] = kw.astype(BF16)
    lxg = jnp.dot(hn, wl_ref[...], preferred_element_type=F32)
    lx_ref[...] = lxg[:, :LRU_W]
    lg_ref[...] = lxg[:, LRU_W:]
    qm = jnp.dot(hn, wqm_ref[...], preferred_element_type=F32)
    qmb_ref[...] = _group_rms(qm, g128_ref[...], mg_ref[...], DH_MEM).astype(BF16)


def _prep_proj(norm_g, w_in, q_norm_g, k_norm_g, mem_q_norm_g):
    o = 0
    cols = {}
    for name, size in (("q", HD), ("k", HD), ("v", HD), ("qi", H_IDX * D_IDX), ("ki", D_IDX), ("wi", H_IDX),
                       ("lx", LRU_W), ("lg", LRU_W), ("qm", MD), ("gates", 3 * D_MODEL)):
        cols[name] = w_in[:, o:o + size]
        o += size
    pad = jnp.zeros((D_MODEL, LANES - D_IDX - H_IDX), w_in.dtype)
    wqkv = jnp.concatenate([cols["q"], cols["k"], cols["v"]], axis=1).astype(BF16)
    wkw = jnp.concatenate([cols["ki"], cols["wi"], pad], axis=1).astype(BF16)
    wl = jnp.concatenate([cols["lx"], cols["lg"]], axis=1).astype(BF16)
    kwscale = jnp.concatenate([jnp.ones((D_IDX,), F32),
                               jnp.full((H_IDX,), H_IDX ** -0.5 * D_IDX ** -0.5, F32),
                               jnp.zeros((LANES - D_IDX - H_IDX,), F32)]).reshape(1, LANES)
    return dict(
        g=norm_g.reshape(1, D_MODEL), wqkv=wqkv, wqi=cols["qi"].astype(BF16), wkw=wkw, wl=wl,
        wqm=cols["qm"].astype(BF16),
        qg=jnp.tile(q_norm_g, H_ATT).reshape(1, HD), kg=jnp.tile(k_norm_g, H_ATT).reshape(1, HD),
        mg=jnp.tile(mem_q_norm_g, H_MEM).reshape(1, MD), kwscale=kwscale,
        g64=_group_matrix(HD, DH_ATT), g128=_group_matrix(MD, DH_MEM),
        wgates=cols["gates"].astype(BF16),
        wqkT=wqkv[:, :2 * HD].T, wvT=wqkv[:, 2 * HD:].T, wqiT=cols["qi"].astype(BF16).T, wkwT=wkw.T,
        qg_col=jnp.tile(q_norm_g, H_ATT).reshape(HD, 1), kg_col=jnp.tile(k_norm_g, H_ATT).reshape(HD, 1),
        kwscale_col=kwscale.reshape(LANES, 1),
    )


def _group_rms_t(xt, gmat, g, group):
    x2 = xt * xt
    hi = x2.astype(BF16)
    lo = (x2 - hi.astype(F32)).astype(BF16)
    ss = jnp.dot(gmat, hi, preferred_element_type=F32) + jnp.dot(gmat, lo, preferred_element_type=F32)
    return xt * lax.rsqrt(ss * (1.0 / group) + EPS) * g


def _proj_t_kernel(x_ref, g_ref, wqkT_ref, wvT_ref, wqiT_ref, wkwT_ref, wl_ref, wqm_ref, qgT_ref, kgT_ref, mg_ref,
                   kwscaleT_ref, g64_ref, g128_ref,
                   kT_ref, vT_ref, kiT_ref, wT_ref, qT_ref, qiT_ref, kb_ref, kwb_ref, vTb_ref, lx_ref, lg_ref,
                   qmb_ref):
    hn = _rms(x_ref[0], g_ref[...]).astype(BF16)
    nt = lambda w_ref: lax.dot_general(w_ref[...], hn, NT_DIMS, preferred_element_type=F32)
    qkT = nt(wqkT_ref)
    qT = _group_rms_t(qkT[:HD], g64_ref[...], qgT_ref[...], DH_ATT)
    kT = _group_rms_t(qkT[HD:], g64_ref[...], kgT_ref[...], DH_ATT)
    kT_ref[0] = kT
    kb_ref[0] = kT.T.astype(BF16)
    qT_ref[0] = (qT * (DH_ATT ** -0.5 * LOG2E)).astype(BF16)
    vT = nt(wvT_ref)
    vT_ref[0] = vT
    for c in range(vTb_ref.shape[1]):
        vTb_ref[0, c] = vT[:, c * KB:(c + 1) * KB].astype(BF16)
    qiT_ref[0] = nt(wqiT_ref).astype(BF16)
    kwT = nt(wkwT_ref) * kwscaleT_ref[...]
    kiT_ref[0] = kwT[:D_IDX]
    wT_ref[0] = kwT[D_IDX:D_IDX + H_IDX]
    kwb_ref[0] = kwT.T.astype(BF16)
    lxg = jnp.dot(hn, wl_ref[...], preferred_element_type=F32)
    lx_ref[0] = lxg[:, :LRU_W]
    lg_ref[0] = lxg[:, LRU_W:]
    qm = jnp.dot(hn, wqm_ref[...], preferred_element_type=F32)
    qmb_ref[0] = _group_rms(qm, g128_ref[...], mg_ref[...], DH_MEM).astype(BF16)


def _proj_t(x, p, tm):
    b, s, _ = x.shape
    bc = lambda col: jnp.broadcast_to(col, (col.shape[0], tm))
    consts = [p["g"], p["wqkT"], p["wvT"], p["wqiT"], p["wkwT"], p["wl"], p["wqm"], bc(p["qg_col"]),
              bc(p["kg_col"]), p["mg"], bc(p["kwscale_col"]), p["g64"], p["g128"]]
    tok = lambda w: pl.BlockSpec((1, tm, w), lambda i, j: (i, j, 0))
    feat = lambda w: pl.BlockSpec((1, w, tm), lambda i, j: (i, 0, j))
    outs = [
        (feat(HD), (b, HD, s), F32), (feat(HD), (b, HD, s), F32), (feat(D_IDX), (b, D_IDX, s), F32),
        (feat(H_IDX), (b, H_IDX, s), F32), (feat(HD), (b, HD, s), BF16), (feat(HD), (b, HD, s), BF16),
        (tok(HD), (b, s, HD), BF16), (tok(LANES), (b, s, LANES), BF16),
        (pl.BlockSpec((1, tm // KB, HD, KB), lambda i, j: (i, j, 0, 0)), (b, s // KB, HD, KB), BF16),
        (tok(LRU_W), (b, s, LRU_W), F32), (tok(LRU_W), (b, s, LRU_W), F32), (tok(MD), (b, s, MD), BF16),
    ]
    return pl.pallas_call(
        _proj_t_kernel,
        grid=(b, s // tm),
        in_specs=[tok(D_MODEL)] + [_const_spec(c.shape) for c in consts],
        out_specs=[o[0] for o in outs],
        out_shape=[jax.ShapeDtypeStruct(o[1], o[2]) for o in outs],
        compiler_params=_params(2),
        name="proj_t",
    )(x, *consts)


def _proj(x, p, tm):
    n = x.shape[0]
    consts = [p[k] for k in ("g", "wqkv", "wqi", "wkw", "wl", "wqm", "qg", "kg", "mg", "kwscale", "g64", "g128")]

    def row(w):
        return pl.BlockSpec((tm, w), lambda i: (i, 0))

    outs = [(HD, F32), (HD, F32), (LANES, F32), (HD, BF16), (HD, BF16), (HD, BF16), (HD, BF16), (LANES, BF16),
            (LRU_W, F32), (LRU_W, F32), (MD, BF16)]
    return pl.pallas_call(
        _proj_kernel,
        grid=(n // tm,),
        in_specs=[row(D_MODEL)] + [_const_spec(c.shape) for c in consts],
        out_specs=[row(w) for w, _ in outs],
        out_shape=[jax.ShapeDtypeStruct((n, w), dt) for w, dt in outs],
        compiler_params=_params(1),
        name="proj",
    )(x, *consts)


def _memkv_kernel(m_ref, g_ref, w_ref, kg_ref, g128_ref, mk_ref, mv_ref):
    hn = _rms(m_ref[...], g_ref[...]).astype(BF16)
    kv = jnp.dot(hn, w_ref[...], preferred_element_type=F32)
    mk_ref[...] = _group_rms(kv[:, :MD], g128_ref[...], kg_ref[...], DH_MEM)
    mv_ref[...] = kv[:, MD:]


def _memkv(mem, norm_g, w_mem_kv, mem_k_norm_g, g128):
    n = mem.shape[0]
    tm = min(n, 512)
    consts = [norm_g.reshape(1, D_MODEL), w_mem_kv.astype(BF16), jnp.tile(mem_k_norm_g, H_MEM).reshape(1, MD), g128]
    row = lambda w: pl.BlockSpec((tm, w), lambda i: (i, 0))
    return pl.pallas_call(
        _memkv_kernel,
        grid=(n // tm,),
        in_specs=[row(D_MODEL)] + [_const_spec(c.shape) for c in consts],
        out_specs=[row(MD), row(MD)],
        out_shape=[jax.ShapeDtypeStruct((n, MD), F32)] * 2,
        compiler_params=_params(1),
        name="memkv",
    )(mem, *consts)


def _memattn_kernel(q_ref, mk_ref, mv_ref, o_ref):
    q = q_ref[0]
    mk = mk_ref[0].astype(BF16)
    mv = mv_ref[0].astype(BF16)
    for h in range(H_MEM):
        sl = slice(h * DH_MEM, (h + 1) * DH_MEM)
        s = lax.dot_general(q[:, sl], mk[:, sl], NT_DIMS, preferred_element_type=F32) * (DH_MEM ** -0.5)
        m = jnp.max(s, axis=-1, keepdims=True)
        e = jnp.exp(s - m)
        p = (e / jnp.sum(e, axis=-1, keepdims=True)).astype(BF16)
        o_ref[0, :, sl] = jnp.dot(p, mv[:, sl], preferred_element_type=F32).astype(BF16)


def _memattn(qm, mk, mv, tm):
    b, t, _ = qm.shape
    n_mem = mk.shape[1]
    return pl.pallas_call(
        _memattn_kernel,
        grid=(b, t // tm),
        in_specs=[pl.BlockSpec((1, tm, MD), lambda i, j: (i, j, 0)),
                  pl.BlockSpec((1, n_mem, MD), lambda i, j: (i, 0, 0)),
                  pl.BlockSpec((1, n_mem, MD), lambda i, j: (i, 0, 0))],
        out_specs=pl.BlockSpec((1, tm, MD), lambda i, j: (i, j, 0)),
        out_shape=jax.ShapeDtypeStruct((b, t, MD), BF16),
        compiler_params=_params(2),
        name="memattn",
    )(qm, mk, mv)


def _shift_rows(x, k, fill):
    rows = lax.broadcasted_iota(I32, x.shape, 0)
    return jnp.where(rows >= k, pltpu.roll(x, k, 0), fill)


def _rglru_kernel(lx_ref, lg_ref, cs_ref, h0_ref, cw_ref, cb_ref, wa_ref, ba_ref, wi_ref, bi_ref, lam_ref,
                  y_ref, nb_ref, hl_ref, tail_ref, h_ref, xe_ref):
    t = pl.program_id(1)

    @pl.when(t == 0)
    def _():
        tail_ref[...] = cs_ref[0]
        h_ref[...] = h0_ref[0]

    x = lx_ref[0]
    tt = x.shape[0]
    xe_ref[:SUBLANES] = tail_ref[...]
    xe_ref[SUBLANES:] = x
    conv = cb_ref[...] + x * cw_ref[CONV_W - 1:CONV_W, :]
    for d in range(1, CONV_W):
        conv = conv + xe_ref[SUBLANES - d:SUBLANES - d + tt, :] * cw_ref[CONV_W - 1 - d:CONV_W - d, :]
    tail_ref[...] = x[tt - SUBLANES:]
    nb_ref[0] = x[tt - SUBLANES:]

    cb16 = conv.astype(BF16)
    r = jax.nn.sigmoid(jnp.dot(cb16, wa_ref[...], preferred_element_type=F32) + ba_ref[...])
    ig = jax.nn.sigmoid(jnp.dot(cb16, wi_ref[...], preferred_element_type=F32) + bi_ref[...])
    nl = -lam_ref[...]
    softplus = jnp.maximum(nl, 0.0) + jnp.log1p(jnp.exp(-jnp.abs(nl)))
    log_a = -LRU_C * r * softplus
    a = jnp.exp(log_a)
    b = jnp.sqrt(-jnp.tanh(log_a) * (a * a + 1.0)) * (ig * conv)
    in_group = lax.broadcasted_iota(I32, x.shape, 0) % SUBLANES
    k = 1
    while k < SUBLANES:
        keep = in_group >= k
        b = a * jnp.where(keep, pltpu.roll(b, k, 0), 0.0) + b
        a = a * jnp.where(keep, pltpu.roll(a, k, 0), 1.0)
        k *= 2
    h_prev = h_ref[SUBLANES - 1:SUBLANES, :]
    groups = []
    for g in range(tt // SUBLANES):
        rows = slice(g * SUBLANES, (g + 1) * SUBLANES)
        groups.append(b[rows] + a[rows] * h_prev)
        h_prev = groups[-1][SUBLANES - 1:SUBLANES, :]
    h = groups[0] if len(groups) == 1 else jnp.concatenate(groups, axis=0)
    h_ref[...] = h[tt - SUBLANES:]
    hl_ref[0] = h[tt - SUBLANES:]
    y_ref[0] = (h * jax.nn.gelu(lg_ref[0])).astype(BF16)


def _block_diag(w):
    nb, bs, _ = w.shape
    eye = jnp.eye(nb, dtype=w.dtype)
    return (eye[:, None, :, None] * w[:, :, None, :]).reshape(nb * bs, nb * bs)


def _prep_rglru(conv_w, conv_b, lru_wa, lru_ba, lru_wi, lru_bi, lru_lambda):
    r = lambda v: v.reshape(1, LRU_W)
    return [conv_w, r(conv_b), _block_diag(lru_wa).astype(BF16), r(lru_ba), _block_diag(lru_wi).astype(BF16),
            r(lru_bi), r(lru_lambda)]


def _rglru(lx, lg, conv_state, h0, consts, tt):
    b, t, w = lx.shape
    cs = jnp.concatenate([jnp.zeros((b, SUBLANES - (CONV_W - 1), w), F32), conv_state], axis=1)
    h0p = jnp.concatenate([jnp.zeros((b, SUBLANES - 1, w), F32), h0[:, None, :]], axis=1)
    seq = pl.BlockSpec((1, tt, w), lambda i, j: (i, j, 0))
    st = pl.BlockSpec((1, SUBLANES, w), lambda i, j: (i, 0, 0))
    y, nb, hl = pl.pallas_call(
        _rglru_kernel,
        grid=(b, t // tt),
        in_specs=[seq, seq, st, st] + [_const_spec(c.shape) for c in consts],
        out_specs=[seq, st, st],
        out_shape=[jax.ShapeDtypeStruct((b, t, w), BF16), jax.ShapeDtypeStruct((b, SUBLANES, w), F32),
                   jax.ShapeDtypeStruct((b, SUBLANES, w), F32)],
        scratch_shapes=[pltpu.VMEM((SUBLANES, w), F32), pltpu.VMEM((SUBLANES, w), F32),
                        pltpu.VMEM((SUBLANES + tt, w), F32)],
        compiler_params=pltpu.CompilerParams(dimension_semantics=("parallel", "arbitrary"),
                                             vmem_limit_bytes=VMEM_LIMIT),
        name="rglru",
    )(lx, lg, cs, h0p, *consts)
    return y, nb[:, SUBLANES - (CONV_W - 1):], hl[:, SUBLANES - 1]


def _score_key(score):
    bits = lax.bitcast_convert_type(score, I32)
    return bits ^ ((bits >> 31) & 0x7FFFFFFF)


def _radix_threshold(count_ge, rows, topk):
    def bit_step(i, thr_u):
        cand_u = thr_u | (jnp.int32(1) << (31 - i))
        cnt = count_ge(cand_u ^ INT_MIN)
        return jnp.where(cnt >= topk, cand_u, thr_u)

    thr_u = lax.fori_loop(0, 32, bit_step, jnp.zeros((rows, 1), I32))
    return jnp.maximum(thr_u ^ INT_MIN, KEY_NEG_INF + 1)


def _dsa_prompt_kernel(qT_ref, qiT_ref, wT_ref, kw_ref, k_ref, vT_ref, tri_ref, o_ref,
                       sc_ref, qpad_ref, qipad_ref, oT_ref, s_scr, s2_scr, p_scr, p2_scr, *, topk):
    i = pl.program_id(1)
    nk = i + 1
    kf = float(topk)
    kblock = lambda j: pl.ds(pl.multiple_of(j * KB, KB), KB)

    zeros64 = jnp.zeros((D_IDX, TQ), BF16)
    for h in range(H_IDX):
        qipad_ref[h] = jnp.concatenate([qiT_ref[0, h * D_IDX:(h + 1) * D_IDX, :], zeros64], axis=0)
    for h in range(H_ATT):
        qh = qT_ref[0, h * DH_ATT:(h + 1) * DH_ATT, :]
        qpad_ref[h] = jnp.concatenate([qh, zeros64] if h % 2 == 0 else [zeros64, qh], axis=0)
    wT = wT_ref[0]

    def block_scores(j, n_blocks):
        rows = pl.ds(pl.multiple_of(j * KB, KB), n_blocks * KB)
        kw = kw_ref[0, rows, :]
        acc = jnp.zeros((n_blocks * KB, TQ), F32)
        for h in range(H_IDX):
            d = jnp.dot(kw, qipad_ref[h], preferred_element_type=F32)
            acc = acc + jnp.maximum(d, 0.0) * wT[h:h + 1, :]
        return rows, acc

    def score_blocks(n_blocks):
        def body(jj, carry):
            lo, hi = carry
            rows, acc = block_scores(jj * n_blocks, n_blocks)
            sc_ref[rows, :] = acc
            return (jnp.minimum(lo, jnp.min(acc, axis=0, keepdims=True)),
                    jnp.maximum(hi, jnp.max(acc, axis=0, keepdims=True)))
        return body

    bounds = (jnp.full((1, TQ), jnp.inf, F32), jnp.full((1, TQ), NEG_INF, F32))
    first = 0
    for n_blocks in SCORE_BLOCKS:
        trips = (i - first) // n_blocks
        bounds = lax.fori_loop(first // n_blocks, first // n_blocks + trips, score_blocks(n_blocks), bounds)
        first = first + trips * n_blocks
    rmin, rmax = bounds
    _, acc = block_scores(i, 1)
    krow = lax.broadcasted_iota(I32, (KB, TQ), 0)
    qcol = lax.broadcasted_iota(I32, (KB, TQ), 1)
    sc_ref[kblock(i), :] = jnp.where(krow <= qcol, acc, NEG_INF)
    rmin = jnp.minimum(rmin, jnp.min(acc, axis=0, keepdims=True))
    rmax = jnp.maximum(rmax, jnp.max(acc, axis=0, keepdims=True))

    def count(pred):
        def body(j, cnt):
            m = jnp.where(pred(sc_ref[kblock(j), :]), 1.0, 0.0)
            for r in range(KB // CNT_ROWS):
                cnt = cnt + m[r * CNT_ROWS:(r + 1) * CNT_ROWS]
            return cnt

        cnt = lax.fori_loop(0, nk, body, jnp.zeros((CNT_ROWS, TQ), F32))
        return jnp.sum(cnt, axis=0, keepdims=True)

    def n_open(done):
        return jnp.sum(1.0 - done).astype(I32)

    n_valid = i * TQ + lax.broadcasted_iota(I32, (1, TQ), 1) + 1
    done0 = jnp.where(n_valid <= topk, 1.0, 0.0)

    def bisect(_, c):
        lo, hi, done = c
        mid = 0.5 * lo + 0.5 * jnp.minimum(hi, rmax)
        c_mid = count(lambda x: x >= mid)
        ge = c_mid >= kf
        live = done < 0.5
        lo = jnp.where(live & ge, mid, lo)
        hi = jnp.where(live & jnp.logical_not(ge), mid, hi)
        done = jnp.maximum(done, jnp.where(c_mid == kf, 1.0, 0.0))
        return lo, hi, done

    lo, hi, done = lax.fori_loop(0, BISECT_STEPS, bisect, (rmin, jnp.full((1, TQ), jnp.inf, F32), done0))
    left = n_open(done)

    def step_down(c):
        lo, hi, done, tied, _ = c

        def body(j, best):
            x = sc_ref[kblock(j), :]
            return jnp.maximum(best, jnp.max(jnp.where(x < hi, x, NEG_INF), axis=0, keepdims=True))

        cand = lax.fori_loop(0, nk, body, jnp.full((1, TQ), NEG_INF, F32))
        c_cand = count(lambda x: x >= cand)
        ok = c_cand >= kf
        live = done < 0.5
        lo = jnp.where(live & ok, cand, lo)
        hi = jnp.where(live & jnp.logical_not(ok), cand, hi)
        tied = jnp.maximum(tied, jnp.where(live & (c_cand > kf), 1.0, 0.0))
        done = jnp.maximum(done, jnp.where(ok, 1.0, 0.0))
        return lo, hi, done, tied, n_open(done)

    thr, _, _, tied, _ = lax.while_loop(lambda c: c[4] > 0, step_down,
                                        (lo, hi, done, jnp.zeros((1, TQ), F32), left))
    n_tied = jnp.sum(tied).astype(I32)

    @pl.when(n_tied == 0)
    def _():
        def body(j, carry):
            x = sc_ref[kblock(j), :]
            sc_ref[kblock(j), :] = jnp.where(x >= thr, 0.0, NEG_INF)
            return carry

        lax.fori_loop(0, nk, body, 0)

    @pl.when(n_tied > 0)
    def _():
        quota = kf - count(lambda x: x > thr)

        def body(j, ties_before):
            x = sc_ref[kblock(j), :]
            eq = jnp.where(x == thr, 1.0, 0.0)
            rank = jnp.dot(tri_ref[...], eq.astype(BF16), preferred_element_type=F32) + ties_before
            sel = (x > thr) | ((x == thr) & (rank < quota))
            sc_ref[kblock(j), :] = jnp.where(sel, 0.0, NEG_INF)
            return ties_before + jnp.sum(eq, axis=0, keepdims=True)

        lax.fori_loop(0, nk, body, jnp.zeros((1, TQ), F32))

    n_pairs = (nk + 1) // 2

    @pl.when(nk % 2 == 1)
    def _():
        sc_ref[kblock(nk), :] = jnp.full((KB, TQ), NEG_INF, F32)

    def score_phase(j, s_out):
        block_max = []
        for h in range(H_ATT):
            pair = slice((h // 2) * LANES, (h // 2 + 1) * LANES)
            s = jnp.dot(k_ref[0, kblock(j), pair], qpad_ref[h], preferred_element_type=F32)
            s = s + sc_ref[kblock(j), :]
            s_out[h] = s
            block_max.append(jnp.max(s, axis=0, keepdims=True))
        return tuple(block_max)

    ones_rows = (lax.broadcasted_iota(I32, (SUM_ROWS, KB), 0) == 0).astype(BF16)

    def value_phase(j, p_in, alphas, accs):
        new_accs = []
        for h in range(H_ATT):
            hrows = slice(h * DH_ATT, (h + 1) * DH_ATT)
            v_aug = jnp.concatenate([vT_ref[0, j, hrows, :], ones_rows], axis=0)
            pv = jnp.dot(v_aug, p_in[h], preferred_element_type=F32)
            new_accs.append(alphas[h] * accs[h] + pv)
        return tuple(new_accs)

    def block_step(j, next_j, s_in, s_out, p_prev, p_out, carry):
        ms, accs, block_max, alphas_prev = carry
        next_max = score_phase(next_j, s_out)
        new_ms, alphas = [], []
        for h in range(H_ATT):
            m_new = jnp.maximum(ms[h], block_max[h])
            m_safe = jnp.where(m_new == NEG_INF, 0.0, m_new)
            alphas.append(jnp.exp2(ms[h] - m_safe))
            new_ms.append(m_new)
            p_out[h] = jnp.exp2(s_in[h] - m_safe).astype(BF16)
        accs = value_phase(jnp.maximum(j - 1, 0), p_prev, alphas_prev, accs)
        return tuple(new_ms), accs, next_max, tuple(alphas)

    def attend_pair(jj, carry):
        j0 = 2 * jj
        carry = block_step(j0, j0 + 1, s_scr, s2_scr, p2_scr, p_scr, carry)
        return block_step(j0 + 1, jnp.minimum(j0 + 2, 2 * n_pairs - 1), s2_scr, s_scr, p_scr, p2_scr, carry)

    p2_scr[...] = jnp.zeros(p2_scr.shape, BF16)
    init = (tuple(jnp.full((1, TQ), NEG_INF, F32) for _ in range(H_ATT)),
            tuple(jnp.zeros((DH_ATT + SUM_ROWS, TQ), F32) for _ in range(H_ATT)),
            score_phase(0, s_scr),
            tuple(jnp.ones((1, TQ), F32) for _ in range(H_ATT)))
    _, accs, _, alphas = lax.fori_loop(0, n_pairs, attend_pair, init)
    accs = value_phase(2 * n_pairs - 1, p2_scr, alphas, accs)
    for h in range(H_ATT):
        oT_ref[h * DH_ATT:(h + 1) * DH_ATT, :] = accs[h][:DH_ATT] / accs[h][DH_ATT:DH_ATT + 1]
    o_ref[0] = oT_ref[...].T.astype(BF16)


def _dsa_prompt(qT, qiT, wT, kwb, kb, vTb):
    b, _, s = qT.shape
    assert s % (2 * KB) == 0 and TQ == KB
    topk = min(TOPK_MAX, s // 4)
    tri = (jnp.arange(KB)[:, None] > jnp.arange(KB)[None, :]).astype(BF16)
    feat = lambda w: pl.BlockSpec((1, w, TQ), lambda i, j: (i, 0, j))
    full = lambda w: pl.BlockSpec((1, s, w), lambda i, j: (i, 0, 0))
    return pl.pallas_call(
        functools.partial(_dsa_prompt_kernel, topk=topk),
        grid=(b, s // TQ),
        in_specs=[feat(HD), feat(HD), feat(H_IDX), full(LANES), full(HD),
                  pl.BlockSpec((1, s // KB, HD, KB), lambda i, j: (i, 0, 0, 0)), _const_spec(tri.shape)],
        out_specs=pl.BlockSpec((1, TQ, HD), lambda i, j: (i, j, 0)),
        out_shape=jax.ShapeDtypeStruct((b, s, HD), BF16),
        scratch_shapes=[pltpu.VMEM((s, TQ), F32), pltpu.VMEM((H_ATT, LANES, TQ), BF16),
                        pltpu.VMEM((H_IDX, LANES, TQ), BF16), pltpu.VMEM((HD, TQ), F32),
                        pltpu.VMEM((H_ATT, KB, TQ), F32), pltpu.VMEM((H_ATT, KB, TQ), F32),
                        pltpu.VMEM((H_ATT, KB, TQ), BF16), pltpu.VMEM((H_ATT, KB, TQ), BF16)],
        compiler_params=_params(2),
        name="dsa_prompt",
    )(qT, qiT, wT, kwb, kb, vTb, tri)


PAGE_GROUP_IDX = 32
PAGE_GROUP_KV = 32
COUNT_LANES = 8


def _dsa_sample_select_kernel(pt_ref, qi_ref, kwq_ref, tri_ref, *rest, n_pages, topk, group):
    page_refs, (bias_ref, biasn_ref, qiall_ref, wb_ref) = rest[:group], rest[group:]
    pg = pl.program_id(1)
    t = qi_ref.shape[1]
    kwq = kwq_ref[0]

    @pl.when(pg == 0)
    def _():
        qi = qi_ref[0].astype(F32)
        qiall_ref[...] = jnp.concatenate(
            [qi[:, h * D_IDX:(h + 1) * D_IDX] for h in range(H_IDX)], axis=0).astype(BF16)
        for h in range(H_IDX):
            wb_ref[h] = jnp.broadcast_to(kwq[:, D_IDX + h:D_IDX + h + 1], (t, LANES))

    def scores(dots):
        acc = jnp.zeros((t, dots.shape[1]), F32)
        for h in range(H_IDX):
            acc = acc + jnp.maximum(dots[h * t:(h + 1) * t], 0.0) * wb_ref[h]
        return acc

    for g in range(group):
        dots = jnp.dot(qiall_ref[...], page_refs[g][0].astype(BF16), preferred_element_type=F32)
        bias_ref[0, pg * group + g] = _score_key(scores(dots))

    @pl.when(pg == pl.num_programs(1) - 1)
    def _():
        new_keys = jnp.concatenate([kwq[:, :D_IDX], jnp.zeros((LANES - t, D_IDX), F32)], axis=0).astype(BF16)
        row = lax.broadcasted_iota(I32, (t, LANES), 0)
        col = lax.broadcasted_iota(I32, (t, LANES), 1)
        dots_new = lax.dot_general(qiall_ref[...], new_keys, NT_DIMS, preferred_element_type=F32)
        key_new = _score_key(jnp.where(col <= row, scores(dots_new), NEG_INF))

        def count(cmp_fn):
            parts = [cmp_fn(key_new).astype(I32)] + [jnp.zeros_like(key_new)] * (COUNT_LANES - 1)
            for p in range(n_pages):
                parts[p % COUNT_LANES] = parts[p % COUNT_LANES] + cmp_fn(bias_ref[0, p]).astype(I32)
            cnt = functools.reduce(lambda a, b: a + b, parts)
            return jnp.sum(cnt, axis=1, keepdims=True)

        thr = _radix_threshold(lambda c: count(lambda key: key >= c), t, topk)
        n_inexact = jnp.sum(jnp.where(count(lambda key: key >= thr) == topk, 0, 1))
        as_bias = lambda sel: lax.bitcast_convert_type(jnp.where(sel, 0.0, NEG_INF), I32)

        @pl.when(n_inexact == 0)
        def _():
            bias_ref[0] = as_bias(bias_ref[0] >= thr)
            biasn_ref[0] = as_bias(key_new >= thr)

        @pl.when(n_inexact > 0)
        def _():
            quota = (topk - count(lambda key: key > thr)).astype(F32)

            def select(key, ties_before):
                eq = key == thr
                eqf = eq.astype(F32)
                rank = jnp.dot(eqf.astype(BF16), tri_ref[...], preferred_element_type=F32) + ties_before
                sel = (key > thr) | (eq & (rank < quota))
                return as_bias(sel), ties_before + jnp.sum(eqf, axis=1, keepdims=True)

            def bias_page(p, ties_before):
                bias, ties = select(bias_ref[0, p], ties_before)
                bias_ref[0, p] = bias
                return ties

            ties = lax.fori_loop(0, n_pages, bias_page, jnp.zeros((t, 1), F32))
            biasn_ref[0], _ = select(key_new, ties)


def _dsa_sample_select(page_table, qib, kw32, cache_kidx_t):
    b, t, _ = qib.shape
    n_pages = page_table.shape[1]
    page = cache_kidx_t.shape[2]
    topk = min(TOPK_MAX, (n_pages * page + t) // 4)
    group = min(PAGE_GROUP_IDX, n_pages)
    assert n_pages % group == 0 and n_pages % COUNT_LANES == 0
    tri = (jnp.arange(page)[:, None] < jnp.arange(page)[None, :]).astype(BF16)
    tok = lambda w: pl.BlockSpec((1, t, w), lambda i, j, pt: (i, 0, 0))
    page_specs = [pl.BlockSpec((1, D_IDX, page), lambda i, j, pt, g=g: (pt[i, j * group + g], 0, 0))
                  for g in range(group)]
    grid_spec = pltpu.PrefetchScalarGridSpec(
        num_scalar_prefetch=1,
        grid=(b, n_pages // group),
        in_specs=[tok(H_IDX * D_IDX), tok(LANES), pl.BlockSpec(tri.shape, lambda i, j, pt: (0, 0))] + page_specs,
        out_specs=[pl.BlockSpec((1, n_pages, t, page), lambda i, j, pt: (i, 0, 0, 0)),
                   pl.BlockSpec((1, t, LANES), lambda i, j, pt: (i, 0, 0))],
        scratch_shapes=[pltpu.VMEM((H_IDX * t, D_IDX), BF16), pltpu.VMEM((H_IDX, t, LANES), F32)],
    )
    return pl.pallas_call(
        functools.partial(_dsa_sample_select_kernel, n_pages=n_pages, topk=topk, group=group),
        grid_spec=grid_spec,
        out_shape=[jax.ShapeDtypeStruct((b, n_pages, t, page), I32), jax.ShapeDtypeStruct((b, t, LANES), I32)],
        compiler_params=pltpu.CompilerParams(dimension_semantics=("parallel", "arbitrary"),
                                             vmem_limit_bytes=VMEM_LIMIT),
        name="dsa_sample_select",
    )(page_table, qib, kw32, tri, *([cache_kidx_t] * group))


def _dsa_sample_attend_kernel(pt_ref, q_ref, kn_ref, vn_ref, bias_ref, biasn_ref, *rest, group):
    kT_refs, vT_refs = rest[:group], rest[group:2 * group]
    o_ref, m_ref, l_ref, acc_ref = rest[2 * group:]
    pg = pl.program_id(1)
    t = q_ref.shape[1]
    page = kT_refs[0].shape[2]
    lane_head = lax.broadcasted_iota(I32, (t, HD), 1) // DH_ATT
    q = q_ref[0].astype(F32)
    qbd = jnp.concatenate([jnp.where(lane_head == h, q, 0.0) for h in range(H_ATT)], axis=0).astype(BF16)

    @pl.when(pg == 0)
    def _():
        m_ref[...] = jnp.full(m_ref.shape, NEG_INF, F32)
        l_ref[...] = jnp.zeros(l_ref.shape, F32)
        acc_ref[...] = jnp.zeros(acc_ref.shape, F32)

    def update(s, bias, pv):
        s = s + jnp.concatenate([lax.bitcast_convert_type(bias, F32)] * H_ATT, axis=0)
        m = m_ref[...]
        m_new = jnp.maximum(m, jnp.max(s, axis=1, keepdims=True))
        m_safe = jnp.where(m_new == NEG_INF, 0.0, m_new)
        alpha = jnp.exp(m - m_safe)
        p = jnp.exp(s - m_safe)
        l_ref[...] = alpha * l_ref[...] + jnp.sum(p, axis=1, keepdims=True)
        acc_ref[...] = alpha * acc_ref[...] + pv(p.astype(BF16))
        m_ref[...] = m_new

    s_pages = jnp.concatenate(
        [jnp.dot(qbd, kT_refs[g][0].astype(BF16), preferred_element_type=F32) for g in range(group)], axis=1)
    bias_pages = jnp.concatenate([bias_ref[0, g] for g in range(group)], axis=1)

    def pv_pages(p):
        out = jnp.zeros((H_ATT * t, HD), F32)
        for g in range(group):
            out = out + lax.dot_general(p[:, g * page:(g + 1) * page], vT_refs[g][0].astype(BF16), NT_DIMS,
                                        preferred_element_type=F32)
        return out

    update(s_pages, bias_pages, pv_pages)

    @pl.when(pg == pl.num_programs(1) - 1)
    def _():
        pad = jnp.zeros((LANES - t, HD), F32)
        kn = jnp.concatenate([kn_ref[0], pad], axis=0).astype(BF16)
        vn = jnp.concatenate([vn_ref[0], pad], axis=0).astype(BF16)
        update(lax.dot_general(qbd, kn, NT_DIMS, preferred_element_type=F32), biasn_ref[0],
               lambda p: jnp.dot(p, vn, preferred_element_type=F32))
        o = acc_ref[...] / l_ref[...]
        out = jnp.zeros((t, HD), F32)
        for h in range(H_ATT):
            out = out + jnp.where(lane_head == h, o[h * t:(h + 1) * t], 0.0)
        o_ref[0] = out.astype(BF16)


def _dsa_sample_attend(page_table, qb, k32, v32, bias, bias_new, cache_kt, cache_vt):
    b, t, _ = qb.shape
    n_pages = page_table.shape[1]
    page = cache_kt.shape[2]
    group = min(PAGE_GROUP_KV, n_pages)
    assert n_pages % group == 0
    tok = lambda w: pl.BlockSpec((1, t, w), lambda i, j, pt: (i, 0, 0))
    kv_specs = [pl.BlockSpec((1, HD, page), lambda i, j, pt, g=g: (pt[i, j * group + g], 0, 0))
                for g in range(group)]
    grid_spec = pltpu.PrefetchScalarGridSpec(
        num_scalar_prefetch=1,
        grid=(b, n_pages // group),
        in_specs=[tok(HD), tok(HD), tok(HD),
                  pl.BlockSpec((1, group, t, page), lambda i, j, pt: (i, j, 0, 0)),
                  tok(LANES)] + kv_specs + kv_specs,
        out_specs=tok(HD),
        scratch_shapes=[pltpu.VMEM((H_ATT * t, 1), F32), pltpu.VMEM((H_ATT * t, 1), F32),
                        pltpu.VMEM((H_ATT * t, HD), F32)],
    )
    return pl.pallas_call(
        functools.partial(_dsa_sample_attend_kernel, group=group),
        grid_spec=grid_spec,
        out_shape=jax.ShapeDtypeStruct((b, t, HD), BF16),
        compiler_params=pltpu.CompilerParams(dimension_semantics=("parallel", "arbitrary"),
                                             vmem_limit_bytes=VMEM_LIMIT),
        name="dsa_sample_attend",
    )(page_table, qb, k32, v32, bias, bias_new, *([cache_kt] * group), *([cache_vt] * group))


def _merge_kernel(x_ref, oa_ref, ol_ref, om_ref, g_ref, wg_ref, wa_ref, wl_ref, wm_ref, wo_ref, o_ref):
    x = x_ref[...]
    hn = _rms(x, g_ref[...]).astype(BF16)
    m = jnp.zeros_like(x)
    for idx, (o_r, w_r) in enumerate(((oa_ref, wa_ref), (ol_ref, wl_ref), (om_ref, wm_ref))):
        gate = jax.nn.sigmoid(jnp.dot(hn, wg_ref[:, idx * D_MODEL:(idx + 1) * D_MODEL], preferred_element_type=F32))
        m = m + gate * jnp.dot(o_r[...], w_r[...], preferred_element_type=F32)
    o_ref[...] = x + jnp.dot(m.astype(BF16), wo_ref[...], preferred_element_type=F32)


def _merge(x, o_att, o_lru, o_mem, consts, tm):
    n = x.shape[0]
    row = lambda w: pl.BlockSpec((tm, w), lambda i: (i, 0))
    return pl.pallas_call(
        _merge_kernel,
        grid=(n // tm,),
        in_specs=[row(D_MODEL), row(HD), row(LRU_W), row(MD)] + [_const_spec(c.shape) for c in consts],
        out_specs=row(D_MODEL),
        out_shape=jax.ShapeDtypeStruct((n, D_MODEL), F32),
        compiler_params=_params(1),
        name="merge",
    )(x, o_att, o_lru, o_mem, *consts)


def _token_tile(n):
    return min(n, 512)


def _layer(x, is_prompt, lw, pp, extra):
    b, t, _ = x.shape
    n = b * t
    tm = _token_tile(n)
    x1 = _ffn(x.reshape(n, D_MODEL), pp["ffn1"], tm)
    r3 = lambda a: a.reshape(b, t, a.shape[-1])
    if is_prompt:
        kT32, vT32, kiT32, wT, qT, qiT, kb, kwb, vTb, lx, lg, qmb = _proj_t(r3(x1), pp["proj"], tm)
        o_att = _dsa_prompt(qT, qiT, wT, kwb, kb, vTb)
        k_new = kT32.reshape(b, H_ATT, DH_ATT, t).transpose(0, 3, 1, 2)
        v_new = vT32.reshape(b, H_ATT, DH_ATT, t).transpose(0, 3, 1, 2)
        ki_new = kiT32.transpose(0, 2, 1)
        conv_state = jnp.zeros((b, CONV_W - 1, LRU_W), F32)
        h0 = jnp.zeros((b, LRU_W), F32)
        mem = extra["mem"]
        mk, mv = _memkv(mem.reshape(-1, D_MODEL), lw["norm_mem_g"], lw["w_mem_kv"], lw["mem_k_norm_g"],
                        pp["proj"]["g128"])
        mk = mk.reshape(b, -1, MD)
        mv = mv.reshape(b, -1, MD)
    else:
        k32, v32, kw32, qb, _, _, qib, _, lx, lg, qmb = _proj(x1, pp["proj"], tm)
        pt = extra["page_table"]
        bias, bias_new = _dsa_sample_select(pt, r3(qib), r3(kw32), extra["cache_kidx_t"])
        o_att = _dsa_sample_attend(pt, r3(qb), r3(k32), r3(v32), bias, bias_new, extra["cache_kt"],
                                   extra["cache_vt"])
        k_new = k32.reshape(b, t, H_ATT, DH_ATT)
        v_new = v32.reshape(b, t, H_ATT, DH_ATT)
        ki_new = r3(kw32)[:, :, :D_IDX]
        conv_state, h0 = extra["state_conv"], extra["state_h"]
        mk, mv = extra["cache_mem_k"], extra["cache_mem_v"]
    o_lru, conv_buf, h_last = _rglru(r3(lx), r3(lg), conv_state, h0, pp["rglru"], min(t, 256))
    o_mem = _memattn(r3(qmb), mk, mv, min(t, 512))
    x2 = _merge(x1, o_att.reshape(n, HD), o_lru.reshape(n, LRU_W), o_mem.reshape(n, MD), pp["merge"], tm)
    y = _ffn(x2, pp["ffn2"], tm).reshape(b, t, D_MODEL)
    if is_prompt:
        state = (k_new, v_new, ki_new, mk.reshape(b, -1, H_MEM, DH_MEM), mv.reshape(b, -1, H_MEM, DH_MEM),
                 conv_buf, h_last)
    else:
        state = (k_new, v_new, ki_new, conv_buf, h_last)
    return y, state


def kernel(x_prompt, x_sample, cache_k, cache_v, cache_kidx, page_table, cache_mem_k, cache_mem_v, state_conv, state_h, mem_prompt, norm_ffn1_g, w_ffn1_in, w_ffn1_out, norm_mix_g, w_in, q_norm_g, k_norm_g, w_attn_o, conv_w, conv_b, lru_wa, lru_ba, lru_wi, lru_bi, lru_lambda, w_lru_o, norm_mem_g, w_mem_kv, mem_q_norm_g, mem_k_norm_g, w_mem_o, w_out, norm_ffn2_g, w_ffn2_in, w_ffn2_out):
    depth = w_in.shape[0]
    n_phys, page = cache_k.shape[1], cache_k.shape[2]
    xp, xs = x_prompt, x_sample
    p_states, s_states = [], []
    for l in range(depth):
        proj = _prep_proj(norm_mix_g[l], w_in[l], q_norm_g[l], k_norm_g[l], mem_q_norm_g[l])
        pp = dict(
            ffn1=_prep_ffn(norm_ffn1_g[l], w_ffn1_in[l], w_ffn1_out[l]),
            ffn2=_prep_ffn(norm_ffn2_g[l], w_ffn2_in[l], w_ffn2_out[l]),
            proj=proj,
            rglru=_prep_rglru(conv_w[l], conv_b[l], lru_wa[l], lru_ba[l], lru_wi[l], lru_bi[l], lru_lambda[l]),
            merge=[proj["g"], proj["wgates"], w_attn_o[l].astype(BF16), w_lru_o[l].astype(BF16),
                   w_mem_o[l].astype(BF16), w_out[l].astype(BF16)],
        )
        lw = dict(norm_mem_g=norm_mem_g[l], w_mem_kv=w_mem_kv[l], mem_k_norm_g=mem_k_norm_g[l])
        xp, st_p = _layer(xp, True, lw, pp, dict(mem=mem_prompt))
        xs, st_s = _layer(xs, False, lw, pp, dict(
            page_table=page_table,
            cache_kt=cache_k[l].transpose(0, 2, 3, 1).reshape(n_phys, HD, page),
            cache_vt=cache_v[l].transpose(0, 2, 3, 1).reshape(n_phys, HD, page),
            cache_kidx_t=cache_kidx[l].transpose(0, 2, 1), cache_mem_k=cache_mem_k[l].reshape(-1, cache_mem_k.shape[2], MD),
            cache_mem_v=cache_mem_v[l].reshape(-1, cache_mem_v.shape[2], MD),
            state_conv=state_conv[l], state_h=state_h[l]))
        p_states.append(st_p)
        s_states.append(st_s)
    stack = lambda states, i: jnp.stack([s[i] for s in states])
    return (xp, xs) + tuple(stack(p_states, i) for i in range(7)) + tuple(stack(s_states, i) for i in range(5))
```

```python
import functools

import jax
import jax.numpy as jnp
from jax import lax
from jax.experimental import pallas as pl
from jax.experimental.pallas import tpu as pltpu

F32, BF16, I32 = jnp.float32, jnp.bfloat16, jnp.int32

D_MODEL = 1024
H_ATT, DH_ATT = 8, 64
H_IDX, D_IDX = 8, 64
TOPK_MAX = 256
LRU_W, LRU_BLOCKS, CONV_W, LRU_C = 512, 8, 4, 8.0
H_MEM, DH_MEM = 4, 128
D_FF = 2816
EPS = 1e-6
HD = H_ATT * DH_ATT
MD = H_MEM * DH_MEM

LANES = 128
SUBLANES = 8
VMEM_BYTES_V7X = 64 * 1024 * 1024
VMEM_LIMIT = VMEM_BYTES_V7X - 8 * 1024 * 1024

FF_CHUNK = 256
TQ = 256
KB = 256
SCORE_BLOCKS = (4, 2, 1)
CNT_ROWS = 32
SUM_ROWS = 16
BISECT_STEPS = 15
LOG2E = 1.4426950408889634
NEG_INF = float("-inf")

NT_DIMS = (((1,), (1,)), ((), ()))


def _params(n_grid, parallel=True):
    sem = ("parallel" if parallel else "arbitrary",) * n_grid
    return pltpu.CompilerParams(dimension_semantics=sem, vmem_limit_bytes=VMEM_LIMIT)


def _const_spec(shape):
    nd = len(shape)
    return pl.BlockSpec(shape, lambda *_: (0,) * nd)


def _rms(x, g):
    ms = jnp.mean(x * x, axis=-1, keepdims=True)
    return x * lax.rsqrt(ms + EPS) * g


def _group_rms(x, gmat, g, group):
    x2 = x * x
    hi = x2.astype(BF16)
    lo = (x2 - hi.astype(F32)).astype(BF16)
    ss = jnp.dot(hi, gmat, preferred_element_type=F32) + jnp.dot(lo, gmat, preferred_element_type=F32)
    return x * lax.rsqrt(ss * (1.0 / group) + EPS) * g


def _group_matrix(width, group):
    idx = jnp.arange(width) // group
    return (idx[:, None] == idx[None, :]).astype(BF16)


def _ffn_kernel(x_ref, g_ref, wg_ref, wu_ref, wo_ref, o_ref):
    x = x_ref[...]
    hn = _rms(x, g_ref[...]).astype(BF16)
    acc = jnp.zeros_like(x)
    for c in range(wg_ref.shape[0]):
        gate = jnp.dot(hn, wg_ref[c], preferred_element_type=F32)
        up = jnp.dot(hn, wu_ref[c], preferred_element_type=F32)
        act = (gate * jax.nn.sigmoid(gate) * up).astype(BF16)
        acc = acc + jnp.dot(act, wo_ref[c], preferred_element_type=F32)
    o_ref[...] = x + 0.5 * acc


def _prep_ffn(g, w_in, w_out):
    nc = D_FF // FF_CHUNK
    wg = w_in[:, :D_FF].reshape(D_MODEL, nc, FF_CHUNK).transpose(1, 0, 2).astype(BF16)
    wu = w_in[:, D_FF:].reshape(D_MODEL, nc, FF_CHUNK).transpose(1, 0, 2).astype(BF16)
    wo = w_out.reshape(nc, FF_CHUNK, D_MODEL).astype(BF16)
    return g.reshape(1, D_MODEL), wg, wu, wo


def _ffn(x, prep, tm):
    g, wg, wu, wo = prep
    n = x.shape[0]
    row = pl.BlockSpec((tm, D_MODEL), lambda i: (i, 0))
    return pl.pallas_call(
        _ffn_kernel,
        grid=(n // tm,),
        in_specs=[row, _const_spec(g.shape), _const_spec(wg.shape), _const_spec(wu.shape), _const_spec(wo.shape)],
        out_specs=row,
        out_shape=jax.ShapeDtypeStruct((n, D_MODEL), F32),
        compiler_params=_params(1),
        name="ffn",
    )(x, g, wg, wu, wo)


def _proj_kernel(x_ref, g_ref, wqkv_ref, wqi_ref, wkw_ref, wl_ref, wqm_ref, qg_ref, kg_ref, mg_ref,
                 kwscale_ref, g64_ref, g128_ref,
                 k32_ref, v32_ref, kw32_ref, qb_ref, kb_ref, vb_ref, qib_ref, kwb_ref, lx_ref, lg_ref, qmb_ref):
    hn = _rms(x_ref[...], g_ref[...]).astype(BF16)
    qkv = jnp.dot(hn, wqkv_ref[...], preferred_element_type=F32)
    q = _group_rms(qkv[:, :HD], g64_ref[...], qg_ref[...], DH_ATT)
    k = _group_rms(qkv[:, HD:2 * HD], g64_ref[...], kg_ref[...], DH_ATT)
    v = qkv[:, 2 * HD:]
    k32_ref[...] = k
    v32_ref[...] = v
    qb_ref[...] = (q * (DH_ATT ** -0.5)).astype(BF16)
    kb_ref[...] = k.astype(BF16)
    vb_ref[...] = v.astype(BF16)
    qib_ref[...] = jnp.dot(hn, wqi_ref[...], preferred_element_type=F32).astype(BF16)
    kw = jnp.dot(hn, wkw_ref[...], preferred_element_type=F32) * kwscale_ref[...]
    kw32_ref[...] = kw
    kwb_ref[...] = kw.astype(BF16)
    lxg = jnp.dot(hn, wl_ref[...], preferred_element_type=F32)
    lx_ref[...] = lxg[:, :LRU_W]
    lg_ref[...] = lxg[:, LRU_W:]
    qm = jnp.dot(hn, wqm_ref[...], preferred_element_type=F32)
    qmb_ref[...] = _group_rms(qm, g128_ref[...], mg_ref[...], DH_MEM).astype(BF16)


def _prep_proj(norm_g, w_in, q_norm_g, k_norm_g, mem_q_norm_g):
    o = 0
    cols = {}
    for name, size in (("q", HD), ("k", HD), ("v", HD), ("qi", H_IDX * D_IDX), ("ki", D_IDX), ("wi", H_IDX),
                       ("lx", LRU_W), ("lg", LRU_W), ("qm", MD), ("gates", 3 * D_MODEL)):
        cols[name] = w_in[:, o:o + size]
        o += size
    pad = jnp.zeros((D_MODEL, LANES - D_IDX - H_IDX), w_in.dtype)
    wqkv = jnp.concatenate([cols["q"], cols["k"], cols["v"]], axis=1).astype(BF16)
    wkw = jnp.concatenate([cols["ki"], cols["wi"], pad], axis=1).astype(BF16)
    wl = jnp.concatenate([cols["lx"], cols["lg"]], axis=1).astype(BF16)
    kwscale = jnp.concatenate([jnp.ones((D_IDX,), F32),
                               jnp.full((H_IDX,), H_IDX ** -0.5 * D_IDX ** -0.5, F32),
                               jnp.zeros((LANES - D_IDX - H_IDX,), F32)]).reshape(1, LANES)
    return dict(
        g=norm_g.reshape(1, D_MODEL), wqkv=wqkv, wqi=cols["qi"].astype(BF16), wkw=wkw, wl=wl,
        wqm=cols["qm"].astype(BF16),
        qg=jnp.tile(q_norm_g, H_ATT).reshape(1, HD), kg=jnp.tile(k_norm_g, H_ATT).reshape(1, HD),
        mg=jnp.tile(mem_q_norm_g, H_MEM).reshape(1, MD), kwscale=kwscale,
        g64=_group_matrix(HD, DH_ATT), g128=_group_matrix(MD, DH_MEM),
        wgates=cols["gates"].astype(BF16),
        wqkT=wqkv[:, :2 * HD].T, wvT=wqkv[:, 2 * HD:].T, wqiT=cols["qi"].astype(BF16).T, wkwT=wkw.T,
        qg_col=jnp.tile(q_norm_g, H_ATT).reshape(HD, 1), kg_col=jnp.tile(k_norm_g, H_ATT).reshape(HD, 1),
        kwscale_col=kwscale.reshape(LANES, 1),
    )


def _group_rms_t(xt, gmat, g, group):
    x2 = xt * xt
    hi = x2.astype(BF16)
    lo = (x2 - hi.astype(F32)).astype(BF16)
    ss = jnp.dot(gmat, hi, preferred_element_type=F32) + jnp.dot(gmat, lo, preferred_element_type=F32)
    return xt * lax.rsqrt(ss * (1.0 / group) + EPS) * g


def _proj_t_kernel(x_ref, g_ref, wqkT_ref, wvT_ref, wqiT_ref, wkwT_ref, wl_ref, wqm_ref, qgT_ref, kgT_ref, mg_ref,
                   kwscaleT_ref, g64_ref, g128_ref,
                   kT_ref, vT_ref, kiT_ref, wT_ref, qT_ref, qiT_ref, kb_ref, kwb_ref, vTb_ref, lx_ref, lg_ref,
                   qmb_ref):
    hn = _rms(x_ref[0], g_ref[...]).astype(BF16)
    nt = lambda w_ref: lax.dot_general(w_ref[...], hn, NT_DIMS, preferred_element_type=F32)
    qkT = nt(wqkT_ref)
    qT = _group_rms_t(qkT[:HD], g64_ref[...], qgT_ref[...], DH_ATT)
    kT = _group_rms_t(qkT[HD:], g64_ref[...], kgT_ref[...], DH_ATT)
    kT_ref[0] = kT
    kb_ref[0] = kT.T.astype(BF16)
    qT_ref[0] = (qT * (DH_ATT ** -0.5 * LOG2E)).astype(BF16)
    vT = nt(wvT_ref)
    vT_ref[0] = vT
    for c in range(vTb_ref.shape[1]):
        vTb_ref[0, c] = vT[:, c * KB:(c + 1) * KB].astype(BF16)
    qiT_ref[0] = nt(wqiT_ref).astype(BF16)
    kwT = nt(wkwT_ref) * kwscaleT_ref[...]
    kiT_ref[0] = kwT[:D_IDX]
    wT_ref[0] = kwT[D_IDX:D_IDX + H_IDX]
    kwb_ref[0] = kwT.T.astype(BF16)
    lxg = jnp.dot(hn, wl_ref[...], preferred_element_type=F32)
    lx_ref[0] = lxg[:, :LRU_W]
    lg_ref[0] = lxg[:, LRU_W:]
    qm = jnp.dot(hn, wqm_ref[...], preferred_element_type=F32)
    qmb_ref[0] = _group_rms(qm, g128_ref[...], mg_ref[...], DH_MEM).astype(BF16)


def _proj_t(x, p, tm):
    b, s, _ = x.shape
    bc = lambda col: jnp.broadcast_to(col, (col.shape[0], tm))
    consts = [p["g"], p["wqkT"], p["wvT"], p["wqiT"], p["wkwT"], p["wl"], p["wqm"], bc(p["qg_col"]),
              bc(p["kg_col"]), p["mg"], bc(p["kwscale_col"]), p["g64"], p["g128"]]
    tok = lambda w: pl.BlockSpec((1, tm, w), lambda i, j: (i, j, 0))
    feat = lambda w: pl.BlockSpec((1, w, tm), lambda i, j: (i, 0, j))
    outs = [
        (feat(HD), (b, HD, s), F32), (feat(HD), (b, HD, s), F32), (feat(D_IDX), (b, D_IDX, s), F32),
        (feat(H_IDX), (b, H_IDX, s), F32), (feat(HD), (b, HD, s), BF16), (feat(HD), (b, HD, s), BF16),
        (tok(HD), (b, s, HD), BF16), (tok(LANES), (b, s, LANES), BF16),
        (pl.BlockSpec((1, tm // KB, HD, KB), lambda i, j: (i, j, 0, 0)), (b, s // KB, HD, KB), BF16),
        (tok(LRU_W), (b, s, LRU_W), F32), (tok(LRU_W), (b, s, LRU_W), F32), (tok(MD), (b, s, MD), BF16),
    ]
    return pl.pallas_call(
        _proj_t_kernel,
        grid=(b, s // tm),
        in_specs=[tok(D_MODEL)] + [_const_spec(c.shape) for c in consts],
        out_specs=[o[0] for o in outs],
        out_shape=[jax.ShapeDtypeStruct(o[1], o[2]) for o in outs],
        compiler_params=_params(2),
        name="proj_t",
    )(x, *consts)


def _proj(x, p, tm):
    n = x.shape[0]
    consts = [p[k] for k in ("g", "wqkv", "wqi", "wkw", "wl", "wqm", "qg", "kg", "mg", "kwscale", "g64", "g128")]

    def row(w):
        return pl.BlockSpec((tm, w), lambda i: (i, 0))

    outs = [(HD, F32), (HD, F32), (LANES, F32), (HD, BF16), (HD, BF16), (HD, BF16), (HD, BF16), (LANES, BF16),
            (LRU_W, F32), (LRU_W, F32), (MD, BF16)]
    return pl.pallas_call(
        _proj_kernel,
        grid=(n // tm,),
        in_specs=[row(D_MODEL)] + [_const_spec(c.shape) for c in consts],
        out_specs=[row(w) for w, _ in outs],
        out_shape=[jax.ShapeDtypeStruct((n, w), dt) for w, dt in outs],
        compiler_params=_params(1),
        name="proj",
    )(x, *consts)


def _memkv_kernel(m_ref, g_ref, w_ref, kg_ref, g128_ref, mk_ref, mv_ref):
    hn = _rms(m_ref[...], g_ref[...]).astype(BF16)
    kv = jnp.dot(hn, w_ref[...], preferred_element_type=F32)
    mk_ref[...] = _group_rms(kv[:, :MD], g128_ref[...], kg_ref[...], DH_MEM)
    mv_ref[...] = kv[:, MD:]


def _memkv(mem, norm_g, w_mem_kv, mem_k_norm_g, g128):
    n = mem.shape[0]
    tm = min(n, 512)
    consts = [norm_g.reshape(1, D_MODEL), w_mem_kv.astype(BF16), jnp.tile(mem_k_norm_g, H_MEM).reshape(1, MD), g128]
    row = lambda w: pl.BlockSpec((tm, w), lambda i: (i, 0))
    return pl.pallas_call(
        _memkv_kernel,
        grid=(n // tm,),
        in_specs=[row(D_MODEL)] + [_const_spec(c.shape) for c in consts],
        out_specs=[row(MD), row(MD)],
        out_shape=[jax.ShapeDtypeStruct((n, MD), F32)] * 2,
        compiler_params=_params(1),
        name="memkv",
    )(mem, *consts)


def _memattn_kernel(q_ref, mk_ref, mv_ref, o_ref):
    q = q_ref[0]
    mk = mk_ref[0].astype(BF16)
    mv = mv_ref[0].astype(BF16)
    for h in range(H_MEM):
        sl = slice(h * DH_MEM, (h + 1) * DH_MEM)
        s = lax.dot_general(q[:, sl], mk[:, sl], NT_DIMS, preferred_element_type=F32) * (DH_MEM ** -0.5)
        m = jnp.max(s, axis=-1, keepdims=True)
        e = jnp.exp(s - m)
        p = (e / jnp.sum(e, axis=-1, keepdims=True)).astype(BF16)
        o_ref[0, :, sl] = jnp.dot(p, mv[:, sl], preferred_element_type=F32).astype(BF16)


def _memattn(qm, mk, mv, tm):
    b, t, _ = qm.shape
    n_mem = mk.shape[1]
    return pl.pallas_call(
        _memattn_kernel,
        grid=(b, t // tm),
        in_specs=[pl.BlockSpec((1, tm, MD), lambda i, j: (i, j, 0)),
                  pl.BlockSpec((1, n_mem, MD), lambda i, j: (i, 0, 0)),
                  pl.BlockSpec((1, n_mem, MD), lambda i, j: (i, 0, 0))],
        out_specs=pl.BlockSpec((1, tm, MD), lambda i, j: (i, j, 0)),
        out_shape=jax.ShapeDtypeStruct((b, t, MD), BF16),
        compiler_params=_params(2),
        name="memattn",
    )(qm, mk, mv)


def _rglru_kernel(lx_ref, lg_ref, cs_ref, h0_ref, cw_ref, cb_ref, wa_ref, ba_ref, wi_ref, bi_ref, lam_ref,
                  y_ref, nb_ref, hl_ref, tail_ref, h_ref, xe_ref):
    t = pl.program_id(1)

    @pl.when(t == 0)
    def _():
        tail_ref[...] = cs_ref[0]
        h_ref[...] = h0_ref[0]

    x = lx_ref[0]
    tt = x.shape[0]
    xe_ref[:SUBLANES] = tail_ref[...]
    xe_ref[SUBLANES:] = x
    conv = cb_ref[...] + x * cw_ref[CONV_W - 1:CONV_W, :]
    for d in range(1, CONV_W):
        conv = conv + xe_ref[SUBLANES - d:SUBLANES - d + tt, :] * cw_ref[CONV_W - 1 - d:CONV_W - d, :]
    tail_ref[...] = x[tt - SUBLANES:]
    nb_ref[0] = x[tt - SUBLANES:]

    cb16 = conv.astype(BF16)
    r = jax.nn.sigmoid(jnp.dot(cb16, wa_ref[...], preferred_element_type=F32) + ba_ref[...])
    ig = jax.nn.sigmoid(jnp.dot(cb16, wi_ref[...], preferred_element_type=F32) + bi_ref[...])
    nl = -lam_ref[...]
    softplus = jnp.maximum(nl, 0.0) + jnp.log1p(jnp.exp(-jnp.abs(nl)))
    log_a = -LRU_C * r * softplus
    a = jnp.exp(log_a)
    b = jnp.sqrt(-jnp.tanh(log_a) * (a * a + 1.0)) * (ig * conv)
    in_group = lax.broadcasted_iota(I32, x.shape, 0) % SUBLANES
    k = 1
    while k < SUBLANES:
        keep = in_group >= k
        b = a * jnp.where(keep, pltpu.roll(b, k, 0), 0.0) + b
        a = a * jnp.where(keep, pltpu.roll(a, k, 0), 1.0)
        k *= 2
    h_prev = h_ref[SUBLANES - 1:SUBLANES, :]
    groups = []
    for g in range(tt // SUBLANES):
        rows = slice(g * SUBLANES, (g + 1) * SUBLANES)
        groups.append(b[rows] + a[rows] * h_prev)
        h_prev = groups[-1][SUBLANES - 1:SUBLANES, :]
    h = groups[0] if len(groups) == 1 else jnp.concatenate(groups, axis=0)
    h_ref[...] = h[tt - SUBLANES:]
    hl_ref[0] = h[tt - SUBLANES:]
    y_ref[0] = (h * jax.nn.gelu(lg_ref[0])).astype(BF16)


def _block_diag(w):
    nb, bs, _ = w.shape
    eye = jnp.eye(nb, dtype=w.dtype)
    return (eye[:, None, :, None] * w[:, :, None, :]).reshape(nb * bs, nb * bs)


def _prep_rglru(conv_w, conv_b, lru_wa, lru_ba, lru_wi, lru_bi, lru_lambda):
    r = lambda v: v.reshape(1, LRU_W)
    return [conv_w, r(conv_b), _block_diag(lru_wa).astype(BF16), r(lru_ba), _block_diag(lru_wi).astype(BF16),
            r(lru_bi), r(lru_lambda)]


def _rglru(lx, lg, conv_state, h0, consts, tt):
    b, t, w = lx.shape
    cs = jnp.concatenate([jnp.zeros((b, SUBLANES - (CONV_W - 1), w), F32), conv_state], axis=1)
    h0p = jnp.concatenate([jnp.zeros((b, SUBLANES - 1, w), F32), h0[:, None, :]], axis=1)
    seq = pl.BlockSpec((1, tt, w), lambda i, j: (i, j, 0))
    st = pl.BlockSpec((1, SUBLANES, w), lambda i, j: (i, 0, 0))
    y, nb, hl = pl.pallas_call(
        _rglru_kernel,
        grid=(b, t // tt),
        in_specs=[seq, seq, st, st] + [_const_spec(c.shape) for c in consts],
        out_specs=[seq, st, st],
        out_shape=[jax.ShapeDtypeStruct((b, t, w), BF16), jax.ShapeDtypeStruct((b, SUBLANES, w), F32),
                   jax.ShapeDtypeStruct((b, SUBLANES, w), F32)],
        scratch_shapes=[pltpu.VMEM((SUBLANES, w), F32), pltpu.VMEM((SUBLANES, w), F32),
                        pltpu.VMEM((SUBLANES + tt, w), F32)],
        compiler_params=pltpu.CompilerParams(dimension_semantics=("parallel", "arbitrary"),
                                             vmem_limit_bytes=VMEM_LIMIT),
        name="rglru",
    )(lx, lg, cs, h0p, *consts)
    return y, nb[:, SUBLANES - (CONV_W - 1):], hl[:, SUBLANES - 1]


def _kth_largest(count, below_max, rmin, rmax, n_valid, topk):
    kf = float(topk)
    n_open = lambda done: jnp.sum(1.0 - done).astype(I32)

    def bisect(_, c):
        lo, hi, done = c
        mid = 0.5 * lo + 0.5 * jnp.minimum(hi, rmax)
        c_mid = count(lambda x: x >= mid)
        ge = c_mid >= kf
        live = done < 0.5
        lo = jnp.where(live & ge, mid, lo)
        hi = jnp.where(live & jnp.logical_not(ge), mid, hi)
        done = jnp.maximum(done, jnp.where(c_mid == kf, 1.0, 0.0))
        return lo, hi, done

    done0 = jnp.where(n_valid <= topk, 1.0, 0.0)
    lo, hi, done = lax.fori_loop(0, BISECT_STEPS, bisect, (rmin, jnp.full(rmin.shape, jnp.inf, F32), done0))

    def step_down(c):
        lo, hi, done, tied, _ = c
        cand = below_max(hi)
        c_cand = count(lambda x: x >= cand)
        ok = c_cand >= kf
        live = done < 0.5
        lo = jnp.where(live & ok, cand, lo)
        hi = jnp.where(live & jnp.logical_not(ok), cand, hi)
        tied = jnp.maximum(tied, jnp.where(live & (c_cand > kf), 1.0, 0.0))
        done = jnp.maximum(done, jnp.where(ok, 1.0, 0.0))
        return lo, hi, done, tied, n_open(done)

    thr, _, _, tied, _ = lax.while_loop(lambda c: c[4] > 0, step_down,
                                        (lo, hi, done, jnp.zeros(rmin.shape, F32), n_open(done)))
    return thr, jnp.sum(tied).astype(I32)


def _dsa_prompt_kernel(qT_ref, qiT_ref, wT_ref, kw_ref, k_ref, vT_ref, tri_ref, o_ref,
                       sc_ref, qpad_ref, qipad_ref, oT_ref, s_scr, s2_scr, p_scr, p2_scr, *, topk):
    i = pl.program_id(1)
    nk = i + 1
    kf = float(topk)
    kblock = lambda j: pl.ds(pl.multiple_of(j * KB, KB), KB)

    zeros64 = jnp.zeros((D_IDX, TQ), BF16)
    for h in range(H_IDX):
        qipad_ref[h] = jnp.concatenate([qiT_ref[0, h * D_IDX:(h + 1) * D_IDX, :], zeros64], axis=0)
    for h in range(H_ATT):
        qh = qT_ref[0, h * DH_ATT:(h + 1) * DH_ATT, :]
        qpad_ref[h] = jnp.concatenate([qh, zeros64] if h % 2 == 0 else [zeros64, qh], axis=0)
    wT = wT_ref[0]

    def block_scores(j, n_blocks):
        rows = pl.ds(pl.multiple_of(j * KB, KB), n_blocks * KB)
        kw = kw_ref[0, rows, :]
        acc = jnp.zeros((n_blocks * KB, TQ), F32)
        for h in range(H_IDX):
            d = jnp.dot(kw, qipad_ref[h], preferred_element_type=F32)
            acc = acc + jnp.maximum(d, 0.0) * wT[h:h + 1, :]
        return rows, acc

    def score_blocks(n_blocks):
        def body(jj, carry):
            lo, hi = carry
            rows, acc = block_scores(jj * n_blocks, n_blocks)
            sc_ref[rows, :] = acc
            return (jnp.minimum(lo, jnp.min(acc, axis=0, keepdims=True)),
                    jnp.maximum(hi, jnp.max(acc, axis=0, keepdims=True)))
        return body

    bounds = (jnp.full((1, TQ), jnp.inf, F32), jnp.full((1, TQ), NEG_INF, F32))
    first = 0
    for n_blocks in SCORE_BLOCKS:
        trips = (i - first) // n_blocks
        bounds = lax.fori_loop(first // n_blocks, first // n_blocks + trips, score_blocks(n_blocks), bounds)
        first = first + trips * n_blocks
    rmin, rmax = bounds
    _, acc = block_scores(i, 1)
    krow = lax.broadcasted_iota(I32, (KB, TQ), 0)
    qcol = lax.broadcasted_iota(I32, (KB, TQ), 1)
    sc_ref[kblock(i), :] = jnp.where(krow <= qcol, acc, NEG_INF)
    rmin = jnp.minimum(rmin, jnp.min(acc, axis=0, keepdims=True))
    rmax = jnp.maximum(rmax, jnp.max(acc, axis=0, keepdims=True))

    def count(pred):
        def body(j, cnt):
            m = jnp.where(pred(sc_ref[kblock(j), :]), 1.0, 0.0)
            for r in range(KB // CNT_ROWS):
                cnt = cnt + m[r * CNT_ROWS:(r + 1) * CNT_ROWS]
            return cnt

        cnt = lax.fori_loop(0, nk, body, jnp.zeros((CNT_ROWS, TQ), F32))
        return jnp.sum(cnt, axis=0, keepdims=True)

    def below_max(hi):
        def body(j, best):
            x = sc_ref[kblock(j), :]
            return jnp.maximum(best, jnp.max(jnp.where(x < hi, x, NEG_INF), axis=0, keepdims=True))

        return lax.fori_loop(0, nk, body, jnp.full((1, TQ), NEG_INF, F32))

    n_valid = i * TQ + lax.broadcasted_iota(I32, (1, TQ), 1) + 1
    thr, n_tied = _kth_largest(count, below_max, rmin, rmax, n_valid, topk)

    @pl.when(n_tied == 0)
    def _():
        def body(j, carry):
            x = sc_ref[kblock(j), :]
            sc_ref[kblock(j), :] = jnp.where(x >= thr, 0.0, NEG_INF)
            return carry

        lax.fori_loop(0, nk, body, 0)

    @pl.when(n_tied > 0)
    def _():
        quota = kf - count(lambda x: x > thr)

        def body(j, ties_before):
            x = sc_ref[kblock(j), :]
            eq = jnp.where(x == thr, 1.0, 0.0)
            rank = jnp.dot(tri_ref[...], eq.astype(BF16), preferred_element_type=F32) + ties_before
            sel = (x > thr) | ((x == thr) & (rank < quota))
            sc_ref[kblock(j), :] = jnp.where(sel, 0.0, NEG_INF)
            return ties_before + jnp.sum(eq, axis=0, keepdims=True)

        lax.fori_loop(0, nk, body, jnp.zeros((1, TQ), F32))

    n_pairs = (nk + 1) // 2

    @pl.when(nk % 2 == 1)
    def _():
        sc_ref[kblock(nk), :] = jnp.full((KB, TQ), NEG_INF, F32)

    def score_phase(j, s_out):
        block_max = []
        for h in range(H_ATT):
            pair = slice((h // 2) * LANES, (h // 2 + 1) * LANES)
            s = jnp.dot(k_ref[0, kblock(j), pair], qpad_ref[h], preferred_element_type=F32)
            s = s + sc_ref[kblock(j), :]
            s_out[h] = s
            block_max.append(jnp.max(s, axis=0, keepdims=True))
        return tuple(block_max)

    ones_rows = (lax.broadcasted_iota(I32, (SUM_ROWS, KB), 0) == 0).astype(BF16)

    def value_phase(j, p_in, alphas, accs):
        new_accs = []
        for h in range(H_ATT):
            hrows = slice(h * DH_ATT, (h + 1) * DH_ATT)
            v_aug = jnp.concatenate([vT_ref[0, j, hrows, :], ones_rows], axis=0)
            pv = jnp.dot(v_aug, p_in[h], preferred_element_type=F32)
            new_accs.append(alphas[h] * accs[h] + pv)
        return tuple(new_accs)

    def block_step(j, next_j, s_in, s_out, p_prev, p_out, carry):
        ms, accs, block_max, alphas_prev = carry
        next_max = score_phase(next_j, s_out)
        new_ms, alphas = [], []
        for h in range(H_ATT):
            m_new = jnp.maximum(ms[h], block_max[h])
            m_safe = jnp.where(m_new == NEG_INF, 0.0, m_new)
            alphas.append(jnp.exp2(ms[h] - m_safe))
            new_ms.append(m_new)
            p_out[h] = jnp.exp2(s_in[h] - m_safe).astype(BF16)
        accs = value_phase(jnp.maximum(j - 1, 0), p_prev, alphas_prev, accs)
        return tuple(new_ms), accs, next_max, tuple(alphas)

    def attend_pair(jj, carry):
        j0 = 2 * jj
        carry = block_step(j0, j0 + 1, s_scr, s2_scr, p2_scr, p_scr, carry)
        return block_step(j0 + 1, jnp.minimum(j0 + 2, 2 * n_pairs - 1), s2_scr, s_scr, p_scr, p2_scr, carry)

    p2_scr[...] = jnp.zeros(p2_scr.shape, BF16)
    init = (tuple(jnp.full((1, TQ), NEG_INF, F32) for _ in range(H_ATT)),
            tuple(jnp.zeros((DH_ATT + SUM_ROWS, TQ), F32) for _ in range(H_ATT)),
            score_phase(0, s_scr),
            tuple(jnp.ones((1, TQ), F32) for _ in range(H_ATT)))
    _, accs, _, alphas = lax.fori_loop(0, n_pairs, attend_pair, init)
    accs = value_phase(2 * n_pairs - 1, p2_scr, alphas, accs)
    for h in range(H_ATT):
        oT_ref[h * DH_ATT:(h + 1) * DH_ATT, :] = accs[h][:DH_ATT] / accs[h][DH_ATT:DH_ATT + 1]
    o_ref[0] = oT_ref[...].T.astype(BF16)


def _dsa_prompt(qT, qiT, wT, kwb, kb, vTb):
    b, _, s = qT.shape
    assert s % (2 * KB) == 0 and TQ == KB
    topk = min(TOPK_MAX, s // 4)
    tri = (jnp.arange(KB)[:, None] > jnp.arange(KB)[None, :]).astype(BF16)
    feat = lambda w: pl.BlockSpec((1, w, TQ), lambda i, j: (i, 0, j))
    full = lambda w: pl.BlockSpec((1, s, w), lambda i, j: (i, 0, 0))
    return pl.pallas_call(
        functools.partial(_dsa_prompt_kernel, topk=topk),
        grid=(b, s // TQ),
        in_specs=[feat(HD), feat(HD), feat(H_IDX), full(LANES), full(HD),
                  pl.BlockSpec((1, s // KB, HD, KB), lambda i, j: (i, 0, 0, 0)), _const_spec(tri.shape)],
        out_specs=pl.BlockSpec((1, TQ, HD), lambda i, j: (i, j, 0)),
        out_shape=jax.ShapeDtypeStruct((b, s, HD), BF16),
        scratch_shapes=[pltpu.VMEM((s, TQ), F32), pltpu.VMEM((H_ATT, LANES, TQ), BF16),
                        pltpu.VMEM((H_IDX, LANES, TQ), BF16), pltpu.VMEM((HD, TQ), F32),
                        pltpu.VMEM((H_ATT, KB, TQ), F32), pltpu.VMEM((H_ATT, KB, TQ), F32),
                        pltpu.VMEM((H_ATT, KB, TQ), BF16), pltpu.VMEM((H_ATT, KB, TQ), BF16)],
        compiler_params=_params(2),
        name="dsa_prompt",
    )(qT, qiT, wT, kwb, kb, vTb, tri)


PAGE_GROUP_IDX = 32
PAGE_GROUP_KV = 32
PAGE_PARTIALS = 8


def _dsa_sample_select_kernel(pt_ref, qi_ref, kwq_ref, tri_ref, *rest, n_pages, topk, group):
    page_refs, (bias_ref, biasn_ref, qiall_ref, wb_ref) = rest[:group], rest[group:]
    pg = pl.program_id(1)
    t = qi_ref.shape[1]
    kwq = kwq_ref[0]

    @pl.when(pg == 0)
    def _():
        qi = qi_ref[0].astype(F32)
        qiall_ref[...] = jnp.concatenate(
            [qi[:, h * D_IDX:(h + 1) * D_IDX] for h in range(H_IDX)], axis=0).astype(BF16)
        for h in range(H_IDX):
            wb_ref[h] = jnp.broadcast_to(kwq[:, D_IDX + h:D_IDX + h + 1], (t, LANES))

    def scores(dots):
        acc = jnp.zeros((t, dots.shape[1]), F32)
        for h in range(H_IDX):
            acc = acc + jnp.maximum(dots[h * t:(h + 1) * t], 0.0) * wb_ref[h]
        return acc

    for g in range(group):
        dots = jnp.dot(qiall_ref[...], page_refs[g][0].astype(BF16), preferred_element_type=F32)
        bias_ref[0, pg * group + g] = scores(dots)

    @pl.when(pg == pl.num_programs(1) - 1)
    def _():
        new_keys = jnp.concatenate([kwq[:, :D_IDX], jnp.zeros((LANES - t, D_IDX), F32)], axis=0).astype(BF16)
        row = lax.broadcasted_iota(I32, (t, LANES), 0)
        col = lax.broadcasted_iota(I32, (t, LANES), 1)
        dots_new = lax.dot_general(qiall_ref[...], new_keys, NT_DIMS, preferred_element_type=F32)
        raw_new = scores(dots_new)
        sc_new = jnp.where(col <= row, raw_new, NEG_INF)

        def over_pages(fn, first, combine):
            parts = [first]
            for p in range(n_pages):
                v = fn(bias_ref[0, p])
                if len(parts) < PAGE_PARTIALS:
                    parts.append(v)
                else:
                    parts[p % PAGE_PARTIALS] = combine(parts[p % PAGE_PARTIALS], v)
            return functools.reduce(combine, parts)

        def count(pred):
            hit = lambda x: jnp.where(pred(x), 1.0, 0.0)
            return jnp.sum(over_pages(hit, hit(sc_new), jnp.add), axis=1, keepdims=True)

        def below_max(hi):
            below = lambda x: jnp.where(x < hi, x, NEG_INF)
            return jnp.max(over_pages(below, below(sc_new), jnp.maximum), axis=1, keepdims=True)

        ident = lambda x: x
        rmin = jnp.min(over_pages(ident, jnp.where(col <= row, raw_new, jnp.inf), jnp.minimum),
                       axis=1, keepdims=True)
        rmax = jnp.max(over_pages(ident, sc_new, jnp.maximum), axis=1, keepdims=True)
        n_valid = n_pages * bias_ref.shape[3] + lax.broadcasted_iota(I32, (t, 1), 0) + 1
        thr, n_tied = _kth_largest(count, below_max, rmin, rmax, n_valid, topk)
        as_bias = lambda sel: jnp.where(sel, 0.0, NEG_INF)

        @pl.when(n_tied == 0)
        def _():
            bias_ref[0] = as_bias(bias_ref[0] >= thr)
            biasn_ref[0] = as_bias(sc_new >= thr)

        @pl.when(n_tied > 0)
        def _():
            quota = topk - count(lambda x: x > thr)

            def select(x, ties_before):
                eq = x == thr
                eqf = jnp.where(eq, 1.0, 0.0)
                rank = jnp.dot(eqf.astype(BF16), tri_ref[...], preferred_element_type=F32) + ties_before
                sel = (x > thr) | (eq & (rank < quota))
                return as_bias(sel), ties_before + jnp.sum(eqf, axis=1, keepdims=True)

            def bias_page(p, ties_before):
                bias, ties = select(bias_ref[0, p], ties_before)
                bias_ref[0, p] = bias
                return ties

            ties = lax.fori_loop(0, n_pages, bias_page, jnp.zeros((t, 1), F32))
            biasn_ref[0], _ = select(sc_new, ties)


def _dsa_sample_select(page_table, qib, kw32, cache_kidx_t):
    b, t, _ = qib.shape
    n_pages = page_table.shape[1]
    page = cache_kidx_t.shape[2]
    topk = min(TOPK_MAX, (n_pages * page + t) // 4)
    group = min(PAGE_GROUP_IDX, n_pages)
    assert n_pages % group == 0
    tri = (jnp.arange(page)[:, None] < jnp.arange(page)[None, :]).astype(BF16)
    tok = lambda w: pl.BlockSpec((1, t, w), lambda i, j, pt: (i, 0, 0))
    page_specs = [pl.BlockSpec((1, D_IDX, page), lambda i, j, pt, g=g: (pt[i, j * group + g], 0, 0))
                  for g in range(group)]
    grid_spec = pltpu.PrefetchScalarGridSpec(
        num_scalar_prefetch=1,
        grid=(b, n_pages // group),
        in_specs=[tok(H_IDX * D_IDX), tok(LANES), pl.BlockSpec(tri.shape, lambda i, j, pt: (0, 0))] + page_specs,
        out_specs=[pl.BlockSpec((1, n_pages, t, page), lambda i, j, pt: (i, 0, 0, 0)),
                   pl.BlockSpec((1, t, LANES), lambda i, j, pt: (i, 0, 0))],
        scratch_shapes=[pltpu.VMEM((H_IDX * t, D_IDX), BF16), pltpu.VMEM((H_IDX, t, LANES), F32)],
    )
    return pl.pallas_call(
        functools.partial(_dsa_sample_select_kernel, n_pages=n_pages, topk=topk, group=group),
        grid_spec=grid_spec,
        out_shape=[jax.ShapeDtypeStruct((b, n_pages, t, page), F32), jax.ShapeDtypeStruct((b, t, LANES), F32)],
        compiler_params=pltpu.CompilerParams(dimension_semantics=("parallel", "arbitrary"),
                                             vmem_limit_bytes=VMEM_LIMIT),
        name="dsa_sample_select",
    )(page_table, qib, kw32, tri, *([cache_kidx_t] * group))


def _dsa_sample_attend_kernel(pt_ref, q_ref, kn_ref, vn_ref, bias_ref, biasn_ref, *rest, group):
    kT_refs, vT_refs = rest[:group], rest[group:2 * group]
    o_ref, m_ref, l_ref, acc_ref = rest[2 * group:]
    pg = pl.program_id(1)
    t = q_ref.shape[1]
    page = kT_refs[0].shape[2]
    lane_head = lax.broadcasted_iota(I32, (t, HD), 1) // DH_ATT
    q = q_ref[0].astype(F32)
    qbd = jnp.concatenate([jnp.where(lane_head == h, q, 0.0) for h in range(H_ATT)], axis=0).astype(BF16)

    @pl.when(pg == 0)
    def _():
        m_ref[...] = jnp.full(m_ref.shape, NEG_INF, F32)
        l_ref[...] = jnp.zeros(l_ref.shape, F32)
        acc_ref[...] = jnp.zeros(acc_ref.shape, F32)

    def update(s, bias, pv):
        s = s + jnp.concatenate([bias] * H_ATT, axis=0)
        m = m_ref[...]
        m_new = jnp.maximum(m, jnp.max(s, axis=1, keepdims=True))
        m_safe = jnp.where(m_new == NEG_INF, 0.0, m_new)
        alpha = jnp.exp(m - m_safe)
        p = jnp.exp(s - m_safe)
        l_ref[...] = alpha * l_ref[...] + jnp.sum(p, axis=1, keepdims=True)
        acc_ref[...] = alpha * acc_ref[...] + pv(p.astype(BF16))
        m_ref[...] = m_new

    s_pages = jnp.concatenate(
        [jnp.dot(qbd, kT_refs[g][0].astype(BF16), preferred_element_type=F32) for g in range(group)], axis=1)
    bias_pages = jnp.concatenate([bias_ref[0, g] for g in range(group)], axis=1)

    def pv_pages(p):
        out = jnp.zeros((H_ATT * t, HD), F32)
        for g in range(group):
            out = out + lax.dot_general(p[:, g * page:(g + 1) * page], vT_refs[g][0].astype(BF16), NT_DIMS,
                                        preferred_element_type=F32)
        return out

    update(s_pages, bias_pages, pv_pages)

    @pl.when(pg == pl.num_programs(1) - 1)
    def _():
        pad = jnp.zeros((LANES - t, HD), F32)
        kn = jnp.concatenate([kn_ref[0], pad], axis=0).astype(BF16)
        vn = jnp.concatenate([vn_ref[0], pad], axis=0).astype(BF16)
        update(lax.dot_general(qbd, kn, NT_DIMS, preferred_element_type=F32), biasn_ref[0],
               lambda p: jnp.dot(p, vn, preferred_element_type=F32))
        o = acc_ref[...] / l_ref[...]
        out = jnp.zeros((t, HD), F32)
        for h in range(H_ATT):
            out = out + jnp.where(lane_head == h, o[h * t:(h + 1) * t], 0.0)
        o_ref[0] = out.astype(BF16)


def _dsa_sample_attend(page_table, qb, k32, v32, bias, bias_new, cache_kt, cache_vt):
    b, t, _ = qb.shape
    n_pages = page_table.shape[1]
    page = cache_kt.shape[2]
    group = min(PAGE_GROUP_KV, n_pages)
    assert n_pages % group == 0
    tok = lambda w: pl.BlockSpec((1, t, w), lambda i, j, pt: (i, 0, 0))
    kv_specs = [pl.BlockSpec((1, HD, page), lambda i, j, pt, g=g: (pt[i, j * group + g], 0, 0))
                for g in range(group)]
    grid_spec = pltpu.PrefetchScalarGridSpec(
        num_scalar_prefetch=1,
        grid=(b, n_pages // group),
        in_specs=[tok(HD), tok(HD), tok(HD),
                  pl.BlockSpec((1, group, t, page), lambda i, j, pt: (i, j, 0, 0)),
                  tok(LANES)] + kv_specs + kv_specs,
        out_specs=tok(HD),
        scratch_shapes=[pltpu.VMEM((H_ATT * t, 1), F32), pltpu.VMEM((H_ATT * t, 1), F32),
                        pltpu.VMEM((H_ATT * t, HD), F32)],
    )
    return pl.pallas_call(
        functools.partial(_dsa_sample_attend_kernel, group=group),
        grid_spec=grid_spec,
        out_shape=jax.ShapeDtypeStruct((b, t, HD), BF16),
        compiler_params=pltpu.CompilerParams(dimension_semantics=("parallel", "arbitrary"),
                                             vmem_limit_bytes=VMEM_LIMIT),
        name="dsa_sample_attend",
    )(page_table, qb, k32, v32, bias, bias_new, *([cache_kt] * group), *([cache_vt] * group))


def _merge_kernel(x_ref, oa_ref, ol_ref, om_ref, g_ref, wg_ref, wa_ref, wl_ref, wm_ref, wo_ref, o_ref):
    x = x_ref[...]
    hn = _rms(x, g_ref[...]).astype(BF16)
    m = jnp.zeros_like(x)
    for idx, (o_r, w_r) in enumerate(((oa_ref, wa_ref), (ol_ref, wl_ref), (om_ref, wm_ref))):
        gate = jax.nn.sigmoid(jnp.dot(hn, wg_ref[:, idx * D_MODEL:(idx + 1) * D_MODEL], preferred_element_type=F32))
        m = m + gate * jnp.dot(o_r[...], w_r[...], preferred_element_type=F32)
    o_ref[...] = x + jnp.dot(m.astype(BF16), wo_ref[...], preferred_element_type=F32)


def _merge(x, o_att, o_lru, o_mem, consts, tm):
    n = x.shape[0]
    row = lambda w: pl.BlockSpec((tm, w), lambda i: (i, 0))
    return pl.pallas_call(
        _merge_kernel,
        grid=(n // tm,),
        in_specs=[row(D_MODEL), row(HD), row(LRU_W), row(MD)] + [_const_spec(c.shape) for c in consts],
        out_specs=row(D_MODEL),
        out_shape=jax.ShapeDtypeStruct((n, D_MODEL), F32),
        compiler_params=_params(1),
        name="merge",
    )(x, o_att, o_lru, o_mem, *consts)


def _token_tile(n):
    return min(n, 512)


def _layer(x, is_prompt, lw, pp, extra):
    b, t, _ = x.shape
    n = b * t
    tm = _token_tile(n)
    x1 = _ffn(x.reshape(n, D_MODEL), pp["ffn1"], tm)
    r3 = lambda a: a.reshape(b, t, a.shape[-1])
    if is_prompt:
        kT32, vT32, kiT32, wT, qT, qiT, kb, kwb, vTb, lx, lg, qmb = _proj_t(r3(x1), pp["proj"], tm)
        o_att = _dsa_prompt(qT, qiT, wT, kwb, kb, vTb)
        k_new = kT32.reshape(b, H_ATT, DH_ATT, t).transpose(0, 3, 1, 2)
        v_new = vT32.reshape(b, H_ATT, DH_ATT, t).transpose(0, 3, 1, 2)
        ki_new = kiT32.transpose(0, 2, 1)
        conv_state = jnp.zeros((b, CONV_W - 1, LRU_W), F32)
        h0 = jnp.zeros((b, LRU_W), F32)
        mem = extra["mem"]
        mk, mv = _memkv(mem.reshape(-1, D_MODEL), lw["norm_mem_g"], lw["w_mem_kv"], lw["mem_k_norm_g"],
                        pp["proj"]["g128"])
        mk = mk.reshape(b, -1, MD)
        mv = mv.reshape(b, -1, MD)
    else:
        k32, v32, kw32, qb, _, _, qib, _, lx, lg, qmb = _proj(x1, pp["proj"], tm)
        pt = extra["page_table"]
        bias, bias_new = _dsa_sample_select(pt, r3(qib), r3(kw32), extra["cache_kidx_t"])
        o_att = _dsa_sample_attend(pt, r3(qb), r3(k32), r3(v32), bias, bias_new, extra["cache_kt"],
                                   extra["cache_vt"])
        k_new = k32.reshape(b, t, H_ATT, DH_ATT)
        v_new = v32.reshape(b, t, H_ATT, DH_ATT)
        ki_new = r3(kw32)[:, :, :D_IDX]
        conv_state, h0 = extra["state_conv"], extra["state_h"]
        mk, mv = extra["cache_mem_k"], extra["cache_mem_v"]
    o_lru, conv_buf, h_last = _rglru(r3(lx), r3(lg), conv_state, h0, pp["rglru"], min(t, 256))
    o_mem = _memattn(r3(qmb), mk, mv, min(t, 512))
    x2 = _merge(x1, o_att.reshape(n, HD), o_lru.reshape(n, LRU_W), o_mem.reshape(n, MD), pp["merge"], tm)
    y = _ffn(x2, pp["ffn2"], tm).reshape(b, t, D_MODEL)
    if is_prompt:
        state = (k_new, v_new, ki_new, mk.reshape(b, -1, H_MEM, DH_MEM), mv.reshape(b, -1, H_MEM, DH_MEM),
                 conv_buf, h_last)
    else:
        state = (k_new, v_new, ki_new, conv_buf, h_last)
    return y, state


def kernel(x_prompt, x_sample, cache_k, cache_v, cache_kidx, page_table, cache_mem_k, cache_mem_v, state_conv, state_h, mem_prompt, norm_ffn1_g, w_ffn1_in, w_ffn1_out, norm_mix_g, w_in, q_norm_g, k_norm_g, w_attn_o, conv_w, conv_b, lru_wa, lru_ba, lru_wi, lru_bi, lru_lambda, w_lru_o, norm_mem_g, w_mem_kv, mem_q_norm_g, mem_k_norm_g, w_mem_o, w_out, norm_ffn2_g, w_ffn2_in, w_ffn2_out):
    depth = w_in.shape[0]
    n_phys, page = cache_k.shape[1], cache_k.shape[2]
    xp, xs = x_prompt, x_sample
    p_states, s_states = [], []
    for l in range(depth):
        proj = _prep_proj(norm_mix_g[l], w_in[l], q_norm_g[l], k_norm_g[l], mem_q_norm_g[l])
        pp = dict(
            ffn1=_prep_ffn(norm_ffn1_g[l], w_ffn1_in[l], w_ffn1_out[l]),
            ffn2=_prep_ffn(norm_ffn2_g[l], w_ffn2_in[l], w_ffn2_out[l]),
            proj=proj,
            rglru=_prep_rglru(conv_w[l], conv_b[l], lru_wa[l], lru_ba[l], lru_wi[l], lru_bi[l], lru_lambda[l]),
            merge=[proj["g"], proj["wgates"], w_attn_o[l].astype(BF16), w_lru_o[l].astype(BF16),
                   w_mem_o[l].astype(BF16), w_out[l].astype(BF16)],
        )
        lw = dict(norm_mem_g=norm_mem_g[l], w_mem_kv=w_mem_kv[l], mem_k_norm_g=mem_k_norm_g[l])
        xp, st_p = _layer(xp, True, lw, pp, dict(mem=mem_prompt))
        xs, st_s = _layer(xs, False, lw, pp, dict(
            page_table=page_table,
            cache_kt=cache_k[l].transpose(0, 2, 3, 1).reshape(n_phys, HD, page),
            cache_vt=cache_v[l].transpose(0, 2, 3, 1).reshape(n_phys, HD, page),
            cache_kidx_t=cache_kidx[l].transpose(0, 2, 1), cache_mem_k=cache_mem_k[l].reshape(-1, cache_mem_k.shape[2], MD),
            cache_mem_v=cache_mem_v[l].reshape(-1, cache_mem_v.shape[2], MD),
            state_conv=state_conv[l], state_h=state_h[l]))
        p_states.append(st_p)
        s_states.append(st_s)
    stack = lambda states, i: jnp.stack([s[i] for s in states])
    return (xp, xs) + tuple(stack(p_states, i) for i in range(7)) + tuple(stack(s_states, i) for i in range(5))
```

```python
import functools

import jax
import jax.numpy as jnp
from jax import lax
from jax.experimental import pallas as pl
from jax.experimental.pallas import tpu as pltpu

F32, BF16, I32 = jnp.float32, jnp.bfloat16, jnp.int32

D_MODEL = 1024
H_ATT, DH_ATT = 8, 64
H_IDX, D_IDX = 8, 64
TOPK_MAX = 256
LRU_W, LRU_BLOCKS, CONV_W, LRU_C = 512, 8, 4, 8.0
H_MEM, DH_MEM = 4, 128
D_FF = 2816
EPS = 1e-6
HD = H_ATT * DH_ATT
MD = H_MEM * DH_MEM

LANES = 128
SUBLANES = 8
VMEM_BYTES_V7X = 64 * 1024 * 1024
VMEM_LIMIT = VMEM_BYTES_V7X - 8 * 1024 * 1024

FF_CHUNK = 256
TQ = 256
KB = 256
SCORE_BLOCKS = (4, 2, 1)
CNT_ROWS = 32
SUM_ROWS = 16
BISECT_STEPS = 14
LOG2E = 1.4426950408889634
NEG_INF = float("-inf")

NT_DIMS = (((1,), (1,)), ((), ()))


def _params(n_grid, parallel=True):
    sem = ("parallel" if parallel else "arbitrary",) * n_grid
    return pltpu.CompilerParams(dimension_semantics=sem, vmem_limit_bytes=VMEM_LIMIT)


def _const_spec(shape):
    nd = len(shape)
    return pl.BlockSpec(shape, lambda *_: (0,) * nd)


def _rms(x, g):
    ms = jnp.mean(x * x, axis=-1, keepdims=True)
    return x * lax.rsqrt(ms + EPS) * g


def _group_rms(x, gmat, g, group):
    x2 = x * x
    hi = x2.astype(BF16)
    lo = (x2 - hi.astype(F32)).astype(BF16)
    ss = jnp.dot(hi, gmat, preferred_element_type=F32) + jnp.dot(lo, gmat, preferred_element_type=F32)
    return x * lax.rsqrt(ss * (1.0 / group) + EPS) * g


def _group_matrix(width, group):
    idx = jnp.arange(width) // group
    return (idx[:, None] == idx[None, :]).astype(BF16)


def _ffn_kernel(x_ref, g_ref, wg_ref, wu_ref, wo_ref, o_ref):
    x = x_ref[...]
    hn = _rms(x, g_ref[...]).astype(BF16)
    acc = jnp.zeros_like(x)
    for c in range(wg_ref.shape[0]):
        gate = jnp.dot(hn, wg_ref[c], preferred_element_type=F32)
        up = jnp.dot(hn, wu_ref[c], preferred_element_type=F32)
        act = (gate * jax.nn.sigmoid(gate) * up).astype(BF16)
        acc = acc + jnp.dot(act, wo_ref[c], preferred_element_type=F32)
    o_ref[...] = x + 0.5 * acc


def _prep_ffn(g, w_in, w_out):
    nc = D_FF // FF_CHUNK
    wg = w_in[:, :D_FF].reshape(D_MODEL, nc, FF_CHUNK).transpose(1, 0, 2).astype(BF16)
    wu = w_in[:, D_FF:].reshape(D_MODEL, nc, FF_CHUNK).transpose(1, 0, 2).astype(BF16)
    wo = w_out.reshape(nc, FF_CHUNK, D_MODEL).astype(BF16)
    return g.reshape(1, D_MODEL), wg, wu, wo


def _ffn(x, prep, tm):
    g, wg, wu, wo = prep
    n = x.shape[0]
    row = pl.BlockSpec((tm, D_MODEL), lambda i: (i, 0))
    return pl.pallas_call(
        _ffn_kernel,
        grid=(n // tm,),
        in_specs=[row, _const_spec(g.shape), _const_spec(wg.shape), _const_spec(wu.shape), _const_spec(wo.shape)],
        out_specs=row,
        out_shape=jax.ShapeDtypeStruct((n, D_MODEL), F32),
        compiler_params=_params(1),
        name="ffn",
    )(x, g, wg, wu, wo)


def _proj_kernel(x_ref, g_ref, wqkv_ref, wqi_ref, wkw_ref, wl_ref, wqm_ref, qg_ref, kg_ref, mg_ref,
                 kwscale_ref, g64_ref, g128_ref,
                 k32_ref, v32_ref, kw32_ref, qb_ref, kb_ref, vb_ref, qib_ref, kwb_ref, lx_ref, lg_ref, qmb_ref):
    hn = _rms(x_ref[...], g_ref[...]).astype(BF16)
    qkv = jnp.dot(hn, wqkv_ref[...], preferred_element_type=F32)
    q = _group_rms(qkv[:, :HD], g64_ref[...], qg_ref[...], DH_ATT)
    k = _group_rms(qkv[:, HD:2 * HD], g64_ref[...], kg_ref[...], DH_ATT)
    v = qkv[:, 2 * HD:]
    k32_ref[...] = k
    v32_ref[...] = v
    qb_ref[...] = (q * (DH_ATT ** -0.5)).astype(BF16)
    kb_ref[...] = k.astype(BF16)
    vb_ref[...] = v.astype(BF16)
    qib_ref[...] = jnp.dot(hn, wqi_ref[...], preferred_element_type=F32).astype(BF16)
    kw = jnp.dot(hn, wkw_ref[...], preferred_element_type=F32) * kwscale_ref[...]
    kw32_ref[...] = kw
    kwb_ref[...] = kw.astype(BF16)
    lxg = jnp.dot(hn, wl_ref[...], preferred_element_type=F32)
    lx_ref[...] = lxg[:, :LRU_W]
    lg_ref[...] = lxg[:, LRU_W:]
    qm = jnp.dot(hn, wqm_ref[...], preferred_element_type=F32)
    qmb_ref[...] = _group_rms(qm, g128_ref[...], mg_ref[...], DH_MEM).astype(BF16)


def _prep_proj(norm_g, w_in, q_norm_g, k_norm_g, mem_q_norm_g):
    o = 0
    cols = {}
    for name, size in (("q", HD), ("k", HD), ("v", HD), ("qi", H_IDX * D_IDX), ("ki", D_IDX), ("wi", H_IDX),
                       ("lx", LRU_W), ("lg", LRU_W), ("qm", MD), ("gates", 3 * D_MODEL)):
        cols[name] = w_in[:, o:o + size]
        o += size
    pad = jnp.zeros((D_MODEL, LANES - D_IDX - H_IDX), w_in.dtype)
    wqkv = jnp.concatenate([cols["q"], cols["k"], cols["v"]], axis=1).astype(BF16)
    wkw = jnp.concatenate([cols["ki"], cols["wi"], pad], axis=1).astype(BF16)
    wl = jnp.concatenate([cols["lx"], cols["lg"]], axis=1).astype(BF16)
    kwscale = jnp.concatenate([jnp.ones((D_IDX,), F32),
                               jnp.full((H_IDX,), H_IDX ** -0.5 * D_IDX ** -0.5, F32),
                               jnp.zeros((LANES - D_IDX - H_IDX,), F32)]).reshape(1, LANES)
    return dict(
        g=norm_g.reshape(1, D_MODEL), wqkv=wqkv, wqi=cols["qi"].astype(BF16), wkw=wkw, wl=wl,
        wqm=cols["qm"].astype(BF16),
        qg=jnp.tile(q_norm_g, H_ATT).reshape(1, HD), kg=jnp.tile(k_norm_g, H_ATT).reshape(1, HD),
        mg=jnp.tile(mem_q_norm_g, H_MEM).reshape(1, MD), kwscale=kwscale,
        g64=_group_matrix(HD, DH_ATT), g128=_group_matrix(MD, DH_MEM),
        wgates=cols["gates"].astype(BF16),
        wqkT=wqkv[:, :2 * HD].T, wvT=wqkv[:, 2 * HD:].T, wqiT=cols["qi"].astype(BF16).T, wkwT=wkw.T,
        qg_col=jnp.tile(q_norm_g, H_ATT).reshape(HD, 1), kg_col=jnp.tile(k_norm_g, H_ATT).reshape(HD, 1),
        kwscale_col=kwscale.reshape(LANES, 1),
    )


def _group_rms_t(xt, gmat, g, group):
    x2 = xt * xt
    hi = x2.astype(BF16)
    lo = (x2 - hi.astype(F32)).astype(BF16)
    ss = jnp.dot(gmat, hi, preferred_element_type=F32) + jnp.dot(gmat, lo, preferred_element_type=F32)
    return xt * lax.rsqrt(ss * (1.0 / group) + EPS) * g


def _proj_t_kernel(x_ref, g_ref, wqkT_ref, wvT_ref, wqiT_ref, wkwT_ref, wl_ref, wqm_ref, qgT_ref, kgT_ref, mg_ref,
                   kwscaleT_ref, g64_ref, g128_ref,
                   kT_ref, vT_ref, kiT_ref, wT_ref, qT_ref, qiT_ref, kb_ref, kwb_ref, vTb_ref, lx_ref, lg_ref,
                   qmb_ref):
    hn = _rms(x_ref[0], g_ref[...]).astype(BF16)
    nt = lambda w_ref: lax.dot_general(w_ref[...], hn, NT_DIMS, preferred_element_type=F32)
    qkT = nt(wqkT_ref)
    qT = _group_rms_t(qkT[:HD], g64_ref[...], qgT_ref[...], DH_ATT)
    kT = _group_rms_t(qkT[HD:], g64_ref[...], kgT_ref[...], DH_ATT)
    kT_ref[0] = kT
    kb_ref[0] = kT.T.astype(BF16)
    qT_ref[0] = (qT * (DH_ATT ** -0.5 * LOG2E)).astype(BF16)
    vT = nt(wvT_ref)
    vT_ref[0] = vT
    for c in range(vTb_ref.shape[1]):
        vTb_ref[0, c] = vT[:, c * KB:(c + 1) * KB].astype(BF16)
    qiT_ref[0] = nt(wqiT_ref).astype(BF16)
    kwT = nt(wkwT_ref) * kwscaleT_ref[...]
    kiT_ref[0] = kwT[:D_IDX]
    wT_ref[0] = kwT[D_IDX:D_IDX + H_IDX]
    kwb_ref[0] = kwT.T.astype(BF16)
    lxg = jnp.dot(hn, wl_ref[...], preferred_element_type=F32)
    lx_ref[0] = lxg[:, :LRU_W]
    lg_ref[0] = lxg[:, LRU_W:]
    qm = jnp.dot(hn, wqm_ref[...], preferred_element_type=F32)
    qmb_ref[0] = _group_rms(qm, g128_ref[...], mg_ref[...], DH_MEM).astype(BF16)


def _proj_t(x, p, tm):
    b, s, _ = x.shape
    bc = lambda col: jnp.broadcast_to(col, (col.shape[0], tm))
    consts = [p["g"], p["wqkT"], p["wvT"], p["wqiT"], p["wkwT"], p["wl"], p["wqm"], bc(p["qg_col"]),
              bc(p["kg_col"]), p["mg"], bc(p["kwscale_col"]), p["g64"], p["g128"]]
    tok = lambda w: pl.BlockSpec((1, tm, w), lambda i, j: (i, j, 0))
    feat = lambda w: pl.BlockSpec((1, w, tm), lambda i, j: (i, 0, j))
    outs = [
        (feat(HD), (b, HD, s), F32), (feat(HD), (b, HD, s), F32), (feat(D_IDX), (b, D_IDX, s), F32),
        (feat(H_IDX), (b, H_IDX, s), F32), (feat(HD), (b, HD, s), BF16), (feat(HD), (b, HD, s), BF16),
        (tok(HD), (b, s, HD), BF16), (tok(LANES), (b, s, LANES), BF16),
        (pl.BlockSpec((1, tm // KB, HD, KB), lambda i, j: (i, j, 0, 0)), (b, s // KB, HD, KB), BF16),
        (tok(LRU_W), (b, s, LRU_W), F32), (tok(LRU_W), (b, s, LRU_W), F32), (tok(MD), (b, s, MD), BF16),
    ]
    return pl.pallas_call(
        _proj_t_kernel,
        grid=(b, s // tm),
        in_specs=[tok(D_MODEL)] + [_const_spec(c.shape) for c in consts],
        out_specs=[o[0] for o in outs],
        out_shape=[jax.ShapeDtypeStruct(o[1], o[2]) for o in outs],
        compiler_params=_params(2),
        name="proj_t",
    )(x, *consts)


def _proj(x, p, tm):
    n = x.shape[0]
    consts = [p[k] for k in ("g", "wqkv", "wqi", "wkw", "wl", "wqm", "qg", "kg", "mg", "kwscale", "g64", "g128")]

    def row(w):
        return pl.BlockSpec((tm, w), lambda i: (i, 0))

    outs = [(HD, F32), (HD, F32), (LANES, F32), (HD, BF16), (HD, BF16), (HD, BF16), (HD, BF16), (LANES, BF16),
            (LRU_W, F32), (LRU_W, F32), (MD, BF16)]
    return pl.pallas_call(
        _proj_kernel,
        grid=(n // tm,),
        in_specs=[row(D_MODEL)] + [_const_spec(c.shape) for c in consts],
        out_specs=[row(w) for w, _ in outs],
        out_shape=[jax.ShapeDtypeStruct((n, w), dt) for w, dt in outs],
        compiler_params=_params(1),
        name="proj",
    )(x, *consts)


def _memkv_kernel(m_ref, g_ref, w_ref, kg_ref, g128_ref, mk_ref, mv_ref):
    hn = _rms(m_ref[...], g_ref[...]).astype(BF16)
    kv = jnp.dot(hn, w_ref[...], preferred_element_type=F32)
    mk_ref[...] = _group_rms(kv[:, :MD], g128_ref[...], kg_ref[...], DH_MEM)
    mv_ref[...] = kv[:, MD:]


def _memkv(mem, norm_g, w_mem_kv, mem_k_norm_g, g128):
    n = mem.shape[0]
    tm = min(n, 512)
    consts = [norm_g.reshape(1, D_MODEL), w_mem_kv.astype(BF16), jnp.tile(mem_k_norm_g, H_MEM).reshape(1, MD), g128]
    row = lambda w: pl.BlockSpec((tm, w), lambda i: (i, 0))
    return pl.pallas_call(
        _memkv_kernel,
        grid=(n // tm,),
        in_specs=[row(D_MODEL)] + [_const_spec(c.shape) for c in consts],
        out_specs=[row(MD), row(MD)],
        out_shape=[jax.ShapeDtypeStruct((n, MD), F32)] * 2,
        compiler_params=_params(1),
        name="memkv",
    )(mem, *consts)


def _memattn_kernel(q_ref, mk_ref, mv_ref, o_ref):
    q = q_ref[0]
    mk = mk_ref[0].astype(BF16)
    mv = mv_ref[0].astype(BF16)
    for h in range(H_MEM):
        sl = slice(h * DH_MEM, (h + 1) * DH_MEM)
        s = lax.dot_general(q[:, sl], mk[:, sl], NT_DIMS, preferred_element_type=F32) * (DH_MEM ** -0.5)
        m = jnp.max(s, axis=-1, keepdims=True)
        e = jnp.exp(s - m)
        p = (e / jnp.sum(e, axis=-1, keepdims=True)).astype(BF16)
        o_ref[0, :, sl] = jnp.dot(p, mv[:, sl], preferred_element_type=F32).astype(BF16)


def _memattn(qm, mk, mv, tm):
    b, t, _ = qm.shape
    n_mem = mk.shape[1]
    return pl.pallas_call(
        _memattn_kernel,
        grid=(b, t // tm),
        in_specs=[pl.BlockSpec((1, tm, MD), lambda i, j: (i, j, 0)),
                  pl.BlockSpec((1, n_mem, MD), lambda i, j: (i, 0, 0)),
                  pl.BlockSpec((1, n_mem, MD), lambda i, j: (i, 0, 0))],
        out_specs=pl.BlockSpec((1, tm, MD), lambda i, j: (i, j, 0)),
        out_shape=jax.ShapeDtypeStruct((b, t, MD), BF16),
        compiler_params=_params(2),
        name="memattn",
    )(qm, mk, mv)


def _rglru_kernel(lx_ref, lg_ref, cs_ref, h0_ref, cw_ref, cb_ref, wa_ref, ba_ref, wi_ref, bi_ref, lam_ref,
                  y_ref, nb_ref, hl_ref, tail_ref, h_ref, xe_ref):
    t = pl.program_id(1)

    @pl.when(t == 0)
    def _():
        tail_ref[...] = cs_ref[0]
        h_ref[...] = h0_ref[0]

    x = lx_ref[0]
    tt = x.shape[0]
    xe_ref[:SUBLANES] = tail_ref[...]
    xe_ref[SUBLANES:] = x
    conv = cb_ref[...] + x * cw_ref[CONV_W - 1:CONV_W, :]
    for d in range(1, CONV_W):
        conv = conv + xe_ref[SUBLANES - d:SUBLANES - d + tt, :] * cw_ref[CONV_W - 1 - d:CONV_W - d, :]
    tail_ref[...] = x[tt - SUBLANES:]
    nb_ref[0] = x[tt - SUBLANES:]

    cb16 = conv.astype(BF16)
    r = jax.nn.sigmoid(jnp.dot(cb16, wa_ref[...], preferred_element_type=F32) + ba_ref[...])
    ig = jax.nn.sigmoid(jnp.dot(cb16, wi_ref[...], preferred_element_type=F32) + bi_ref[...])
    nl = -lam_ref[...]
    softplus = jnp.maximum(nl, 0.0) + jnp.log1p(jnp.exp(-jnp.abs(nl)))
    log_a = -LRU_C * r * softplus
    a = jnp.exp(log_a)
    b = jnp.sqrt(-jnp.tanh(log_a) * (a * a + 1.0)) * (ig * conv)
    in_group = lax.broadcasted_iota(I32, x.shape, 0) % SUBLANES
    k = 1
    while k < SUBLANES:
        keep = in_group >= k
        b = a * jnp.where(keep, pltpu.roll(b, k, 0), 0.0) + b
        a = a * jnp.where(keep, pltpu.roll(a, k, 0), 1.0)
        k *= 2
    h_prev = h_ref[SUBLANES - 1:SUBLANES, :]
    groups = []
    for g in range(tt // SUBLANES):
        rows = slice(g * SUBLANES, (g + 1) * SUBLANES)
        groups.append(b[rows] + a[rows] * h_prev)
        h_prev = groups[-1][SUBLANES - 1:SUBLANES, :]
    h = groups[0] if len(groups) == 1 else jnp.concatenate(groups, axis=0)
    h_ref[...] = h[tt - SUBLANES:]
    hl_ref[0] = h[tt - SUBLANES:]
    y_ref[0] = (h * jax.nn.gelu(lg_ref[0])).astype(BF16)


def _block_diag(w):
    nb, bs, _ = w.shape
    eye = jnp.eye(nb, dtype=w.dtype)
    return (eye[:, None, :, None] * w[:, :, None, :]).reshape(nb * bs, nb * bs)


def _prep_rglru(conv_w, conv_b, lru_wa, lru_ba, lru_wi, lru_bi, lru_lambda):
    r = lambda v: v.reshape(1, LRU_W)
    return [conv_w, r(conv_b), _block_diag(lru_wa).astype(BF16), r(lru_ba), _block_diag(lru_wi).astype(BF16),
            r(lru_bi), r(lru_lambda)]


def _rglru(lx, lg, conv_state, h0, consts, tt):
    b, t, w = lx.shape
    cs = jnp.concatenate([jnp.zeros((b, SUBLANES - (CONV_W - 1), w), F32), conv_state], axis=1)
    h0p = jnp.concatenate([jnp.zeros((b, SUBLANES - 1, w), F32), h0[:, None, :]], axis=1)
    seq = pl.BlockSpec((1, tt, w), lambda i, j: (i, j, 0))
    st = pl.BlockSpec((1, SUBLANES, w), lambda i, j: (i, 0, 0))
    y, nb, hl = pl.pallas_call(
        _rglru_kernel,
        grid=(b, t // tt),
        in_specs=[seq, seq, st, st] + [_const_spec(c.shape) for c in consts],
        out_specs=[seq, st, st],
        out_shape=[jax.ShapeDtypeStruct((b, t, w), BF16), jax.ShapeDtypeStruct((b, SUBLANES, w), F32),
                   jax.ShapeDtypeStruct((b, SUBLANES, w), F32)],
        scratch_shapes=[pltpu.VMEM((SUBLANES, w), F32), pltpu.VMEM((SUBLANES, w), F32),
                        pltpu.VMEM((SUBLANES + tt, w), F32)],
        compiler_params=pltpu.CompilerParams(dimension_semantics=("parallel", "arbitrary"),
                                             vmem_limit_bytes=VMEM_LIMIT),
        name="rglru",
    )(lx, lg, cs, h0p, *consts)
    return y, nb[:, SUBLANES - (CONV_W - 1):], hl[:, SUBLANES - 1]


def _kth_largest(count, below_max, rmin, rmax, n_valid, topk):
    kf = float(topk)
    n_open = lambda done: jnp.sum(1.0 - done).astype(I32)

    def bisect(_, c):
        lo, hi, done = c
        mid = 0.5 * lo + 0.5 * jnp.minimum(hi, rmax)
        c_mid = count(lambda x: x >= mid)
        ge = c_mid >= kf
        live = done < 0.5
        lo = jnp.where(live & ge, mid, lo)
        hi = jnp.where(live & jnp.logical_not(ge), mid, hi)
        done = jnp.maximum(done, jnp.where(c_mid == kf, 1.0, 0.0))
        return lo, hi, done

    done0 = jnp.where(n_valid <= topk, 1.0, 0.0)
    lo, hi, done = lax.fori_loop(0, BISECT_STEPS, bisect, (rmin, jnp.full(rmin.shape, jnp.inf, F32), done0))

    def step_down(c):
        lo, hi, done, tied, _ = c
        cand = below_max(hi)
        c_cand = count(lambda x: x >= cand)
        ok = c_cand >= kf
        live = done < 0.5
        lo = jnp.where(live & ok, cand, lo)
        hi = jnp.where(live & jnp.logical_not(ok), cand, hi)
        tied = jnp.maximum(tied, jnp.where(live & (c_cand > kf), 1.0, 0.0))
        done = jnp.maximum(done, jnp.where(ok, 1.0, 0.0))
        return lo, hi, done, tied, n_open(done)

    thr, _, _, tied, _ = lax.while_loop(lambda c: c[4] > 0, step_down,
                                        (lo, hi, done, jnp.zeros(rmin.shape, F32), n_open(done)))
    return thr, jnp.sum(tied).astype(I32)


def _dsa_prompt_kernel(qT_ref, qiT_ref, wT_ref, kw_ref, k_ref, vT_ref, tri_ref, o_ref,
                       sc_ref, qpad_ref, qipad_ref, oT_ref, s_scr, s2_scr, p_scr, p2_scr, *, topk):
    i = pl.program_id(1)
    nk = i + 1
    kf = float(topk)
    kblock = lambda j: pl.ds(pl.multiple_of(j * KB, KB), KB)

    zeros64 = jnp.zeros((D_IDX, TQ), BF16)
    for h in range(H_IDX):
        qipad_ref[h] = jnp.concatenate([qiT_ref[0, h * D_IDX:(h + 1) * D_IDX, :], zeros64], axis=0)
    for h in range(H_ATT):
        qh = qT_ref[0, h * DH_ATT:(h + 1) * DH_ATT, :]
        qpad_ref[h] = jnp.concatenate([qh, zeros64] if h % 2 == 0 else [zeros64, qh], axis=0)
    wT = wT_ref[0]

    def block_scores(j, n_blocks):
        rows = pl.ds(pl.multiple_of(j * KB, KB), n_blocks * KB)
        kw = kw_ref[0, rows, :]
        acc = jnp.zeros((n_blocks * KB, TQ), F32)
        for h in range(H_IDX):
            d = jnp.dot(kw, qipad_ref[h], preferred_element_type=F32)
            acc = acc + jnp.maximum(d, 0.0) * wT[h:h + 1, :]
        return rows, acc

    def score_blocks(n_blocks):
        def body(jj, carry):
            lo, hi = carry
            rows, acc = block_scores(jj * n_blocks, n_blocks)
            sc_ref[rows, :] = acc
            return (jnp.minimum(lo, jnp.min(acc, axis=0, keepdims=True)),
                    jnp.maximum(hi, jnp.max(acc, axis=0, keepdims=True)))
        return body

    bounds = (jnp.full((1, TQ), jnp.inf, F32), jnp.full((1, TQ), NEG_INF, F32))
    first = 0
    for n_blocks in SCORE_BLOCKS:
        trips = (i - first) // n_blocks
        bounds = lax.fori_loop(first // n_blocks, first // n_blocks + trips, score_blocks(n_blocks), bounds)
        first = first + trips * n_blocks
    rmin, rmax = bounds
    _, acc = block_scores(i, 1)
    krow = lax.broadcasted_iota(I32, (KB, TQ), 0)
    qcol = lax.broadcasted_iota(I32, (KB, TQ), 1)
    sc_ref[kblock(i), :] = jnp.where(krow <= qcol, acc, NEG_INF)
    rmin = jnp.minimum(rmin, jnp.min(acc, axis=0, keepdims=True))
    rmax = jnp.maximum(rmax, jnp.max(acc, axis=0, keepdims=True))

    def count(pred):
        def body(j, cnt):
            m = jnp.where(pred(sc_ref[kblock(j), :]), 1.0, 0.0)
            for r in range(KB // CNT_ROWS):
                cnt = cnt + m[r * CNT_ROWS:(r + 1) * CNT_ROWS]
            return cnt

        cnt = lax.fori_loop(0, nk, body, jnp.zeros((CNT_ROWS, TQ), F32))
        return jnp.sum(cnt, axis=0, keepdims=True)

    def below_max(hi):
        def body(j, best):
            x = sc_ref[kblock(j), :]
            return jnp.maximum(best, jnp.max(jnp.where(x < hi, x, NEG_INF), axis=0, keepdims=True))

        return lax.fori_loop(0, nk, body, jnp.full((1, TQ), NEG_INF, F32))

    n_valid = i * TQ + lax.broadcasted_iota(I32, (1, TQ), 1) + 1
    thr, n_tied = _kth_largest(count, below_max, rmin, rmax, n_valid, topk)

    @pl.when(n_tied == 0)
    def _():
        def body(j, carry):
            x = sc_ref[kblock(j), :]
            sc_ref[kblock(j), :] = jnp.where(x >= thr, 0.0, NEG_INF)
            return carry

        lax.fori_loop(0, nk, body, 0)

    @pl.when(n_tied > 0)
    def _():
        quota = kf - count(lambda x: x > thr)

        def body(j, ties_before):
            x = sc_ref[kblock(j), :]
            eq = jnp.where(x == thr, 1.0, 0.0)
            rank = jnp.dot(tri_ref[...], eq.astype(BF16), preferred_element_type=F32) + ties_before
            sel = (x > thr) | ((x == thr) & (rank < quota))
            sc_ref[kblock(j), :] = jnp.where(sel, 0.0, NEG_INF)
            return ties_before + jnp.sum(eq, axis=0, keepdims=True)

        lax.fori_loop(0, nk, body, jnp.zeros((1, TQ), F32))

    n_pairs = (nk + 1) // 2

    @pl.when(nk % 2 == 1)
    def _():
        sc_ref[kblock(nk), :] = jnp.full((KB, TQ), NEG_INF, F32)

    def score_phase(j, s_out):
        block_max = []
        for h in range(H_ATT):
            pair = slice((h // 2) * LANES, (h // 2 + 1) * LANES)
            s = jnp.dot(k_ref[0, kblock(j), pair], qpad_ref[h], preferred_element_type=F32)
            s = s + sc_ref[kblock(j), :]
            s_out[h] = s
            block_max.append(jnp.max(s, axis=0, keepdims=True))
        return tuple(block_max)

    ones_rows = (lax.broadcasted_iota(I32, (SUM_ROWS, KB), 0) == 0).astype(BF16)

    def value_phase(j, p_in, alphas, accs):
        new_accs = []
        for h in range(H_ATT):
            hrows = slice(h * DH_ATT, (h + 1) * DH_ATT)
            v_aug = jnp.concatenate([vT_ref[0, j, hrows, :], ones_rows], axis=0)
            pv = jnp.dot(v_aug, p_in[h], preferred_element_type=F32)
            new_accs.append(alphas[h] * accs[h] + pv)
        return tuple(new_accs)

    def block_step(j, next_j, s_in, s_out, p_prev, p_out, carry):
        ms, accs, block_max, alphas_prev = carry
        next_max = score_phase(next_j, s_out)
        new_ms, alphas = [], []
        for h in range(H_ATT):
            m_new = jnp.maximum(ms[h], block_max[h])
            m_safe = jnp.where(m_new == NEG_INF, 0.0, m_new)
            alphas.append(jnp.exp2(ms[h] - m_safe))
            new_ms.append(m_new)
            p_out[h] = jnp.exp2((s_in[h] - m_safe).astype(BF16))
        accs = value_phase(jnp.maximum(j - 1, 0), p_prev, alphas_prev, accs)
        return tuple(new_ms), accs, next_max, tuple(alphas)

    def attend_pair(jj, carry):
        j0 = 2 * jj
        carry = block_step(j0, j0 + 1, s_scr, s2_scr, p2_scr, p_scr, carry)
        return block_step(j0 + 1, jnp.minimum(j0 + 2, 2 * n_pairs - 1), s2_scr, s_scr, p_scr, p2_scr, carry)

    p2_scr[...] = jnp.zeros(p2_scr.shape, BF16)
    init = (tuple(jnp.full((1, TQ), NEG_INF, F32) for _ in range(H_ATT)),
            tuple(jnp.zeros((DH_ATT + SUM_ROWS, TQ), F32) for _ in range(H_ATT)),
            score_phase(0, s_scr),
            tuple(jnp.ones((1, TQ), F32) for _ in range(H_ATT)))
    _, accs, _, alphas = lax.fori_loop(0, n_pairs, attend_pair, init)
    accs = value_phase(2 * n_pairs - 1, p2_scr, alphas, accs)
    for h in range(H_ATT):
        oT_ref[h * DH_ATT:(h + 1) * DH_ATT, :] = accs[h][:DH_ATT] / accs[h][DH_ATT:DH_ATT + 1]
    o_ref[0] = oT_ref[...].T.astype(BF16)


def _dsa_prompt(qT, qiT, wT, kwb, kb, vTb):
    b, _, s = qT.shape
    assert s % (2 * KB) == 0 and TQ == KB
    topk = min(TOPK_MAX, s // 4)
    tri = (jnp.arange(KB)[:, None] > jnp.arange(KB)[None, :]).astype(BF16)
    feat = lambda w: pl.BlockSpec((1, w, TQ), lambda i, j: (i, 0, j))
    full = lambda w: pl.BlockSpec((1, s, w), lambda i, j: (i, 0, 0))
    return pl.pallas_call(
        functools.partial(_dsa_prompt_kernel, topk=topk),
        grid=(b, s // TQ),
        in_specs=[feat(HD), feat(HD), feat(H_IDX), full(LANES), full(HD),
                  pl.BlockSpec((1, s // KB, HD, KB), lambda i, j: (i, 0, 0, 0)), _const_spec(tri.shape)],
        out_specs=pl.BlockSpec((1, TQ, HD), lambda i, j: (i, j, 0)),
        out_shape=jax.ShapeDtypeStruct((b, s, HD), BF16),
        scratch_shapes=[pltpu.VMEM((s, TQ), F32), pltpu.VMEM((H_ATT, LANES, TQ), BF16),
                        pltpu.VMEM((H_IDX, LANES, TQ), BF16), pltpu.VMEM((HD, TQ), F32),
                        pltpu.VMEM((H_ATT, KB, TQ), F32), pltpu.VMEM((H_ATT, KB, TQ), F32),
                        pltpu.VMEM((H_ATT, KB, TQ), BF16), pltpu.VMEM((H_ATT, KB, TQ), BF16)],
        compiler_params=_params(2),
        name="dsa_prompt",
    )(qT, qiT, wT, kwb, kb, vTb, tri)


PAGE_GROUP_IDX = 64
PAGE_GROUP_KV = 32
PAGE_PARTIALS = 8


def _dsa_sample_select_kernel(pt_ref, qi_ref, kwq_ref, tri_ref, *rest, n_pages, topk, group):
    page_refs, (bias_ref, biasn_ref, qiall_ref, wb_ref) = rest[:group], rest[group:]
    pg = pl.program_id(1)
    t = qi_ref.shape[1]
    kwq = kwq_ref[0]

    @pl.when(pg == 0)
    def _():
        qi = qi_ref[0].astype(F32)
        qiall_ref[...] = jnp.concatenate(
            [qi[:, h * D_IDX:(h + 1) * D_IDX] for h in range(H_IDX)], axis=0).astype(BF16)
        for h in range(H_IDX):
            wb_ref[h] = jnp.broadcast_to(kwq[:, D_IDX + h:D_IDX + h + 1], (t, LANES))

    def scores(dots):
        acc = jnp.zeros((t, dots.shape[1]), F32)
        for h in range(H_IDX):
            acc = acc + jnp.maximum(dots[h * t:(h + 1) * t], 0.0) * wb_ref[h]
        return acc

    for g in range(group):
        dots = jnp.dot(qiall_ref[...], page_refs[g][0].astype(BF16), preferred_element_type=F32)
        bias_ref[0, pg * group + g] = scores(dots)

    @pl.when(pg == pl.num_programs(1) - 1)
    def _():
        new_keys = jnp.concatenate([kwq[:, :D_IDX], jnp.zeros((LANES - t, D_IDX), F32)], axis=0).astype(BF16)
        row = lax.broadcasted_iota(I32, (t, LANES), 0)
        col = lax.broadcasted_iota(I32, (t, LANES), 1)
        dots_new = lax.dot_general(qiall_ref[...], new_keys, NT_DIMS, preferred_element_type=F32)
        raw_new = scores(dots_new)
        sc_new = jnp.where(col <= row, raw_new, NEG_INF)

        def over_pages(fn, first, combine):
            parts = [first]
            for p in range(n_pages):
                v = fn(bias_ref[0, p])
                if len(parts) < PAGE_PARTIALS:
                    parts.append(v)
                else:
                    parts[p % PAGE_PARTIALS] = combine(parts[p % PAGE_PARTIALS], v)
            return functools.reduce(combine, parts)

        def count(pred):
            hit = lambda x: jnp.where(pred(x), 1.0, 0.0)
            return jnp.sum(over_pages(hit, hit(sc_new), jnp.add), axis=1, keepdims=True)

        def below_max(hi):
            below = lambda x: jnp.where(x < hi, x, NEG_INF)
            return jnp.max(over_pages(below, below(sc_new), jnp.maximum), axis=1, keepdims=True)

        ident = lambda x: x
        rmin = jnp.min(over_pages(ident, jnp.where(col <= row, raw_new, jnp.inf), jnp.minimum),
                       axis=1, keepdims=True)
        rmax = jnp.max(over_pages(ident, sc_new, jnp.maximum), axis=1, keepdims=True)
        n_valid = n_pages * bias_ref.shape[3] + lax.broadcasted_iota(I32, (t, 1), 0) + 1
        thr, n_tied = _kth_largest(count, below_max, rmin, rmax, n_valid, topk)
        as_bias = lambda sel: jnp.where(sel, 0.0, NEG_INF)

        @pl.when(n_tied == 0)
        def _():
            bias_ref[0] = as_bias(bias_ref[0] >= thr)
            biasn_ref[0] = as_bias(sc_new >= thr)

        @pl.when(n_tied > 0)
        def _():
            quota = topk - count(lambda x: x > thr)

            def select(x, ties_before):
                eq = x == thr
                eqf = jnp.where(eq, 1.0, 0.0)
                rank = jnp.dot(eqf.astype(BF16), tri_ref[...], preferred_element_type=F32) + ties_before
                sel = (x > thr) | (eq & (rank < quota))
                return as_bias(sel), ties_before + jnp.sum(eqf, axis=1, keepdims=True)

            def bias_page(p, ties_before):
                bias, ties = select(bias_ref[0, p], ties_before)
                bias_ref[0, p] = bias
                return ties

            ties = lax.fori_loop(0, n_pages, bias_page, jnp.zeros((t, 1), F32))
            biasn_ref[0], _ = select(sc_new, ties)


def _dsa_sample_select(page_table, qib, kw32, cache_kidx_t):
    b, t, _ = qib.shape
    n_pages = page_table.shape[1]
    page = cache_kidx_t.shape[2]
    topk = min(TOPK_MAX, (n_pages * page + t) // 4)
    group = min(PAGE_GROUP_IDX, n_pages)
    assert n_pages % group == 0
    tri = (jnp.arange(page)[:, None] < jnp.arange(page)[None, :]).astype(BF16)
    tok = lambda w: pl.BlockSpec((1, t, w), lambda i, j, pt: (i, 0, 0))
    page_specs = [pl.BlockSpec((1, D_IDX, page), lambda i, j, pt, g=g: (pt[i, j * group + g], 0, 0))
                  for g in range(group)]
    grid_spec = pltpu.PrefetchScalarGridSpec(
        num_scalar_prefetch=1,
        grid=(b, n_pages // group),
        in_specs=[tok(H_IDX * D_IDX), tok(LANES), pl.BlockSpec(tri.shape, lambda i, j, pt: (0, 0))] + page_specs,
        out_specs=[pl.BlockSpec((1, n_pages, t, page), lambda i, j, pt: (i, 0, 0, 0)),
                   pl.BlockSpec((1, t, LANES), lambda i, j, pt: (i, 0, 0))],
        scratch_shapes=[pltpu.VMEM((H_IDX * t, D_IDX), BF16), pltpu.VMEM((H_IDX, t, LANES), F32)],
    )
    return pl.pallas_call(
        functools.partial(_dsa_sample_select_kernel, n_pages=n_pages, topk=topk, group=group),
        grid_spec=grid_spec,
        out_shape=[jax.ShapeDtypeStruct((b, n_pages, t, page), F32), jax.ShapeDtypeStruct((b, t, LANES), F32)],
        compiler_params=pltpu.CompilerParams(dimension_semantics=("parallel", "arbitrary"),
                                             vmem_limit_bytes=VMEM_LIMIT),
        name="dsa_sample_select",
    )(page_table, qib, kw32, tri, *([cache_kidx_t] * group))


def _dsa_sample_attend_kernel(pt_ref, q_ref, kn_ref, vn_ref, bias_ref, biasn_ref, *rest, group):
    kT_refs, vT_refs = rest[:group], rest[group:2 * group]
    o_ref, m_ref, l_ref, acc_ref = rest[2 * group:]
    pg = pl.program_id(1)
    t = q_ref.shape[1]
    page = kT_refs[0].shape[2]
    lane_head = lax.broadcasted_iota(I32, (t, HD), 1) // DH_ATT
    q = q_ref[0].astype(F32)
    qbd = jnp.concatenate([jnp.where(lane_head == h, q, 0.0) for h in range(H_ATT)], axis=0).astype(BF16)

    @pl.when(pg == 0)
    def _():
        m_ref[...] = jnp.full(m_ref.shape, NEG_INF, F32)
        l_ref[...] = jnp.zeros(l_ref.shape, F32)
        acc_ref[...] = jnp.zeros(acc_ref.shape, F32)

    def update(s, bias, pv):
        s = s + jnp.concatenate([bias] * H_ATT, axis=0)
        m = m_ref[...]
        m_new = jnp.maximum(m, jnp.max(s, axis=1, keepdims=True))
        m_safe = jnp.where(m_new == NEG_INF, 0.0, m_new)
        alpha = jnp.exp(m - m_safe)
        p = jnp.exp(s - m_safe)
        l_ref[...] = alpha * l_ref[...] + jnp.sum(p, axis=1, keepdims=True)
        acc_ref[...] = alpha * acc_ref[...] + pv(p.astype(BF16))
        m_ref[...] = m_new

    s_pages = jnp.concatenate(
        [jnp.dot(qbd, kT_refs[g][0].astype(BF16), preferred_element_type=F32) for g in range(group)], axis=1)
    bias_pages = jnp.concatenate([bias_ref[0, g] for g in range(group)], axis=1)

    def pv_pages(p):
        out = jnp.zeros((H_ATT * t, HD), F32)
        for g in range(group):
            out = out + lax.dot_general(p[:, g * page:(g + 1) * page], vT_refs[g][0].astype(BF16), NT_DIMS,
                                        preferred_element_type=F32)
        return out

    update(s_pages, bias_pages, pv_pages)

    @pl.when(pg == pl.num_programs(1) - 1)
    def _():
        pad = jnp.zeros((LANES - t, HD), F32)
        kn = jnp.concatenate([kn_ref[0], pad], axis=0).astype(BF16)
        vn = jnp.concatenate([vn_ref[0], pad], axis=0).astype(BF16)
        update(lax.dot_general(qbd, kn, NT_DIMS, preferred_element_type=F32), biasn_ref[0],
               lambda p: jnp.dot(p, vn, preferred_element_type=F32))
        o = acc_ref[...] / l_ref[...]
        out = jnp.zeros((t, HD), F32)
        for h in range(H_ATT):
            out = out + jnp.where(lane_head == h, o[h * t:(h + 1) * t], 0.0)
        o_ref[0] = out.astype(BF16)


def _dsa_sample_attend(page_table, qb, k32, v32, bias, bias_new, cache_kt, cache_vt):
    b, t, _ = qb.shape
    n_pages = page_table.shape[1]
    page = cache_kt.shape[2]
    group = min(PAGE_GROUP_KV, n_pages)
    assert n_pages % group == 0
    tok = lambda w: pl.BlockSpec((1, t, w), lambda i, j, pt: (i, 0, 0))
    kv_specs = [pl.BlockSpec((1, HD, page), lambda i, j, pt, g=g: (pt[i, j * group + g], 0, 0))
                for g in range(group)]
    grid_spec = pltpu.PrefetchScalarGridSpec(
        num_scalar_prefetch=1,
        grid=(b, n_pages // group),
        in_specs=[tok(HD), tok(HD), tok(HD),
                  pl.BlockSpec((1, group, t, page), lambda i, j, pt: (i, j, 0, 0)),
                  tok(LANES)] + kv_specs + kv_specs,
        out_specs=tok(HD),
        scratch_shapes=[pltpu.VMEM((H_ATT * t, 1), F32), pltpu.VMEM((H_ATT * t, 1), F32),
                        pltpu.VMEM((H_ATT * t, HD), F32)],
    )
    return pl.pallas_call(
        functools.partial(_dsa_sample_attend_kernel, group=group),
        grid_spec=grid_spec,
        out_shape=jax.ShapeDtypeStruct((b, t, HD), BF16),
        compiler_params=pltpu.CompilerParams(dimension_semantics=("parallel", "arbitrary"),
                                             vmem_limit_bytes=VMEM_LIMIT),
        name="dsa_sample_attend",
    )(page_table, qb, k32, v32, bias, bias_new, *([cache_kt] * group), *([cache_vt] * group))


def _merge_kernel(x_ref, oa_ref, ol_ref, om_ref, g_ref, wg_ref, wa_ref, wl_ref, wm_ref, wo_ref, o_ref):
    x = x_ref[...]
    hn = _rms(x, g_ref[...]).astype(BF16)
    m = jnp.zeros_like(x)
    for idx, (o_r, w_r) in enumerate(((oa_ref, wa_ref), (ol_ref, wl_ref), (om_ref, wm_ref))):
        gate = jax.nn.sigmoid(jnp.dot(hn, wg_ref[:, idx * D_MODEL:(idx + 1) * D_MODEL], preferred_element_type=F32))
        m = m + gate * jnp.dot(o_r[...], w_r[...], preferred_element_type=F32)
    o_ref[...] = x + jnp.dot(m.astype(BF16), wo_ref[...], preferred_element_type=F32)


def _merge(x, o_att, o_lru, o_mem, consts, tm):
    n = x.shape[0]
    row = lambda w: pl.BlockSpec((tm, w), lambda i: (i, 0))
    return pl.pallas_call(
        _merge_kernel,
        grid=(n // tm,),
        in_specs=[row(D_MODEL), row(HD), row(LRU_W), row(MD)] + [_const_spec(c.shape) for c in consts],
        out_specs=row(D_MODEL),
        out_shape=jax.ShapeDtypeStruct((n, D_MODEL), F32),
        compiler_params=_params(1),
        name="merge",
    )(x, o_att, o_lru, o_mem, *consts)


def _token_tile(n):
    return min(n, 512)


def _layer(x, is_prompt, lw, pp, extra):
    b, t, _ = x.shape
    n = b * t
    tm = _token_tile(n)
    x1 = _ffn(x.reshape(n, D_MODEL), pp["ffn1"], tm)
    r3 = lambda a: a.reshape(b, t, a.shape[-1])
    if is_prompt:
        kT32, vT32, kiT32, wT, qT, qiT, kb, kwb, vTb, lx, lg, qmb = _proj_t(r3(x1), pp["proj"], tm)
        o_att = _dsa_prompt(qT, qiT, wT, kwb, kb, vTb)
        k_new = kT32.reshape(b, H_ATT, DH_ATT, t).transpose(0, 3, 1, 2)
        v_new = vT32.reshape(b, H_ATT, DH_ATT, t).transpose(0, 3, 1, 2)
        ki_new = kiT32.transpose(0, 2, 1)
        conv_state = jnp.zeros((b, CONV_W - 1, LRU_W), F32)
        h0 = jnp.zeros((b, LRU_W), F32)
        mem = extra["mem"]
        mk, mv = _memkv(mem.reshape(-1, D_MODEL), lw["norm_mem_g"], lw["w_mem_kv"], lw["mem_k_norm_g"],
                        pp["proj"]["g128"])
        mk = mk.reshape(b, -1, MD)
        mv = mv.reshape(b, -1, MD)
    else:
        k32, v32, kw32, qb, _, _, qib, _, lx, lg, qmb = _proj(x1, pp["proj"], tm)
        pt = extra["page_table"]
        bias, bias_new = _dsa_sample_select(pt, r3(qib), r3(kw32), extra["cache_kidx_t"])
        o_att = _dsa_sample_attend(pt, r3(qb), r3(k32), r3(v32), bias, bias_new, extra["cache_kt"],
                                   extra["cache_vt"])
        k_new = k32.reshape(b, t, H_ATT, DH_ATT)
        v_new = v32.reshape(b, t, H_ATT, DH_ATT)
        ki_new = r3(kw32)[:, :, :D_IDX]
        conv_state, h0 = extra["state_conv"], extra["state_h"]
        mk, mv = extra["cache_mem_k"], extra["cache_mem_v"]
    o_lru, conv_buf, h_last = _rglru(r3(lx), r3(lg), conv_state, h0, pp["rglru"], min(t, 256))
    o_mem = _memattn(r3(qmb), mk, mv, min(t, 512))
    x2 = _merge(x1, o_att.reshape(n, HD), o_lru.reshape(n, LRU_W), o_mem.reshape(n, MD), pp["merge"], tm)
    y = _ffn(x2, pp["ffn2"], tm).reshape(b, t, D_MODEL)
    if is_prompt:
        state = (k_new, v_new, ki_new, mk.reshape(b, -1, H_MEM, DH_MEM), mv.reshape(b, -1, H_MEM, DH_MEM),
                 conv_buf, h_last)
    else:
        state = (k_new, v_new, ki_new, conv_buf, h_last)
    return y, state


def kernel(x_prompt, x_sample, cache_k, cache_v, cache_kidx, page_table, cache_mem_k, cache_mem_v, state_conv, state_h, mem_prompt, norm_ffn1_g, w_ffn1_in, w_ffn1_out, norm_mix_g, w_in, q_norm_g, k_norm_g, w_attn_o, conv_w, conv_b, lru_wa, lru_ba, lru_wi, lru_bi, lru_lambda, w_lru_o, norm_mem_g, w_mem_kv, mem_q_norm_g, mem_k_norm_g, w_mem_o, w_out, norm_ffn2_g, w_ffn2_in, w_ffn2_out):
    depth = w_in.shape[0]
    n_phys, page = cache_k.shape[1], cache_k.shape[2]
    xp, xs = x_prompt, x_sample
    p_states, s_states = [], []
    for l in range(depth):
        proj = _prep_proj(norm_mix_g[l], w_in[l], q_norm_g[l], k_norm_g[l], mem_q_norm_g[l])
        pp = dict(
            ffn1=_prep_ffn(norm_ffn1_g[l], w_ffn1_in[l], w_ffn1_out[l]),
            ffn2=_prep_ffn(norm_ffn2_g[l], w_ffn2_in[l], w_ffn2_out[l]),
            proj=proj,
            rglru=_prep_rglru(conv_w[l], conv_b[l], lru_wa[l], lru_ba[l], lru_wi[l], lru_bi[l], lru_lambda[l]),
            merge=[proj["g"], proj["wgates"], w_attn_o[l].astype(BF16), w_lru_o[l].astype(BF16),
                   w_mem_o[l].astype(BF16), w_out[l].astype(BF16)],
        )
        lw = dict(norm_mem_g=norm_mem_g[l], w_mem_kv=w_mem_kv[l], mem_k_norm_g=mem_k_norm_g[l])
        xp, st_p = _layer(xp, True, lw, pp, dict(mem=mem_prompt))
        xs, st_s = _layer(xs, False, lw, pp, dict(
            page_table=page_table,
            cache_kt=cache_k[l].transpose(0, 2, 3, 1).reshape(n_phys, HD, page),
            cache_vt=cache_v[l].transpose(0, 2, 3, 1).reshape(n_phys, HD, page),
            cache_kidx_t=cache_kidx[l].transpose(0, 2, 1), cache_mem_k=cache_mem_k[l].reshape(-1, cache_mem_k.shape[2], MD),
            cache_mem_v=cache_mem_v[l].reshape(-1, cache_mem_v.shape[2], MD),
            state_conv=state_conv[l], state_h=state_h[l]))
        p_states.append(st_p)
        s_states.append(st_s)
    stack = lambda states, i: jnp.stack([s[i] for s in states])
    return (xp, xs) + tuple(stack(p_states, i) for i in range(7)) + tuple(stack(s_states, i) for i in range(5))
```

```python
import functools

import jax
import jax.numpy as jnp
from jax import lax
from jax.experimental import pallas as pl
from jax.experimental.pallas import tpu as pltpu

F32, BF16, I32 = jnp.float32, jnp.bfloat16, jnp.int32

D_MODEL = 1024
H_ATT, DH_ATT = 8, 64
H_IDX, D_IDX = 8, 64
TOPK_MAX = 256
LRU_W, LRU_BLOCKS, CONV_W, LRU_C = 512, 8, 4, 8.0
H_MEM, DH_MEM = 4, 128
D_FF = 2816
EPS = 1e-6
HD = H_ATT * DH_ATT
MD = H_MEM * DH_MEM

LANES = 128
SUBLANES = 8
VMEM_BYTES_V7X = 64 * 1024 * 1024
VMEM_LIMIT = VMEM_BYTES_V7X - 8 * 1024 * 1024

FF_CHUNK = 256
TQ = 256
KB = 256
SCORE_BLOCKS = (4, 2, 1)
CNT_ROWS = 32
SUM_ROWS = 16
BISECT_STEPS = 15
LOG2E = 1.4426950408889634
NEG_INF = float("-inf")

NT_DIMS = (((1,), (1,)), ((), ()))


def _params(n_grid, parallel=True):
    sem = ("parallel" if parallel else "arbitrary",) * n_grid
    return pltpu.CompilerParams(dimension_semantics=sem, vmem_limit_bytes=VMEM_LIMIT)


def _const_spec(shape):
    nd = len(shape)
    return pl.BlockSpec(shape, lambda *_: (0,) * nd)


def _rms(x, g):
    ms = jnp.mean(x * x, axis=-1, keepdims=True)
    return x * lax.rsqrt(ms + EPS) * g


def _group_rms(x, gmat, g, group):
    x2 = x * x
    hi = x2.astype(BF16)
    lo = (x2 - hi.astype(F32)).astype(BF16)
    ss = jnp.dot(hi, gmat, preferred_element_type=F32) + jnp.dot(lo, gmat, preferred_element_type=F32)
    return x * lax.rsqrt(ss * (1.0 / group) + EPS) * g


def _group_matrix(width, group):
    idx = jnp.arange(width) // group
    return (idx[:, None] == idx[None, :]).astype(BF16)


def _ffn_kernel(x_ref, g_ref, wg_ref, wu_ref, wo_ref, o_ref):
    x = x_ref[...]
    hn = _rms(x, g_ref[...]).astype(BF16)
    acc = jnp.zeros_like(x)
    for c in range(wg_ref.shape[0]):
        gate = jnp.dot(hn, wg_ref[c], preferred_element_type=F32)
        up = jnp.dot(hn, wu_ref[c], preferred_element_type=F32)
        act = (gate * jax.nn.sigmoid(gate) * up).astype(BF16)
        acc = acc + jnp.dot(act, wo_ref[c], preferred_element_type=F32)
    o_ref[...] = x + 0.5 * acc


def _prep_ffn(g, w_in, w_out):
    nc = D_FF // FF_CHUNK
    wg = w_in[:, :D_FF].reshape(D_MODEL, nc, FF_CHUNK).transpose(1, 0, 2).astype(BF16)
    wu = w_in[:, D_FF:].reshape(D_MODEL, nc, FF_CHUNK).transpose(1, 0, 2).astype(BF16)
    wo = w_out.reshape(nc, FF_CHUNK, D_MODEL).astype(BF16)
    return g.reshape(1, D_MODEL), wg, wu, wo


def _ffn(x, prep, tm):
    g, wg, wu, wo = prep
    n = x.shape[0]
    row = pl.BlockSpec((tm, D_MODEL), lambda i: (i, 0))
    return pl.pallas_call(
        _ffn_kernel,
        grid=(n // tm,),
        in_specs=[row, _const_spec(g.shape), _const_spec(wg.shape), _const_spec(wu.shape), _const_spec(wo.shape)],
        out_specs=row,
        out_shape=jax.ShapeDtypeStruct((n, D_MODEL), F32),
        compiler_params=_params(1),
        name="ffn",
    )(x, g, wg, wu, wo)


def _proj_kernel(x_ref, g_ref, wqkv_ref, wqi_ref, wkw_ref, wl_ref, wqm_ref, qg_ref, kg_ref, mg_ref,
                 kwscale_ref, g64_ref, g128_ref,
                 k32_ref, v32_ref, kw32_ref, qb_ref, kb_ref, vb_ref, qib_ref, kwb_ref, lx_ref, lg_ref, qmb_ref):
    hn = _rms(x_ref[...], g_ref[...]).astype(BF16)
    qkv = jnp.dot(hn, wqkv_ref[...], preferred_element_type=F32)
    q = _group_rms(qkv[:, :HD], g64_ref[...], qg_ref[...], DH_ATT)
    k = _group_rms(qkv[:, HD:2 * HD], g64_ref[...], kg_ref[...], DH_ATT)
    v = qkv[:, 2 * HD:]
    k32_ref[...] = k
    v32_ref[...] = v
    qb_ref[...] = (q * (DH_ATT ** -0.5)).astype(BF16)
    kb_ref[...] = k.astype(BF16)
    vb_ref[...] = v.astype(BF16)
    qib_ref[...] = jnp.dot(hn, wqi_ref[...], preferred_element_type=F32).astype(BF16)
    kw = jnp.dot(hn, wkw_ref[...], preferred_element_type=F32) * kwscale_ref[...]
    kw32_ref[...] = kw
    kwb_ref[...] = kw.astype(BF16)
    lxg = jnp.dot(hn, wl_ref[...], preferred_element_type=F32)
    lx_ref[...] = lxg[:, :LRU_W]
    lg_ref[...] = lxg[:, LRU_W:]
    qm = jnp.dot(hn, wqm_ref[...], preferred_element_type=F32)
    qmb_ref[...] = _group_rms(qm, g128_ref[...], mg_ref[...], DH_MEM).astype(BF16)


def _prep_proj(norm_g, w_in, q_norm_g, k_norm_g, mem_q_norm_g):
    o = 0
    cols = {}
    for name, size in (("q", HD), ("k", HD), ("v", HD), ("qi", H_IDX * D_IDX), ("ki", D_IDX), ("wi", H_IDX),
                       ("lx", LRU_W), ("lg", LRU_W), ("qm", MD), ("gates", 3 * D_MODEL)):
        cols[name] = w_in[:, o:o + size]
        o += size
    pad = jnp.zeros((D_MODEL, LANES - D_IDX - H_IDX), w_in.dtype)
    wqkv = jnp.concatenate([cols["q"], cols["k"], cols["v"]], axis=1).astype(BF16)
    wkw = jnp.concatenate([cols["ki"], cols["wi"], pad], axis=1).astype(BF16)
    wl = jnp.concatenate([cols["lx"], cols["lg"]], axis=1).astype(BF16)
    kwscale = jnp.concatenate([jnp.ones((D_IDX,), F32),
                               jnp.full((H_IDX,), H_IDX ** -0.5 * D_IDX ** -0.5, F32),
                               jnp.zeros((LANES - D_IDX - H_IDX,), F32)]).reshape(1, LANES)
    return dict(
        g=norm_g.reshape(1, D_MODEL), wqkv=wqkv, wqi=cols["qi"].astype(BF16), wkw=wkw, wl=wl,
        wqm=cols["qm"].astype(BF16),
        qg=jnp.tile(q_norm_g, H_ATT).reshape(1, HD), kg=jnp.tile(k_norm_g, H_ATT).reshape(1, HD),
        mg=jnp.tile(mem_q_norm_g, H_MEM).reshape(1, MD), kwscale=kwscale,
        g64=_group_matrix(HD, DH_ATT), g128=_group_matrix(MD, DH_MEM),
        wgates=cols["gates"].astype(BF16),
        wqkT=wqkv[:, :2 * HD].T, wvT=wqkv[:, 2 * HD:].T, wqiT=cols["qi"].astype(BF16).T, wkwT=wkw.T,
        qg_col=jnp.tile(q_norm_g, H_ATT).reshape(HD, 1), kg_col=jnp.tile(k_norm_g, H_ATT).reshape(HD, 1),
        kwscale_col=kwscale.reshape(LANES, 1),
    )


def _group_rms_t(xt, gmat, g, group):
    x2 = xt * xt
    hi = x2.astype(BF16)
    lo = (x2 - hi.astype(F32)).astype(BF16)
    ss = jnp.dot(gmat, hi, preferred_element_type=F32) + jnp.dot(gmat, lo, preferred_element_type=F32)
    return xt * lax.rsqrt(ss * (1.0 / group) + EPS) * g


def _proj_t_kernel(x_ref, g_ref, wqkT_ref, wvT_ref, wqiT_ref, wkwT_ref, wl_ref, wqm_ref, qgT_ref, kgT_ref, mg_ref,
                   kwscaleT_ref, g64_ref, g128_ref,
                   kT_ref, vT_ref, kiT_ref, wT_ref, qT_ref, qiT_ref, kb_ref, kwb_ref, vTb_ref, lx_ref, lg_ref,
                   qmb_ref):
    hn = _rms(x_ref[0], g_ref[...]).astype(BF16)
    nt = lambda w_ref: lax.dot_general(w_ref[...], hn, NT_DIMS, preferred_element_type=F32)
    qkT = nt(wqkT_ref)
    qT = _group_rms_t(qkT[:HD], g64_ref[...], qgT_ref[...], DH_ATT)
    kT = _group_rms_t(qkT[HD:], g64_ref[...], kgT_ref[...], DH_ATT)
    kT_ref[0] = kT
    kb_ref[0] = kT.T.astype(BF16)
    qT_ref[0] = (qT * (DH_ATT ** -0.5 * LOG2E)).astype(BF16)
    vT = nt(wvT_ref)
    vT_ref[0] = vT
    for c in range(vTb_ref.shape[1]):
        vTb_ref[0, c] = vT[:, c * KB:(c + 1) * KB].astype(BF16)
    qiT_ref[0] = nt(wqiT_ref).astype(BF16)
    kwT = nt(wkwT_ref) * kwscaleT_ref[...]
    kiT_ref[0] = kwT[:D_IDX]
    wT_ref[0] = kwT[D_IDX:D_IDX + H_IDX]
    kwb_ref[0] = kwT.T.astype(BF16)
    lxg = jnp.dot(hn, wl_ref[...], preferred_element_type=F32)
    lx_ref[0] = lxg[:, :LRU_W]
    lg_ref[0] = lxg[:, LRU_W:]
    qm = jnp.dot(hn, wqm_ref[...], preferred_element_type=F32)
    qmb_ref[0] = _group_rms(qm, g128_ref[...], mg_ref[...], DH_MEM).astype(BF16)


def _proj_t(x, p, tm):
    b, s, _ = x.shape
    bc = lambda col: jnp.broadcast_to(col, (col.shape[0], tm))
    consts = [p["g"], p["wqkT"], p["wvT"], p["wqiT"], p["wkwT"], p["wl"], p["wqm"], bc(p["qg_col"]),
              bc(p["kg_col"]), p["mg"], bc(p["kwscale_col"]), p["g64"], p["g128"]]
    tok = lambda w: pl.BlockSpec((1, tm, w), lambda i, j: (i, j, 0))
    feat = lambda w: pl.BlockSpec((1, w, tm), lambda i, j: (i, 0, j))
    outs = [
        (feat(HD), (b, HD, s), F32), (feat(HD), (b, HD, s), F32), (feat(D_IDX), (b, D_IDX, s), F32),
        (feat(H_IDX), (b, H_IDX, s), F32), (feat(HD), (b, HD, s), BF16), (feat(HD), (b, HD, s), BF16),
        (tok(HD), (b, s, HD), BF16), (tok(LANES), (b, s, LANES), BF16),
        (pl.BlockSpec((1, tm // KB, HD, KB), lambda i, j: (i, j, 0, 0)), (b, s // KB, HD, KB), BF16),
        (tok(LRU_W), (b, s, LRU_W), F32), (tok(LRU_W), (b, s, LRU_W), F32), (tok(MD), (b, s, MD), BF16),
    ]
    return pl.pallas_call(
        _proj_t_kernel,
        grid=(b, s // tm),
        in_specs=[tok(D_MODEL)] + [_const_spec(c.shape) for c in consts],
        out_specs=[o[0] for o in outs],
        out_shape=[jax.ShapeDtypeStruct(o[1], o[2]) for o in outs],
        compiler_params=_params(2),
        name="proj_t",
    )(x, *consts)


def _proj(x, p, tm):
    n = x.shape[0]
    consts = [p[k] for k in ("g", "wqkv", "wqi", "wkw", "wl", "wqm", "qg", "kg", "mg", "kwscale", "g64", "g128")]

    def row(w):
        return pl.BlockSpec((tm, w), lambda i: (i, 0))

    outs = [(HD, F32), (HD, F32), (LANES, F32), (HD, BF16), (HD, BF16), (HD, BF16), (HD, BF16), (LANES, BF16),
            (LRU_W, F32), (LRU_W, F32), (MD, BF16)]
    return pl.pallas_call(
        _proj_kernel,
        grid=(n // tm,),
        in_specs=[row(D_MODEL)] + [_const_spec(c.shape) for c in consts],
        out_specs=[row(w) for w, _ in outs],
        out_shape=[jax.ShapeDtypeStruct((n, w), dt) for w, dt in outs],
        compiler_params=_params(1),
        name="proj",
    )(x, *consts)


def _memkv_kernel(m_ref, g_ref, w_ref, kg_ref, g128_ref, mk_ref, mv_ref):
    hn = _rms(m_ref[...], g_ref[...]).astype(BF16)
    kv = jnp.dot(hn, w_ref[...], preferred_element_type=F32)
    mk_ref[...] = _group_rms(kv[:, :MD], g128_ref[...], kg_ref[...], DH_MEM)
    mv_ref[...] = kv[:, MD:]


def _memkv(mem, norm_g, w_mem_kv, mem_k_norm_g, g128):
    n = mem.shape[0]
    tm = min(n, 512)
    consts = [norm_g.reshape(1, D_MODEL), w_mem_kv.astype(BF16), jnp.tile(mem_k_norm_g, H_MEM).reshape(1, MD), g128]
    row = lambda w: pl.BlockSpec((tm, w), lambda i: (i, 0))
    return pl.pallas_call(
        _memkv_kernel,
        grid=(n // tm,),
        in_specs=[row(D_MODEL)] + [_const_spec(c.shape) for c in consts],
        out_specs=[row(MD), row(MD)],
        out_shape=[jax.ShapeDtypeStruct((n, MD), F32)] * 2,
        compiler_params=_params(1),
        name="memkv",
    )(mem, *consts)


def _memattn_kernel(q_ref, mk_ref, mv_ref, o_ref):
    q = q_ref[0]
    mk = mk_ref[0].astype(BF16)
    mv = mv_ref[0].astype(BF16)
    for h in range(H_MEM):
        sl = slice(h * DH_MEM, (h + 1) * DH_MEM)
        s = lax.dot_general(q[:, sl], mk[:, sl], NT_DIMS, preferred_element_type=F32) * (DH_MEM ** -0.5)
        m = jnp.max(s, axis=-1, keepdims=True)
        e = jnp.exp(s - m)
        p = (e / jnp.sum(e, axis=-1, keepdims=True)).astype(BF16)
        o_ref[0, :, sl] = jnp.dot(p, mv[:, sl], preferred_element_type=F32).astype(BF16)


def _memattn(qm, mk, mv, tm):
    b, t, _ = qm.shape
    n_mem = mk.shape[1]
    return pl.pallas_call(
        _memattn_kernel,
        grid=(b, t // tm),
        in_specs=[pl.BlockSpec((1, tm, MD), lambda i, j: (i, j, 0)),
                  pl.BlockSpec((1, n_mem, MD), lambda i, j: (i, 0, 0)),
                  pl.BlockSpec((1, n_mem, MD), lambda i, j: (i, 0, 0))],
        out_specs=pl.BlockSpec((1, tm, MD), lambda i, j: (i, j, 0)),
        out_shape=jax.ShapeDtypeStruct((b, t, MD), BF16),
        compiler_params=_params(2),
        name="memattn",
    )(qm, mk, mv)


def _rglru_kernel(lx_ref, lg_ref, cs_ref, h0_ref, cw_ref, cb_ref, wa_ref, ba_ref, wi_ref, bi_ref, lam_ref,
                  y_ref, nb_ref, hl_ref, tail_ref, h_ref, xe_ref):
    t = pl.program_id(1)

    @pl.when(t == 0)
    def _():
        tail_ref[...] = cs_ref[0]
        h_ref[...] = h0_ref[0]

    x = lx_ref[0]
    tt = x.shape[0]
    xe_ref[:SUBLANES] = tail_ref[...]
    xe_ref[SUBLANES:] = x
    conv = cb_ref[...] + x * cw_ref[CONV_W - 1:CONV_W, :]
    for d in range(1, CONV_W):
        conv = conv + xe_ref[SUBLANES - d:SUBLANES - d + tt, :] * cw_ref[CONV_W - 1 - d:CONV_W - d, :]
    tail_ref[...] = x[tt - SUBLANES:]
    nb_ref[0] = x[tt - SUBLANES:]

    cb16 = conv.astype(BF16)
    r = jax.nn.sigmoid(jnp.dot(cb16, wa_ref[...], preferred_element_type=F32) + ba_ref[...])
    ig = jax.nn.sigmoid(jnp.dot(cb16, wi_ref[...], preferred_element_type=F32) + bi_ref[...])
    nl = -lam_ref[...]
    softplus = jnp.maximum(nl, 0.0) + jnp.log1p(jnp.exp(-jnp.abs(nl)))
    log_a = -LRU_C * r * softplus
    a = jnp.exp(log_a)
    b = jnp.sqrt(-jnp.tanh(log_a) * (a * a + 1.0)) * (ig * conv)
    in_group = lax.broadcasted_iota(I32, x.shape, 0) % SUBLANES
    k = 1
    while k < SUBLANES:
        keep = in_group >= k
        b = a * jnp.where(keep, pltpu.roll(b, k, 0), 0.0) + b
        a = a * jnp.where(keep, pltpu.roll(a, k, 0), 1.0)
        k *= 2
    h_prev = h_ref[SUBLANES - 1:SUBLANES, :]
    groups = []
    for g in range(tt // SUBLANES):
        rows = slice(g * SUBLANES, (g + 1) * SUBLANES)
        groups.append(b[rows] + a[rows] * h_prev)
        h_prev = groups[-1][SUBLANES - 1:SUBLANES, :]
    h = groups[0] if len(groups) == 1 else jnp.concatenate(groups, axis=0)
    h_ref[...] = h[tt - SUBLANES:]
    hl_ref[0] = h[tt - SUBLANES:]
    y_ref[0] = (h * jax.nn.gelu(lg_ref[0])).astype(BF16)


def _block_diag(w):
    nb, bs, _ = w.shape
    eye = jnp.eye(nb, dtype=w.dtype)
    return (eye[:, None, :, None] * w[:, :, None, :]).reshape(nb * bs, nb * bs)


def _prep_rglru(conv_w, conv_b, lru_wa, lru_ba, lru_wi, lru_bi, lru_lambda):
    r = lambda v: v.reshape(1, LRU_W)
    return [conv_w, r(conv_b), _block_diag(lru_wa).astype(BF16), r(lru_ba), _block_diag(lru_wi).astype(BF16),
            r(lru_bi), r(lru_lambda)]


def _rglru(lx, lg, conv_state, h0, consts, tt):
    b, t, w = lx.shape
    cs = jnp.concatenate([jnp.zeros((b, SUBLANES - (CONV_W - 1), w), F32), conv_state], axis=1)
    h0p = jnp.concatenate([jnp.zeros((b, SUBLANES - 1, w), F32), h0[:, None, :]], axis=1)
    seq = pl.BlockSpec((1, tt, w), lambda i, j: (i, j, 0))
    st = pl.BlockSpec((1, SUBLANES, w), lambda i, j: (i, 0, 0))
    y, nb, hl = pl.pallas_call(
        _rglru_kernel,
        grid=(b, t // tt),
        in_specs=[seq, seq, st, st] + [_const_spec(c.shape) for c in consts],
        out_specs=[seq, st, st],
        out_shape=[jax.ShapeDtypeStruct((b, t, w), BF16), jax.ShapeDtypeStruct((b, SUBLANES, w), F32),
                   jax.ShapeDtypeStruct((b, SUBLANES, w), F32)],
        scratch_shapes=[pltpu.VMEM((SUBLANES, w), F32), pltpu.VMEM((SUBLANES, w), F32),
                        pltpu.VMEM((SUBLANES + tt, w), F32)],
        compiler_params=pltpu.CompilerParams(dimension_semantics=("parallel", "arbitrary"),
                                             vmem_limit_bytes=VMEM_LIMIT),
        name="rglru",
    )(lx, lg, cs, h0p, *consts)
    return y, nb[:, SUBLANES - (CONV_W - 1):], hl[:, SUBLANES - 1]


def _kth_largest(count, below_max, rmin, rmax, n_valid, topk):
    kf = float(topk)
    n_open = lambda done: jnp.sum(1.0 - done).astype(I32)

    def bisect(_, c):
        lo, hi, done = c
        mid = 0.5 * lo + 0.5 * jnp.minimum(hi, rmax)
        c_mid = count(lambda x: x >= mid)
        ge = c_mid >= kf
        live = done < 0.5
        lo = jnp.where(live & ge, mid, lo)
        hi = jnp.where(live & jnp.logical_not(ge), mid, hi)
        done = jnp.maximum(done, jnp.where(c_mid == kf, 1.0, 0.0))
        return lo, hi, done

    done0 = jnp.where(n_valid <= topk, 1.0, 0.0)
    lo, hi, done = lax.fori_loop(0, BISECT_STEPS, bisect, (rmin, jnp.full(rmin.shape, jnp.inf, F32), done0))

    def step_down(c):
        lo, hi, done, tied, _ = c
        cand = below_max(hi)
        c_cand = count(lambda x: x >= cand)
        ok = c_cand >= kf
        live = done < 0.5
        lo = jnp.where(live & ok, cand, lo)
        hi = jnp.where(live & jnp.logical_not(ok), cand, hi)
        tied = jnp.maximum(tied, jnp.where(live & (c_cand > kf), 1.0, 0.0))
        done = jnp.maximum(done, jnp.where(ok, 1.0, 0.0))
        return lo, hi, done, tied, n_open(done)

    thr, _, _, tied, _ = lax.while_loop(lambda c: c[4] > 0, step_down,
                                        (lo, hi, done, jnp.zeros(rmin.shape, F32), n_open(done)))
    return thr, jnp.sum(tied).astype(I32)


def _dsa_prompt_kernel(qT_ref, qiT_ref, wT_ref, kw_ref, k_ref, vT_ref, tri_ref, o_ref,
                       sc_ref, qpad_ref, qipad_ref, oT_ref, s_scr, s2_scr, p_scr, p2_scr, *, topk):
    i = pl.program_id(1)
    nk = i + 1
    kf = float(topk)
    kblock = lambda j: pl.ds(pl.multiple_of(j * KB, KB), KB)

    zeros64 = jnp.zeros((D_IDX, TQ), BF16)
    for h in range(H_IDX):
        qipad_ref[h] = jnp.concatenate([qiT_ref[0, h * D_IDX:(h + 1) * D_IDX, :], zeros64], axis=0)
    for h in range(H_ATT):
        qh = qT_ref[0, h * DH_ATT:(h + 1) * DH_ATT, :]
        qpad_ref[h] = jnp.concatenate([qh, zeros64] if h % 2 == 0 else [zeros64, qh], axis=0)
    wT = wT_ref[0]

    def block_scores(j, n_blocks):
        rows = pl.ds(pl.multiple_of(j * KB, KB), n_blocks * KB)
        kw = kw_ref[0, rows, :]
        acc = jnp.zeros((n_blocks * KB, TQ), F32)
        for h in range(H_IDX):
            d = jnp.dot(kw, qipad_ref[h], preferred_element_type=F32)
            acc = acc + jnp.maximum(d, 0.0) * wT[h:h + 1, :]
        return rows, acc

    def score_blocks(n_blocks):
        def body(jj, carry):
            lo, hi = carry
            rows, acc = block_scores(jj * n_blocks, n_blocks)
            sc_ref[rows, :] = acc
            return (jnp.minimum(lo, jnp.min(acc, axis=0, keepdims=True)),
                    jnp.maximum(hi, jnp.max(acc, axis=0, keepdims=True)))
        return body

    bounds = (jnp.full((1, TQ), jnp.inf, F32), jnp.full((1, TQ), NEG_INF, F32))
    first = 0
    for n_blocks in SCORE_BLOCKS:
        trips = (i - first) // n_blocks
        bounds = lax.fori_loop(first // n_blocks, first // n_blocks + trips, score_blocks(n_blocks), bounds)
        first = first + trips * n_blocks
    rmin, rmax = bounds
    _, acc = block_scores(i, 1)
    krow = lax.broadcasted_iota(I32, (KB, TQ), 0)
    qcol = lax.broadcasted_iota(I32, (KB, TQ), 1)
    sc_ref[kblock(i), :] = jnp.where(krow <= qcol, acc, NEG_INF)
    rmin = jnp.minimum(rmin, jnp.min(acc, axis=0, keepdims=True))
    rmax = jnp.maximum(rmax, jnp.max(acc, axis=0, keepdims=True))

    def count(pred):
        def body(j, cnt):
            m = jnp.where(pred(sc_ref[kblock(j), :]), 1.0, 0.0)
            for r in range(KB // CNT_ROWS):
                cnt = cnt + m[r * CNT_ROWS:(r + 1) * CNT_ROWS]
            return cnt

        cnt = lax.fori_loop(0, nk, body, jnp.zeros((CNT_ROWS, TQ), F32))
        return jnp.sum(cnt, axis=0, keepdims=True)

    def below_max(hi):
        def body(j, best):
            x = sc_ref[kblock(j), :]
            return jnp.maximum(best, jnp.max(jnp.where(x < hi, x, NEG_INF), axis=0, keepdims=True))

        return lax.fori_loop(0, nk, body, jnp.full((1, TQ), NEG_INF, F32))

    n_valid = i * TQ + lax.broadcasted_iota(I32, (1, TQ), 1) + 1
    thr, n_tied = _kth_largest(count, below_max, rmin, rmax, n_valid, topk)

    @pl.when(n_tied == 0)
    def _():
        def body(j, carry):
            x = sc_ref[kblock(j), :]
            sc_ref[kblock(j), :] = jnp.where(x >= thr, 0.0, NEG_INF)
            return carry

        lax.fori_loop(0, nk, body, 0)

    @pl.when(n_tied > 0)
    def _():
        quota = kf - count(lambda x: x > thr)

        def body(j, ties_before):
            x = sc_ref[kblock(j), :]
            eq = jnp.where(x == thr, 1.0, 0.0)
            rank = jnp.dot(tri_ref[...], eq.astype(BF16), preferred_element_type=F32) + ties_before
            sel = (x > thr) | ((x == thr) & (rank < quota))
            sc_ref[kblock(j), :] = jnp.where(sel, 0.0, NEG_INF)
            return ties_before + jnp.sum(eq, axis=0, keepdims=True)

        lax.fori_loop(0, nk, body, jnp.zeros((1, TQ), F32))

    n_pairs = (nk + 1) // 2

    @pl.when(nk % 2 == 1)
    def _():
        sc_ref[kblock(nk), :] = jnp.full((KB, TQ), NEG_INF, F32)

    def score_phase(j, s_out):
        block_max = []
        for h in range(H_ATT):
            pair = slice((h // 2) * LANES, (h // 2 + 1) * LANES)
            s = jnp.dot(k_ref[0, kblock(j), pair], qpad_ref[h], preferred_element_type=F32)
            s = s + sc_ref[kblock(j), :]
            s_out[h] = s
            block_max.append(jnp.max(s, axis=0, keepdims=True))
        return tuple(block_max)

    ones_rows = (lax.broadcasted_iota(I32, (SUM_ROWS, KB), 0) == 0).astype(BF16)

    def value_phase(j, p_in, alphas, accs):
        new_accs = []
        for h in range(H_ATT):
            hrows = slice(h * DH_ATT, (h + 1) * DH_ATT)
            v_aug = jnp.concatenate([vT_ref[0, j, hrows, :], ones_rows], axis=0)
            pv = jnp.dot(v_aug, p_in[h], preferred_element_type=F32)
            new_accs.append(alphas[h] * accs[h] + pv)
        return tuple(new_accs)

    def block_step(j, next_j, s_in, s_out, p_prev, p_out, carry):
        ms, accs, block_max, alphas_prev = carry
        next_max = score_phase(next_j, s_out)
        new_ms, alphas = [], []
        for h in range(H_ATT):
            m_new = jnp.maximum(ms[h], block_max[h])
            m_safe = jnp.where(m_new == NEG_INF, 0.0, m_new)
            alphas.append(jnp.exp2(ms[h] - m_safe))
            new_ms.append(m_new)
            p_out[h] = jnp.exp2((s_in[h] - m_safe).astype(BF16))
        accs = value_phase(jnp.maximum(j - 1, 0), p_prev, alphas_prev, accs)
        return tuple(new_ms), accs, next_max, tuple(alphas)

    def attend_pair(jj, carry):
        j0 = 2 * jj
        carry = block_step(j0, j0 + 1, s_scr, s2_scr, p2_scr, p_scr, carry)
        return block_step(j0 + 1, jnp.minimum(j0 + 2, 2 * n_pairs - 1), s2_scr, s_scr, p_scr, p2_scr, carry)

    p2_scr[...] = jnp.zeros(p2_scr.shape, BF16)
    init = (tuple(jnp.full((1, TQ), NEG_INF, F32) for _ in range(H_ATT)),
            tuple(jnp.zeros((DH_ATT + SUM_ROWS, TQ), F32) for _ in range(H_ATT)),
            score_phase(0, s_scr),
            tuple(jnp.ones((1, TQ), F32) for _ in range(H_ATT)))
    _, accs, _, alphas = lax.fori_loop(0, n_pairs, attend_pair, init)
    accs = value_phase(2 * n_pairs - 1, p2_scr, alphas, accs)
    for h in range(H_ATT):
        oT_ref[h * DH_ATT:(h + 1) * DH_ATT, :] = accs[h][:DH_ATT] / accs[h][DH_ATT:DH_ATT + 1]
    o_ref[0] = oT_ref[...].T.astype(BF16)


def _dsa_prompt(qT, qiT, wT, kwb, kb, vTb):
    b, _, s = qT.shape
    assert s % (2 * KB) == 0 and TQ == KB
    topk = min(TOPK_MAX, s // 4)
    tri = (jnp.arange(KB)[:, None] > jnp.arange(KB)[None, :]).astype(BF16)
    feat = lambda w: pl.BlockSpec((1, w, TQ), lambda i, j: (i, 0, j))
    full = lambda w: pl.BlockSpec((1, s, w), lambda i, j: (i, 0, 0))
    return pl.pallas_call(
        functools.partial(_dsa_prompt_kernel, topk=topk),
        grid=(b, s // TQ),
        in_specs=[feat(HD), feat(HD), feat(H_IDX), full(LANES), full(HD),
                  pl.BlockSpec((1, s // KB, HD, KB), lambda i, j: (i, 0, 0, 0)), _const_spec(tri.shape)],
        out_specs=pl.BlockSpec((1, TQ, HD), lambda i, j: (i, j, 0)),
        out_shape=jax.ShapeDtypeStruct((b, s, HD), BF16),
        scratch_shapes=[pltpu.VMEM((s, TQ), F32), pltpu.VMEM((H_ATT, LANES, TQ), BF16),
                        pltpu.VMEM((H_IDX, LANES, TQ), BF16), pltpu.VMEM((HD, TQ), F32),
                        pltpu.VMEM((H_ATT, KB, TQ), F32), pltpu.VMEM((H_ATT, KB, TQ), F32),
                        pltpu.VMEM((H_ATT, KB, TQ), BF16), pltpu.VMEM((H_ATT, KB, TQ), BF16)],
        compiler_params=_params(2),
        name="dsa_prompt",
    )(qT, qiT, wT, kwb, kb, vTb, tri)


PAGE_GROUP_IDX = 64
PAGE_GROUP_KV = 32
PAGE_PARTIALS = 8


def _dsa_sample_select_kernel(pt_ref, qi_ref, kwq_ref, tri_ref, *rest, n_pages, topk, group):
    page_refs, (bias_ref, biasn_ref, qiall_ref, wb_ref) = rest[:group], rest[group:]
    pg = pl.program_id(1)
    t = qi_ref.shape[1]
    kwq = kwq_ref[0]

    @pl.when(pg == 0)
    def _():
        qi = qi_ref[0].astype(F32)
        qiall_ref[...] = jnp.concatenate(
            [qi[:, h * D_IDX:(h + 1) * D_IDX] for h in range(H_IDX)], axis=0).astype(BF16)
        for h in range(H_IDX):
            wb_ref[h] = jnp.broadcast_to(kwq[:, D_IDX + h:D_IDX + h + 1], (t, LANES))

    def scores(dots):
        acc = jnp.zeros((t, dots.shape[1]), F32)
        for h in range(H_IDX):
            acc = acc + jnp.maximum(dots[h * t:(h + 1) * t], 0.0) * wb_ref[h]
        return acc

    for g in range(group):
        dots = jnp.dot(qiall_ref[...], page_refs[g][0].astype(BF16), preferred_element_type=F32)
        bias_ref[0, pg * group + g] = scores(dots)

    @pl.when(pg == pl.num_programs(1) - 1)
    def _():
        new_keys = jnp.concatenate([kwq[:, :D_IDX], jnp.zeros((LANES - t, D_IDX), F32)], axis=0).astype(BF16)
        row = lax.broadcasted_iota(I32, (t, LANES), 0)
        col = lax.broadcasted_iota(I32, (t, LANES), 1)
        dots_new = lax.dot_general(qiall_ref[...], new_keys, NT_DIMS, preferred_element_type=F32)
        raw_new = scores(dots_new)
        sc_new = jnp.where(col <= row, raw_new, NEG_INF)

        def over_pages(fn, first, combine):
            parts = [first]
            for p in range(n_pages):
                v = fn(bias_ref[0, p])
                if len(parts) < PAGE_PARTIALS:
                    parts.append(v)
                else:
                    parts[p % PAGE_PARTIALS] = combine(parts[p % PAGE_PARTIALS], v)
            return functools.reduce(combine, parts)

        def count(pred):
            hit = lambda x: jnp.where(pred(x), 1.0, 0.0)
            return jnp.sum(over_pages(hit, hit(sc_new), jnp.add), axis=1, keepdims=True)

        def below_max(hi):
            below = lambda x: jnp.where(x < hi, x, NEG_INF)
            return jnp.max(over_pages(below, below(sc_new), jnp.maximum), axis=1, keepdims=True)

        ident = lambda x: x
        rmin = jnp.min(over_pages(ident, jnp.where(col <= row, raw_new, jnp.inf), jnp.minimum),
                       axis=1, keepdims=True)
        rmax = jnp.max(over_pages(ident, sc_new, jnp.maximum), axis=1, keepdims=True)
        n_valid = n_pages * bias_ref.shape[3] + lax.broadcasted_iota(I32, (t, 1), 0) + 1
        thr, n_tied = _kth_largest(count, below_max, rmin, rmax, n_valid, topk)
        as_bias = lambda sel: jnp.where(sel, 0.0, NEG_INF)

        @pl.when(n_tied == 0)
        def _():
            bias_ref[0] = as_bias(bias_ref[0] >= thr)
            biasn_ref[0] = as_bias(sc_new >= thr)

        @pl.when(n_tied > 0)
        def _():
            quota = topk - count(lambda x: x > thr)

            def select(x, ties_before):
                eq = x == thr
                eqf = jnp.where(eq, 1.0, 0.0)
                rank = jnp.dot(eqf.astype(BF16), tri_ref[...], preferred_element_type=F32) + ties_before
                sel = (x > thr) | (eq & (rank < quota))
                return as_bias(sel), ties_before + jnp.sum(eqf, axis=1, keepdims=True)

            def bias_page(p, ties_before):
                bias, ties = select(bias_ref[0, p], ties_before)
                bias_ref[0, p] = bias
                return ties

            ties = lax.fori_loop(0, n_pages, bias_page, jnp.zeros((t, 1), F32))
            biasn_ref[0], _ = select(sc_new, ties)


def _dsa_sample_select(page_table, qib, kw32, cache_kidx_t):
    b, t, _ = qib.shape
    n_pages = page_table.shape[1]
    page = cache_kidx_t.shape[2]
    topk = min(TOPK_MAX, (n_pages * page + t) // 4)
    group = min(PAGE_GROUP_IDX, n_pages)
    assert n_pages % group == 0
    tri = (jnp.arange(page)[:, None] < jnp.arange(page)[None, :]).astype(BF16)
    tok = lambda w: pl.BlockSpec((1, t, w), lambda i, j, pt: (i, 0, 0))
    page_specs = [pl.BlockSpec((1, D_IDX, page), lambda i, j, pt, g=g: (pt[i, j * group + g], 0, 0))
                  for g in range(group)]
    grid_spec = pltpu.PrefetchScalarGridSpec(
        num_scalar_prefetch=1,
        grid=(b, n_pages // group),
        in_specs=[tok(H_IDX * D_IDX), tok(LANES), pl.BlockSpec(tri.shape, lambda i, j, pt: (0, 0))] + page_specs,
        out_specs=[pl.BlockSpec((1, n_pages, t, page), lambda i, j, pt: (i, 0, 0, 0)),
                   pl.BlockSpec((1, t, LANES), lambda i, j, pt: (i, 0, 0))],
        scratch_shapes=[pltpu.VMEM((H_IDX * t, D_IDX), BF16), pltpu.VMEM((H_IDX, t, LANES), F32)],
    )
    return pl.pallas_call(
        functools.partial(_dsa_sample_select_kernel, n_pages=n_pages, topk=topk, group=group),
        grid_spec=grid_spec,
        out_shape=[jax.ShapeDtypeStruct((b, n_pages, t, page), F32), jax.ShapeDtypeStruct((b, t, LANES), F32)],
        compiler_params=pltpu.CompilerParams(dimension_semantics=("parallel", "arbitrary"),
                                             vmem_limit_bytes=VMEM_LIMIT),
        name="dsa_sample_select",
    )(page_table, qib, kw32, tri, *([cache_kidx_t] * group))


def _dsa_sample_attend_kernel(pt_ref, q_ref, kn_ref, vn_ref, bias_ref, biasn_ref, *rest, group):
    kT_refs, vT_refs = rest[:group], rest[group:2 * group]
    o_ref, m_ref, l_ref, acc_ref = rest[2 * group:]
    pg = pl.program_id(1)
    t = q_ref.shape[1]
    page = kT_refs[0].shape[2]
    lane_head = lax.broadcasted_iota(I32, (t, HD), 1) // DH_ATT
    q = q_ref[0].astype(F32)
    qbd = jnp.concatenate([jnp.where(lane_head == h, q, 0.0) for h in range(H_ATT)], axis=0).astype(BF16)

    @pl.when(pg == 0)
    def _():
        m_ref[...] = jnp.full(m_ref.shape, NEG_INF, F32)
        l_ref[...] = jnp.zeros(l_ref.shape, F32)
        acc_ref[...] = jnp.zeros(acc_ref.shape, F32)

    def update(s, bias, pv):
        s = s + jnp.concatenate([bias] * H_ATT, axis=0)
        m = m_ref[...]
        m_new = jnp.maximum(m, jnp.max(s, axis=1, keepdims=True))
        m_safe = jnp.where(m_new == NEG_INF, 0.0, m_new)
        alpha = jnp.exp(m - m_safe)
        p = jnp.exp(s - m_safe)
        l_ref[...] = alpha * l_ref[...] + jnp.sum(p, axis=1, keepdims=True)
        acc_ref[...] = alpha * acc_ref[...] + pv(p.astype(BF16))
        m_ref[...] = m_new

    s_pages = jnp.concatenate(
        [jnp.dot(qbd, kT_refs[g][0].astype(BF16), preferred_element_type=F32) for g in range(group)], axis=1)
    bias_pages = jnp.concatenate([bias_ref[0, g] for g in range(group)], axis=1)

    def pv_pages(p):
        out = jnp.zeros((H_ATT * t, HD), F32)
        for g in range(group):
            out = out + lax.dot_general(p[:, g * page:(g + 1) * page], vT_refs[g][0].astype(BF16), NT_DIMS,
                                        preferred_element_type=F32)
        return out

    update(s_pages, bias_pages, pv_pages)

    @pl.when(pg == pl.num_programs(1) - 1)
    def _():
        pad = jnp.zeros((LANES - t, HD), F32)
        kn = jnp.concatenate([kn_ref[0], pad], axis=0).astype(BF16)
        vn = jnp.concatenate([vn_ref[0], pad], axis=0).astype(BF16)
        update(lax.dot_general(qbd, kn, NT_DIMS, preferred_element_type=F32), biasn_ref[0],
               lambda p: jnp.dot(p, vn, preferred_element_type=F32))
        o = acc_ref[...] / l_ref[...]
        out = jnp.zeros((t, HD), F32)
        for h in range(H_ATT):
            out = out + jnp.where(lane_head == h, o[h * t:(h + 1) * t], 0.0)
        o_ref[0] = out.astype(BF16)


def _dsa_sample_attend(page_table, qb, k32, v32, bias, bias_new, cache_kt, cache_vt):
    b, t, _ = qb.shape
    n_pages = page_table.shape[1]
    page = cache_kt.shape[2]
    group = min(PAGE_GROUP_KV, n_pages)
    assert n_pages % group == 0
    tok = lambda w: pl.BlockSpec((1, t, w), lambda i, j, pt: (i, 0, 0))
    kv_specs = [pl.BlockSpec((1, HD, page), lambda i, j, pt, g=g: (pt[i, j * group + g], 0, 0))
                for g in range(group)]
    grid_spec = pltpu.PrefetchScalarGridSpec(
        num_scalar_prefetch=1,
        grid=(b, n_pages // group),
        in_specs=[tok(HD), tok(HD), tok(HD),
                  pl.BlockSpec((1, group, t, page), lambda i, j, pt: (i, j, 0, 0)),
                  tok(LANES)] + kv_specs + kv_specs,
        out_specs=tok(HD),
        scratch_shapes=[pltpu.VMEM((H_ATT * t, 1), F32), pltpu.VMEM((H_ATT * t, 1), F32),
                        pltpu.VMEM((H_ATT * t, HD), F32)],
    )
    return pl.pallas_call(
        functools.partial(_dsa_sample_attend_kernel, group=group),
        grid_spec=grid_spec,
        out_shape=jax.ShapeDtypeStruct((b, t, HD), BF16),
        compiler_params=pltpu.CompilerParams(dimension_semantics=("parallel", "arbitrary"),
                                             vmem_limit_bytes=VMEM_LIMIT),
        name="dsa_sample_attend",
    )(page_table, qb, k32, v32, bias, bias_new, *([cache_kt] * group), *([cache_vt] * group))


def _merge_kernel(x_ref, oa_ref, ol_ref, om_ref, g_ref, wg_ref, wa_ref, wl_ref, wm_ref, wo_ref, o_ref):
    x = x_ref[...]
    hn = _rms(x, g_ref[...]).astype(BF16)
    m = jnp.zeros_like(x)
    for idx, (o_r, w_r) in enumerate(((oa_ref, wa_ref), (ol_ref, wl_ref), (om_ref, wm_ref))):
        gate = jax.nn.sigmoid(jnp.dot(hn, wg_ref[:, idx * D_MODEL:(idx + 1) * D_MODEL], preferred_element_type=F32))
        m = m + gate * jnp.dot(o_r[...], w_r[...], preferred_element_type=F32)
    o_ref[...] = x + jnp.dot(m.astype(BF16), wo_ref[...], preferred_element_type=F32)


def _merge(x, o_att, o_lru, o_mem, consts, tm):
    n = x.shape[0]
    row = lambda w: pl.BlockSpec((tm, w), lambda i: (i, 0))
    return pl.pallas_call(
        _merge_kernel,
        grid=(n // tm,),
        in_specs=[row(D_MODEL), row(HD), row(LRU_W), row(MD)] + [_const_spec(c.shape) for c in consts],
        out_specs=row(D_MODEL),
        out_shape=jax.ShapeDtypeStruct((n, D_MODEL), F32),
        compiler_params=_params(1),
        name="merge",
    )(x, o_att, o_lru, o_mem, *consts)


def _token_tile(n):
    return min(n, 512)


def _layer(x, is_prompt, lw, pp, extra):
    b, t, _ = x.shape
    n = b * t
    tm = _token_tile(n)
    x1 = _ffn(x.reshape(n, D_MODEL), pp["ffn1"], tm)
    r3 = lambda a: a.reshape(b, t, a.shape[-1])
    if is_prompt:
        kT32, vT32, kiT32, wT, qT, qiT, kb, kwb, vTb, lx, lg, qmb = _proj_t(r3(x1), pp["proj"], tm)
        o_att = _dsa_prompt(qT, qiT, wT, kwb, kb, vTb)
        k_new = kT32.reshape(b, H_ATT, DH_ATT, t).transpose(0, 3, 1, 2)
        v_new = vT32.reshape(b, H_ATT, DH_ATT, t).transpose(0, 3, 1, 2)
        ki_new = kiT32.transpose(0, 2, 1)
        conv_state = jnp.zeros((b, CONV_W - 1, LRU_W), F32)
        h0 = jnp.zeros((b, LRU_W), F32)
        mem = extra["mem"]
        mk, mv = _memkv(mem.reshape(-1, D_MODEL), lw["norm_mem_g"], lw["w_mem_kv"], lw["mem_k_norm_g"],
                        pp["proj"]["g128"])
        mk = mk.reshape(b, -1, MD)
        mv = mv.reshape(b, -1, MD)
    else:
        k32, v32, kw32, qb, _, _, qib, _, lx, lg, qmb = _proj(x1, pp["proj"], tm)
        pt = extra["page_table"]
        bias, bias_new = _dsa_sample_select(pt, r3(qib), r3(kw32), extra["cache_kidx_t"])
        o_att = _dsa_sample_attend(pt, r3(qb), r3(k32), r3(v32), bias, bias_new, extra["cache_kt"],
                                   extra["cache_vt"])
        k_new = k32.reshape(b, t, H_ATT, DH_ATT)
        v_new = v32.reshape(b, t, H_ATT, DH_ATT)
        ki_new = r3(kw32)[:, :, :D_IDX]
        conv_state, h0 = extra["state_conv"], extra["state_h"]
        mk, mv = extra["cache_mem_k"], extra["cache_mem_v"]
    o_lru, conv_buf, h_last = _rglru(r3(lx), r3(lg), conv_state, h0, pp["rglru"], min(t, 256))
    o_mem = _memattn(r3(qmb), mk, mv, min(t, 512))
    x2 = _merge(x1, o_att.reshape(n, HD), o_lru.reshape(n, LRU_W), o_mem.reshape(n, MD), pp["merge"], tm)
    y = _ffn(x2, pp["ffn2"], tm).reshape(b, t, D_MODEL)
    if is_prompt:
        state = (k_new, v_new, ki_new, mk.reshape(b, -1, H_MEM, DH_MEM), mv.reshape(b, -1, H_MEM, DH_MEM),
                 conv_buf, h_last)
    else:
        state = (k_new, v_new, ki_new, conv_buf, h_last)
    return y, state


def kernel(x_prompt, x_sample, cache_k, cache_v, cache_kidx, page_table, cache_mem_k, cache_mem_v, state_conv, state_h, mem_prompt, norm_ffn1_g, w_ffn1_in, w_ffn1_out, norm_mix_g, w_in, q_norm_g, k_norm_g, w_attn_o, conv_w, conv_b, lru_wa, lru_ba, lru_wi, lru_bi, lru_lambda, w_lru_o, norm_mem_g, w_mem_kv, mem_q_norm_g, mem_k_norm_g, w_mem_o, w_out, norm_ffn2_g, w_ffn2_in, w_ffn2_out):
    depth = w_in.shape[0]
    n_phys, page = cache_k.shape[1], cache_k.shape[2]
    xp, xs = x_prompt, x_sample
    p_states, s_states = [], []
    for l in range(depth):
        proj = _prep_proj(norm_mix_g[l], w_in[l], q_norm_g[l], k_norm_g[l], mem_q_norm_g[l])
        pp = dict(
            ffn1=_prep_ffn(norm_ffn1_g[l], w_ffn1_in[l], w_ffn1_out[l]),
            ffn2=_prep_ffn(norm_ffn2_g[l], w_ffn2_in[l], w_ffn2_out[l]),
            proj=proj,
            rglru=_prep_rglru(conv_w[l], conv_b[l], lru_wa[l], lru_ba[l], lru_wi[l], lru_bi[l], lru_lambda[l]),
            merge=[proj["g"], proj["wgates"], w_attn_o[l].astype(BF16), w_lru_o[l].astype(BF16),
                   w_mem_o[l].astype(BF16), w_out[l].astype(BF16)],
        )
        lw = dict(norm_mem_g=norm_mem_g[l], w_mem_kv=w_mem_kv[l], mem_k_norm_g=mem_k_norm_g[l])
        xp, st_p = _layer(xp, True, lw, pp, dict(mem=mem_prompt))
        xs, st_s = _layer(xs, False, lw, pp, dict(
            page_table=page_table,
            cache_kt=cache_k[l].transpose(0, 2, 3, 1).reshape(n_phys, HD, page),
            cache_vt=cache_v[l].transpose(0, 2, 3, 1).reshape(n_phys, HD, page),
            cache_kidx_t=cache_kidx[l].transpose(0, 2, 1), cache_mem_k=cache_mem_k[l].reshape(-1, cache_mem_k.shape[2], MD),
            cache_mem_v=cache_mem_v[l].reshape(-1, cache_mem_v.shape[2], MD),
            state_conv=state_conv[l], state_h=state_h[l]))
        p_states.append(st_p)
        s_states.append(st_s)
    stack = lambda states, i: jnp.stack([s[i] for s in states])
    return (xp, xs) + tuple(stack(p_states, i) for i in range(7)) + tuple(stack(s_states, i) for i in range(5))
```
